```python
import functools
import jax, jax.numpy as jnp
from jax import lax
import numpy as np

D_MODEL = 2048
BATCH = 16
SEQ = 256
DEPTH = 2
DEC_BATCH = 4
DEC_SEQ = 1024
PAST_LEN = 256

GRID_W = 64
GLA_WIDTH = D_MODEL // 2
CONV_WIDTH = D_MODEL - GLA_WIDTH
GLA_HEADS = 4
GLA_DV = GLA_WIDTH // GLA_HEADS
GLA_DK = GLA_DV // 2
Q_COLS = GLA_HEADS * GLA_DK
GLA_RANK = 16
GLA_TAU = 16.0
GLA_CHUNK = 64
CONV_K = 31
D_FF = 7 * D_MODEL // 2
N_EXPERTS = 8
TOP_K = 2
N_DENSE = (DEPTH + 1) // 2
N_MOE = DEPTH // 2
ALPHA = (2 * DEPTH) ** 0.25
BETA = (8 * DEPTH) ** -0.25
LN_EPS = 1e-5
RMS_EPS = 1e-6
SPLITS = (Q_COLS, 2 * Q_COLS, 2 * Q_COLS + GLA_WIDTH, 2 * Q_COLS + 2 * GLA_WIDTH,
          2 * Q_COLS + 2 * GLA_WIDTH + 2 * GLA_RANK)
IN_COLS = SPLITS[-1] + 2 * CONV_WIDTH

kernel_name = "hybrid_gla_conformer_diffusion_step"


def _layer_norm(x, g, b):
    xf = x.astype(jnp.float32)
    mu = jnp.mean(xf, -1, keepdims=True)
    var = jnp.mean(jnp.square(xf - mu), -1, keepdims=True)
    return ((xf - mu) * lax.rsqrt(var + LN_EPS) * g.astype(jnp.float32) + b.astype(jnp.float32)).astype(x.dtype)


def _rms_norm(x, g):
    xf = x.astype(jnp.float32)
    return (xf * lax.rsqrt(jnp.mean(xf * xf, -1, keepdims=True) + RMS_EPS) * g.astype(jnp.float32)).astype(x.dtype)


def _grid_pos_emb(n_tokens):
    rows = n_tokens // GRID_W
    r = jnp.broadcast_to(jnp.arange(rows, dtype=jnp.float32)[:, None], (rows, GRID_W)).reshape(-1)
    col = jnp.broadcast_to(jnp.arange(GRID_W, dtype=jnp.float32)[None, :], (rows, GRID_W)).reshape(-1)
    quarter = D_MODEL // 4
    freqs = 1.0 / (10000.0 ** (jnp.arange(quarter, dtype=jnp.float32) / quarter))
    def enc(p):
        ang = p[:, None] * freqs[None, :]
        return jnp.concatenate([jnp.sin(ang), jnp.cos(ang)], -1)
    return jnp.concatenate([enc(r), enc(col)], -1)


def _gla_chunked(q, k, v, log_a, s0):
    B, L, H, DK = q.shape
    DV = v.shape[-1]
    C = GLA_CHUNK
    N = L // C
    q = q.reshape(B, N, C, H, DK)
    k = k.reshape(B, N, C, H, DK)
    v = v.reshape(B, N, C, H, DV)
    b = jnp.cumsum(log_a.astype(jnp.float32).reshape(B, N, C, H, DK), axis=2)
    b_last = b[:, :, -1:]
    qd = q * jnp.exp(b)
    kd = k * jnp.exp(-b)
    kl = k * jnp.exp(b_last - b)
    causal = jnp.tril(jnp.ones((C, C), dtype=bool))
    att = jnp.where(causal, jnp.einsum('bnihd,bnjhd->bnhij', qd, kd), 0.0)
    o_intra = jnp.einsum('bnhij,bnjhv->bnihv', att, v)
    ds = jnp.einsum('bnjhd,bnjhv->bnhdv', kl, v)
    decay = jnp.exp(b_last[:, :, 0])

    def step(s, inp):
        dec, d = inp
        return dec[..., None] * s + d, s

    s_final, s_prev = lax.scan(step, s0.astype(jnp.float32),
                               (jnp.moveaxis(decay, 1, 0), jnp.moveaxis(ds, 1, 0)))
    s_prev = jnp.moveaxis(s_prev, 0, 1)
    o_inter = jnp.einsum('bnihd,bnhdv->bnihv', qd, s_prev)
    o = (o_intra + o_inter).reshape(B, L, H, DV).astype(v.dtype)
    return o, s_final


def _gla_bidir(q, k, v, la_f, la_b, s0_f, s0_b):
    flip = lambda t: jnp.flip(t, axis=1)
    o_f, s_f = _gla_chunked(q, k, v, la_f, s0_f)
    o_b, s_b = _gla_chunked(flip(q), flip(k), flip(v), flip(la_b), s0_b)
    return o_f + flip(o_b), s_f, s_b


def _mixer(h, s0, w_in, w_a2, b_a2, gla_norm_g, conv_w, conv_b, conv_ln_g, conv_ln_b, w_pw2, b_pw2, w_out):
    B, L, _ = h.shape
    z = h @ w_in
    q, k, v, g, a_lr, u = jnp.split(z, SPLITS, axis=-1)
    q = q.reshape(B, L, GLA_HEADS, GLA_DK) * (GLA_DK ** -0.5)
    k = k.reshape(B, L, GLA_HEADS, GLA_DK)
    v = v.reshape(B, L, GLA_HEADS, GLA_DV)
    a_lr = a_lr.reshape(B, L, 2, GLA_RANK)
    gate_logits = jnp.einsum('blzr,zrk->blzk', a_lr, w_a2) + b_a2
    log_a = (jax.nn.log_sigmoid(gate_logits.astype(jnp.float32)) / GLA_TAU).reshape(B, L, 2, GLA_HEADS, GLA_DK)
    o, s_f, s_b = _gla_bidir(q, k, v, log_a[:, :, 0], log_a[:, :, 1], s0[:, 0], s0[:, 1])
    o = _rms_norm(o, gla_norm_g).reshape(B, L, GLA_WIDTH) * jax.nn.silu(g)
    ua, ug = jnp.split(u, 2, axis=-1)
    u = ua * jax.nn.sigmoid(ug)
    u = lax.conv_general_dilated(u, conv_w.astype(u.dtype).reshape(CONV_K, 1, CONV_WIDTH),
                                 window_strides=(1,), padding=[(CONV_K // 2, CONV_K // 2)],
                                 dimension_numbers=('NWC', 'WIO', 'NWC'),
                                 feature_group_count=CONV_WIDTH) + conv_b
    u = jax.nn.silu(_layer_norm(u, conv_ln_g, conv_ln_b))
    u = u @ w_pw2 + b_pw2
    y = jnp.concatenate([o, u.astype(o.dtype)], axis=-1) @ w_out
    return y, jnp.stack([s_f, s_b], axis=1)


def _swiglu(h, w1, w3, w2):
    return (jax.nn.silu(h @ w1) * (h @ w3)) @ w2


def _moe(h, w_r, b_r, w1, w3, w2):
    logits = (h @ w_r + b_r).astype(jnp.float32)
    top_v, top_i = lax.top_k(logits, TOP_K)
    wts = jax.nn.softmax(top_v, axis=-1)
    gates = jnp.sum(jax.nn.one_hot(top_i, N_EXPERTS, dtype=jnp.float32) * wts[..., None], axis=-2)
    out = jnp.zeros_like(h)
    for e in range(N_EXPERTS):
        out = out + gates[..., e:e + 1].astype(h.dtype) * _swiglu(h, w1[e], w3[e], w2[e])
    return out


def _layer(x, cond, s0, w_mod, b_mod, mix, ln1_g, ln1_b, ln2_g, ln2_b, ffn):
    mod = (jax.nn.silu(cond) @ w_mod + b_mod).reshape(-1, 1, 6, D_MODEL)
    sh1, sc1, g1, sh2, sc2, g2 = (mod[:, :, i] for i in range(6))
    h = x * (1 + sc1) + sh1
    y, s_new = _mixer(h, s0, *mix)
    x = _layer_norm(ALPHA * x + g1 * y, ln1_g, ln1_b)
    h = x * (1 + sc2) + sh2
    x = _layer_norm(ALPHA * x + g2 * ffn(h), ln2_g, ln2_b)
    return x, s_new


def setup_inputs(seed: int = 0) -> dict:
    key = jax.random.key(seed)
    ks = iter(jax.random.split(key, 40))
    nrm = lambda shape, s: jax.random.normal(next(ks), shape, jnp.float32) * s
    D = D_MODEL
    return {
        "x_prompt": nrm((BATCH, SEQ, D), 1.0),
        "x_sample": nrm((DEC_BATCH, DEC_SEQ, D), 1.0),
        "state_gla": nrm((DEC_BATCH, DEPTH, 2, GLA_HEADS, GLA_DK, GLA_DV), 1.0),
        "c": nrm((DEC_BATCH, D), 1.0),
        "c_ctx": nrm((D,), 1.0),
        "w_mod": nrm((DEPTH, D, 6 * D), 0.5 * D ** -0.5),
        "b_mod": nrm((DEPTH, 6 * D), 0.02),
        "w_in": nrm((DEPTH, D, IN_COLS), D ** -0.5),
        "w_a2": nrm((DEPTH, 2, GLA_RANK, Q_COLS), GLA_RANK ** -0.5),
        "b_a2": nrm((DEPTH, 2, Q_COLS), 0.02),
        "gla_norm_g": 1.0 + nrm((DEPTH, GLA_DV), 0.02),
        "conv_w": nrm((DEPTH, CONV_K, CONV_WIDTH), CONV_K ** -0.5),
        "conv_b": nrm((DEPTH, CONV_WIDTH), 0.02),
        "conv_ln_g": 1.0 + nrm((DEPTH, CONV_WIDTH), 0.02),
        "conv_ln_b": nrm((DEPTH, CONV_WIDTH), 0.02),
        "w_pw2": nrm((DEPTH, CONV_WIDTH, CONV_WIDTH), CONV_WIDTH ** -0.5),
        "b_pw2": nrm((DEPTH, CONV_WIDTH), 0.02),
        "w_out": nrm((DEPTH, GLA_WIDTH + CONV_WIDTH, D), BETA * (GLA_WIDTH + CONV_WIDTH) ** -0.5),
        "ln1_g": 1.0 + nrm((DEPTH, D), 0.02),
        "ln1_b": nrm((DEPTH, D), 0.02),
        "ln2_g": 1.0 + nrm((DEPTH, D), 0.02),
        "ln2_b": nrm((DEPTH, D), 0.02),
        "ffn_w1": nrm((N_DENSE, D, D_FF), D ** -0.5),
        "ffn_w3": nrm((N_DENSE, D, D_FF), D ** -0.5),
        "ffn_w2": nrm((N_DENSE, D_FF, D), BETA * D_FF ** -0.5),
        "moe_w_router": nrm((N_MOE, D, N_EXPERTS), D ** -0.5),
        "moe_b_router": nrm((N_MOE, N_EXPERTS), 0.01),
        "moe_w1": nrm((N_MOE, N_EXPERTS, D, D_FF), D ** -0.5),
        "moe_w3": nrm((N_MOE, N_EXPERTS, D, D_FF), D ** -0.5),
        "moe_w2": nrm((N_MOE, N_EXPERTS, D_FF, D), BETA * D_FF ** -0.5),
    }


def reference(x_prompt, x_sample, state_gla, c, c_ctx, w_mod, b_mod, w_in, w_a2, b_a2, gla_norm_g,
              conv_w, conv_b, conv_ln_g, conv_ln_b, w_pw2, b_pw2, w_out, ln1_g, ln1_b, ln2_g, ln2_b,
              ffn_w1, ffn_w3, ffn_w2, moe_w_router, moe_b_router, moe_w1, moe_w3, moe_w2):
    def layer_args(l):
        mix = (w_in[l], w_a2[l], b_a2[l], gla_norm_g[l], conv_w[l], conv_b[l], conv_ln_g[l],
               conv_ln_b[l], w_pw2[l], b_pw2[l], w_out[l])
        i = l // 2
        if l % 2 == 0:
            ffn = functools.partial(_swiglu, w1=ffn_w1[i], w3=ffn_w3[i], w2=ffn_w2[i])
        else:
            ffn = functools.partial(_moe, w_r=moe_w_router[i], b_r=moe_b_router[i],
                                    w1=moe_w1[i], w3=moe_w3[i], w2=moe_w2[i])
        return mix, ffn

    xp = x_prompt
    zero_state = jnp.zeros((x_prompt.shape[0], 2, GLA_HEADS, GLA_DK, GLA_DV), jnp.float32)
    ctx_states = []
    for l in range(DEPTH):
        mix, ffn = layer_args(l)
        xp, s = _layer(xp, c_ctx, zero_state, w_mod[l], b_mod[l], mix, ln1_g[l], ln1_b[l], ln2_g[l], ln2_b[l], ffn)
        ctx_states.append(s)
    new_state_gla = jnp.stack(ctx_states, axis=1)

    xs = x_sample + _grid_pos_emb(x_sample.shape[1]).astype(x_sample.dtype)[None]
    for l in range(DEPTH):
        mix, ffn = layer_args(l)
        xs, _ = _layer(xs, c, state_gla[:, l], w_mod[l], b_mod[l], mix, ln1_g[l], ln1_b[l], ln2_g[l], ln2_b[l], ffn)

    return (xp, xs, new_state_gla)
```

```python
import functools

import jax
import jax.numpy as jnp
from jax import lax
from jax.experimental import pallas as pl
from jax.experimental.pallas import tpu as pltpu

F32 = jnp.float32
BF16 = jnp.bfloat16

D = 2048
NB_P, L_P = 16, 256
NB_S, L_S = 4, 1024
TP = NB_P * L_P
T = TP + NB_S * L_S
DEPTH = 2
GRID_W = 64
GW = D // 2
CW = D - GW
H = 4
DV = GW // H
DK = DV // 2
QC = H * DK
RANK = 16
TAU = 16.0
CHUNK = 64
CONV_K = 31
FF = 7 * D // 2
NE = 8
ALPHA = (2 * DEPTH) ** 0.25
LN_EPS = 1e-5
RMS_EPS = 1e-6
MAIN_COLS = 2 * QC + 2 * GW

TILE = 256
NTILES = T // TILE
NTILES_P = TP // TILE
TILES_PER_S = L_S // TILE
HALO = 16

VMEM_LIMIT = 56 * 1024 * 1024

SH1, SC1, G1, SH2, SC2, G2 = range(6)


def _params(n_axes, vmem=VMEM_LIMIT):
    return pltpu.CompilerParams(dimension_semantics=("arbitrary",) * n_axes,
                                vmem_limit_bytes=vmem)


def _mod_row(tok0):
    return jnp.where(tok0 < TP, 0, 1 + (tok0 - TP) // L_S)


def _sigmoid(x):
    return 1.0 / (1.0 + jnp.exp(-x))


def _silu(x):
    return x * _sigmoid(x)


def _layer_norm(r, g, b):
    mu = jnp.mean(r, -1, keepdims=True)
    rc = r - mu
    var = jnp.mean(rc * rc, -1, keepdims=True)
    return rc * lax.rsqrt(var + LN_EPS) * g + b


def _assemble_kernel(xp_ref, xs_ref, pos_ref, o_ref):
    i = pl.program_id(0)

    @pl.when(i < NTILES_P)
    def _():
        o_ref[...] = xp_ref[...]

    @pl.when(i >= NTILES_P)
    def _():
        o_ref[...] = xs_ref[...] + pos_ref[...]


def _assemble(xp, xs, pos):
    return pl.pallas_call(
        _assemble_kernel,
        out_shape=jax.ShapeDtypeStruct((T, D), F32),
        grid=(NTILES,),
        in_specs=[
            pl.BlockSpec((TILE, D), lambda i: (jnp.minimum(i, NTILES_P - 1), 0)),
            pl.BlockSpec((TILE, D), lambda i: (jnp.maximum(i - NTILES_P, 0), 0)),
            pl.BlockSpec((TILE, D), lambda i: (jnp.maximum(i - NTILES_P, 0) % TILES_PER_S, 0)),
        ],
        out_specs=pl.BlockSpec((TILE, D), lambda i: (i, 0)),
        compiler_params=_params(1),
        name="assemble",
    )(xp, xs, pos)


MOD_TN = 1024


def _mod_kernel(c_ref, w_ref, b_ref, o_ref):
    s = _silu(c_ref[...])
    o_ref[0] = jnp.dot(s, w_ref[0], preferred_element_type=F32,
                       precision=lax.Precision.HIGHEST) + b_ref[0]


def _modulation(cond8, w_mod, b_mod):
    out = pl.pallas_call(
        _mod_kernel,
        out_shape=jax.ShapeDtypeStruct((DEPTH, 8, 6 * D), F32),
        grid=(DEPTH, 6 * D // MOD_TN),
        in_specs=[
            pl.BlockSpec((8, D), lambda l, j: (0, 0)),
            pl.BlockSpec((1, D, MOD_TN), lambda l, j: (l, 0, j)),
            pl.BlockSpec((1, 1, MOD_TN), lambda l, j: (l, 0, j)),
        ],
        out_specs=pl.BlockSpec((1, 8, MOD_TN), lambda l, j: (l, 0, j)),
        compiler_params=_params(2),
        name="modulation",
    )(cond8, w_mod, b_mod.reshape(DEPTH, 1, 6 * D))
    return out.reshape(DEPTH, 8, 6, D)


PROJ_TM = 512
PROJ_TN = 512


def _modulated(x_ref, mod_ref, shift, scale):
    m = mod_ref[0]
    return x_ref[...] * (1.0 + m[scale:scale + 1, :]) + m[shift:shift + 1, :]


def _inproj_kernel(x_ref, mod_ref, w_ref, wa_ref, z_ref, a_ref, h_scr):
    @pl.when(pl.program_id(1) == 0)
    def _():
        hb = _modulated(x_ref, mod_ref, SH1, SC1).astype(BF16)
        h_scr[...] = hb
        a_ref[...] = jnp.dot(hb, wa_ref[...], preferred_element_type=F32)

    z_ref[...] = jnp.dot(h_scr[...], w_ref[...], preferred_element_type=F32)


def _inproj(x, mod_l, w_main, w_a):
    return pl.pallas_call(
        _inproj_kernel,
        out_shape=(jax.ShapeDtypeStruct((T, MAIN_COLS), F32),
                   jax.ShapeDtypeStruct((T, 128), F32)),
        grid=(T // PROJ_TM, MAIN_COLS // PROJ_TN),
        in_specs=[
            pl.BlockSpec((PROJ_TM, D), lambda i, j: (i, 0)),
            pl.BlockSpec((1, 6, D), lambda i, j: (_mod_row(i * PROJ_TM), 0, 0)),
            pl.BlockSpec((D, PROJ_TN), lambda i, j: (0, j)),
            pl.BlockSpec((D, 128), lambda i, j: (0, 0)),
        ],
        out_specs=(pl.BlockSpec((PROJ_TM, PROJ_TN), lambda i, j: (i, j)),
                   pl.BlockSpec((PROJ_TM, 128), lambda i, j: (i, 0))),
        scratch_shapes=[pltpu.VMEM((PROJ_TM, D), BF16)],
        compiler_params=_params(2),
        name="inproj",
    )(x, mod_l, w_main, w_a)


def _uproj_kernel(x_ref, mod_ref, wa_ref, wg_ref, u_ref, h_scr):
    @pl.when(pl.program_id(1) == 0)
    def _():
        h_scr[...] = _modulated(x_ref, mod_ref, SH1, SC1).astype(BF16)

    h = h_scr[...]
    a = jnp.dot(h, wa_ref[...], preferred_element_type=F32)
    g = jnp.dot(h, wg_ref[...], preferred_element_type=F32)
    u_ref[...] = a * _sigmoid(g)


def _uproj(x, mod_l, w_u):
    nj = CW // PROJ_TN
    return pl.pallas_call(
        _uproj_kernel,
        out_shape=jax.ShapeDtypeStruct((T, CW), F32),
        grid=(T // PROJ_TM, nj),
        in_specs=[
            pl.BlockSpec((PROJ_TM, D), lambda i, j: (i, 0)),
            pl.BlockSpec((1, 6, D), lambda i, j: (_mod_row(i * PROJ_TM), 0, 0)),
            pl.BlockSpec((D, PROJ_TN), lambda i, j: (0, j)),
            pl.BlockSpec((D, PROJ_TN), lambda i, j: (0, j + nj)),
        ],
        out_specs=pl.BlockSpec((PROJ_TM, PROJ_TN), lambda i, j: (i, j)),
        scratch_shapes=[pltpu.VMEM((PROJ_TM, D), BF16)],
        compiler_params=_params(2),
        name="uproj",
    )(x, mod_l, w_u, w_u)


CONV_RC = 32
CONV_CC = 256


def _conv_kernel(uc_ref, up_ref, un_ref, cw_ref, cb_ref, lg_ref, lb_ref, wp_ref, bp_ref,
                 o_ref, pad_scr, conv_scr):
    i = pl.program_id(0)
    s = jnp.maximum(i - NTILES_P, 0) % TILES_PER_S
    has_prev = jnp.logical_and(i >= NTILES_P, s != 0)
    has_next = jnp.logical_and(i >= NTILES_P, s != TILES_PER_S - 1)
    pad_scr[0:HALO, :] = jnp.where(has_prev, up_ref[...], 0.0)
    pad_scr[HALO:HALO + TILE, :] = uc_ref[...]
    pad_scr[HALO + TILE:HALO + TILE + HALO, :] = jnp.where(has_next, un_ref[...], 0.0)

    off = HALO - CONV_K // 2
    for c in range(CW // CONV_CC):
        cs = slice(c * CONV_CC, (c + 1) * CONV_CC)

        def body(r, carry, cs=cs):
            r0 = pl.multiple_of(r * CONV_RC, CONV_RC)
            win = pad_scr[pl.ds(r0, 2 * CONV_RC), cs]
            acc = jnp.zeros((CONV_RC, CONV_CC), F32)
            for k in range(CONV_K):
                acc = acc + cw_ref[k:k + 1, cs] * win[k + off:k + off + CONV_RC, :]
            conv_scr[pl.ds(r0, CONV_RC), cs] = acc
            return carry

        lax.fori_loop(0, TILE // CONV_RC, body, 0)

    v = conv_scr[...] + cb_ref[...]
    y = _silu(_layer_norm(v, lg_ref[...], lb_ref[...]))
    o_ref[...] = jnp.dot(y.astype(BF16), wp_ref[...], preferred_element_type=F32) + bp_ref[...]


def _conv_module(u, conv_w, conv_b, ln_g, ln_b, w_pw2, b_pw2):
    hb = TILE // HALO
    row = lambda a: a.reshape(1, CW)
    return pl.pallas_call(
        _conv_kernel,
        out_shape=jax.ShapeDtypeStruct((T, CW), F32),
        grid=(NTILES,),
        in_specs=[
            pl.BlockSpec((TILE, CW), lambda i: (i, 0)),
            pl.BlockSpec((HALO, CW), lambda i: (jnp.maximum(i * hb - 1, 0), 0)),
            pl.BlockSpec((HALO, CW), lambda i: (jnp.minimum((i + 1) * hb, T // HALO - 1), 0)),
            pl.BlockSpec((CONV_K, CW), lambda i: (0, 0)),
            pl.BlockSpec((1, CW), lambda i: (0, 0)),
            pl.BlockSpec((1, CW), lambda i: (0, 0)),
            pl.BlockSpec((1, CW), lambda i: (0, 0)),
            pl.BlockSpec((CW, CW), lambda i: (0, 0)),
            pl.BlockSpec((1, CW), lambda i: (0, 0)),
        ],
        out_specs=pl.BlockSpec((TILE, CW), lambda i: (i, 0)),
        scratch_shapes=[pltpu.VMEM((TILE + 2 * HALO, CW), F32), pltpu.VMEM((TILE, CW), F32)],
        compiler_params=_params(1),
        name="conv_module",
    )(u, u, u, conv_w, row(conv_b), row(ln_g), row(ln_b), w_pw2, row(b_pw2))


def _split3(x):
    hi = x.astype(BF16)
    r1 = x - hi.astype(F32)
    mid = r1.astype(BF16)
    lo = (r1 - mid.astype(F32)).astype(BF16)
    return hi, mid, lo


def _log_sigmoid(x):
    return jnp.minimum(x, 0.0) - jnp.log(1.0 + jnp.exp(-jnp.abs(x)))


def _dot_nt(a, b):
    return lax.dot_general(a, b, (((1,), (1,)), ((), ())), preferred_element_type=F32)


def _dot_tn(a, b):
    return lax.dot_general(a, b, (((0,), (0,)), ((), ())), preferred_element_type=F32)


def _gla_kernel(q_ref, k_ref, v_ref, a_ref, wa2_ref, ba2_ref, tri_ref, s0_ref,
                o_ref, snew_ref, st_scr, *, reverse):
    n = pl.program_id(0)
    tile = (NTILES - 1 - n) if reverse else n
    is_prompt = tile < NTILES_P
    spos = jnp.maximum(tile - NTILES_P, 0) % TILES_PER_S
    seq_first = spos == (TILES_PER_S - 1 if reverse else 0)

    @pl.when(is_prompt)
    def _():
        st_scr[...] = jnp.zeros_like(st_scr)

    @pl.when(jnp.logical_and(jnp.logical_not(is_prompt), seq_first))
    def _():
        for h in range(H):
            st_scr[h] = s0_ref[0, 0, h].T

    tri = tri_ref[...]
    mask = tri > 0
    a_lr = a_ref[:, 0:2 * RANK].astype(BF16)
    n_chunks = TILE // CHUNK
    order = range(n_chunks - 1, -1, -1) if reverse else range(n_chunks)
    for h in range(H):
        ks = slice(h * DK, (h + 1) * DK)
        logits = jnp.dot(a_lr, wa2_ref[:, ks], preferred_element_type=F32) + ba2_ref[:, ks]
        la = _log_sigmoid(logits) / TAU
        hi, mid, lo = _split3(la)
        b = (jnp.dot(tri, hi, preferred_element_type=F32)
             + jnp.dot(tri, mid, preferred_element_type=F32)
             + jnp.dot(tri, lo, preferred_element_type=F32))
        q = q_ref[:, ks] * (DK ** -0.5)
        k = k_ref[:, ks]
        vb = v_ref[:, h * DV:(h + 1) * DV].astype(BF16)
        qd = (q * jnp.exp(b)).astype(BF16)
        kd = (k * jnp.exp(-b)).astype(BF16)
        att = jnp.where(mask, _dot_nt(qd, kd), 0.0).astype(BF16)
        o_intra = jnp.dot(att, vb, preferred_element_type=F32)
        st = st_scr[h]
        for c in order:
            rows = slice(c * CHUNK, (c + 1) * CHUNK)
            last = c * CHUNK if reverse else (c + 1) * CHUNK - 1
            b_last = b[last:last + 1, :]
            kl = (k[rows] * jnp.exp(b_last - b[rows])).astype(BF16)
            o_ref[rows, h * DV:(h + 1) * DV] = o_intra[rows] + _dot_nt(qd[rows], st.astype(BF16))
            st = jnp.exp(b_last) * st + _dot_tn(vb[rows], kl)
        st_scr[h] = st

    @pl.when(is_prompt)
    def _():
        for h in range(H):
            snew_ref[0, h] = st_scr[h].T


def _gla(z, a_lr, wa2_dir, ba2_dir, tri_dir, s0, direction):
    reverse = direction == 1
    tile = (lambda n: NTILES - 1 - n) if reverse else (lambda n: n)
    req = lambda n: jnp.clip((tile(n) - NTILES_P) // TILES_PER_S, 0, NB_S - 1)
    return pl.pallas_call(
        functools.partial(_gla_kernel, reverse=reverse),
        out_shape=(jax.ShapeDtypeStruct((T, GW), F32),
                   jax.ShapeDtypeStruct((NB_P, H, DK, DV), F32)),
        grid=(NTILES,),
        in_specs=[
            pl.BlockSpec((TILE, QC), lambda n: (tile(n), 0)),
            pl.BlockSpec((TILE, QC), lambda n: (tile(n), 1)),
            pl.BlockSpec((TILE, GW), lambda n: (tile(n), 1)),
            pl.BlockSpec((TILE, 128), lambda n: (tile(n), 0)),
            pl.BlockSpec((2 * RANK, QC), lambda n: (0, 0)),
            pl.BlockSpec((1, QC), lambda n: (0, 0)),
            pl.BlockSpec((TILE, TILE), lambda n: (0, 0)),
            pl.BlockSpec((1, 1, H, DK, DV), lambda n: (req(n), direction, 0, 0, 0)),
        ],
        out_specs=(pl.BlockSpec((TILE, GW), lambda n: (tile(n), 0)),
                   pl.BlockSpec((1, H, DK, DV), lambda n: (jnp.minimum(tile(n), NTILES_P - 1), 0, 0, 0))),
        scratch_shapes=[pltpu.VMEM((H, DV, DK), F32)],
        compiler_params=_params(1),
        name="gla_bwd" if reverse else "gla_fwd",
    )(z, z, z, a_lr, wa2_dir, ba2_dir, tri_dir, s0)


def _mixout_kernel(of_ref, ob_ref, g_ref, u_ref, x_ref, mod_ref, gng_ref, wo_ref, l1g_ref, l1b_ref,
                   *rest, with_router):
    if with_router:
        wr_ref, br_ref, x1_ref, h2_ref, route_ref = rest
    else:
        x1_ref, h2_ref = rest
    o = of_ref[...] + ob_ref[...]
    parts = []
    for h in range(H):
        oh = o[:, h * DV:(h + 1) * DV]
        ms = jnp.mean(oh * oh, -1, keepdims=True)
        parts.append(oh * lax.rsqrt(ms + RMS_EPS) * gng_ref[...])
    on = jnp.concatenate(parts, axis=-1) * _silu(g_ref[...])
    y = (jnp.dot(on.astype(BF16), wo_ref[0:GW, :], preferred_element_type=F32)
         + jnp.dot(u_ref[...].astype(BF16), wo_ref[GW:D, :], preferred_element_type=F32))
    m = mod_ref[0]
    x1 = _layer_norm(ALPHA * x_ref[...] + m[G1:G1 + 1, :] * y, l1g_ref[...], l1b_ref[...])
    x1_ref[...] = x1
    h2 = x1 * (1.0 + m[SC2:SC2 + 1, :]) + m[SH2:SH2 + 1, :]
    h2_ref[...] = h2.astype(h2_ref.dtype)
    if with_router:
        logits = jnp.dot(h2, wr_ref[...], preferred_element_type=F32,
                         precision=lax.Precision.HIGHEST) + br_ref[...]
        lane = lax.broadcasted_iota(jnp.int32, logits.shape, 1)
        lg = jnp.where(lane < NE, logits, -jnp.inf)
        v1 = jnp.max(lg, -1, keepdims=True)
        i1 = jnp.min(jnp.where(lg == v1, lane, 128), -1, keepdims=True)
        lg2 = jnp.where(lane == i1, -jnp.inf, lg)
        v2 = jnp.max(lg2, -1, keepdims=True)
        i2 = jnp.min(jnp.where(lg2 == v2, lane, 128), -1, keepdims=True)
        e2 = jnp.exp(v2 - v1)
        w1 = 1.0 / (1.0 + e2)
        w2 = e2 / (1.0 + e2)
        route_ref[...] = jnp.where(lane == 0, i1.astype(F32),
                         jnp.where(lane == 1, i2.astype(F32),
                         jnp.where(lane == 2, w1, jnp.where(lane == 3, w2, 0.0))))


def _mixout(o_f, o_b, z, u2, x, mod_l, gng, w_out, ln_g, ln_b, router=None):
    row = lambda a: a.reshape(1, -1)
    full = lambda shape: pl.BlockSpec(shape, lambda i: (0,) * len(shape))
    in_specs = [
        pl.BlockSpec((TILE, GW), lambda i: (i, 0)),
        pl.BlockSpec((TILE, GW), lambda i: (i, 0)),
        pl.BlockSpec((TILE, GW), lambda i: (i, 2)),
        pl.BlockSpec((TILE, CW), lambda i: (i, 0)),
        pl.BlockSpec((TILE, D), lambda i: (i, 0)),
        pl.BlockSpec((1, 6, D), lambda i: (_mod_row(i * TILE), 0, 0)),
        full((1, DV)), full((D, D)), full((1, D)), full((1, D)),
    ]
    args = [o_f, o_b, z, u2, x, mod_l, row(gng), w_out, row(ln_g), row(ln_b)]
    out_shape = [jax.ShapeDtypeStruct((T, D), F32)]
    out_specs = [pl.BlockSpec((TILE, D), lambda i: (i, 0)), pl.BlockSpec((TILE, D), lambda i: (i, 0))]
    if router is None:
        out_shape.append(jax.ShapeDtypeStruct((T, D), BF16))
    else:
        w_r, b_r = router
        in_specs += [full((D, 128)), full((1, 128))]
        args += [w_r, b_r]
        out_shape += [jax.ShapeDtypeStruct((T, D), F32), jax.ShapeDtypeStruct((T, 128), F32)]
        out_specs.append(pl.BlockSpec((TILE, 128), lambda i: (i, 0)))
    return pl.pallas_call(
        functools.partial(_mixout_kernel, with_router=router is not None),
        out_shape=tuple(out_shape),
        grid=(NTILES,),
        in_specs=in_specs,
        out_specs=tuple(out_specs),
        compiler_params=_params(1),
        name="mixout",
    )(*args)


FFN_R = 1024
FFN_SUB = 256
FFN_TF = 256


def _ffn_kernel(te_ref, tn_ref, tb_ref, h_ref, w1_ref, w3_ref, w2_ref, o_ref, w1_scr, w3_scr, w2_scr):
    s = pl.program_id(0)
    j = pl.program_id(1)
    n = tn_ref[s]

    @pl.when(jnp.logical_and(j == 0, n > 0))
    def _():
        o_ref[...] = jnp.zeros_like(o_ref)

    @pl.when(n > 0)
    def _():
        w1_scr[...] = w1_ref[0].astype(BF16)
        w3_scr[...] = w3_ref[0].astype(BF16)
        w2_scr[...] = w2_ref[0].astype(BF16)
        for c in range(FFN_R // FFN_SUB):
            rows = slice(c * FFN_SUB, (c + 1) * FFN_SUB)

            @pl.when(c * FFN_SUB < n)
            def _(rows=rows):
                hc = h_ref[rows, :]
                g = jnp.dot(hc, w1_scr[...], preferred_element_type=F32)
                u = jnp.dot(hc, w3_scr[...], preferred_element_type=F32)
                a = (_silu(g) * u).astype(BF16)
                o_ref[rows, :] += jnp.dot(a, w2_scr[...], preferred_element_type=F32)


def _ffn(hs, tile_expert, tile_rows, tile_block, w1, w3, w2):
    n_tiles = tile_expert.shape[0]
    nj = FF // FFN_TF

    def jj(s, j, tn):
        return jnp.where(tn[s] > 0, j, nj - 1)

    return pl.pallas_call(
        _ffn_kernel,
        out_shape=jax.ShapeDtypeStruct((hs.shape[0], D), F32),
        grid_spec=pltpu.PrefetchScalarGridSpec(
            num_scalar_prefetch=3,
            grid=(n_tiles, nj),
            in_specs=[
                pl.BlockSpec((FFN_R, D), lambda s, j, te, tn, tb: (tb[s], 0)),
                pl.BlockSpec((1, D, FFN_TF), lambda s, j, te, tn, tb: (te[s], 0, jj(s, j, tn))),
                pl.BlockSpec((1, D, FFN_TF), lambda s, j, te, tn, tb: (te[s], 0, jj(s, j, tn))),
                pl.BlockSpec((1, FFN_TF, D), lambda s, j, te, tn, tb: (te[s], jj(s, j, tn), 0)),
            ],
            out_specs=pl.BlockSpec((FFN_R, D), lambda s, j, te, tn, tb: (tb[s], 0)),
            scratch_shapes=[pltpu.VMEM((D, FFN_TF), BF16), pltpu.VMEM((D, FFN_TF), BF16),
                            pltpu.VMEM((FFN_TF, D), BF16)],
        ),
        compiler_params=_params(2),
        name="ffn",
    )(tile_expert, tile_rows, tile_block, hs, w1, w3, w2)


def _dispatch_kernel(src_ref, live_ref, h_hbm, o_ref, buf, sem):
    c = pl.program_id(0)

    @pl.when(live_ref[c] > 0)
    def _():
        def issue(r, carry):
            t = src_ref[c * FFN_SUB + r]
            pltpu.make_async_copy(h_hbm.at[pl.ds(t, 1), :], buf.at[pl.ds(r, 1), :], sem).start()
            return carry

        lax.fori_loop(0, FFN_SUB, issue, 0)
        pltpu.make_async_copy(h_hbm.at[pl.ds(0, FFN_SUB), :], buf, sem).wait()
        o_ref[...] = buf[...].astype(BF16)

    @pl.when(live_ref[c] == 0)
    def _():
        o_ref[...] = jnp.zeros_like(o_ref)


def _dispatch(src_tok, live, h2):
    n_rows = src_tok.shape[0]
    return pl.pallas_call(
        _dispatch_kernel,
        out_shape=jax.ShapeDtypeStruct((n_rows, D), BF16),
        grid_spec=pltpu.PrefetchScalarGridSpec(
            num_scalar_prefetch=2,
            grid=(n_rows // FFN_SUB,),
            in_specs=[pl.BlockSpec(memory_space=pl.ANY)],
            out_specs=pl.BlockSpec((FFN_SUB, D), lambda c, src, live: (c, 0)),
            scratch_shapes=[pltpu.VMEM((FFN_SUB, D), F32), pltpu.SemaphoreType.DMA],
        ),
        compiler_params=_params(1),
        name="dispatch",
    )(src_tok, live, h2)


def _ln2_kernel(x_ref, f_ref, mod_ref, g_ref, b_ref, o_ref):
    m = mod_ref[0]
    o_ref[...] = _layer_norm(ALPHA * x_ref[...] + m[G2:G2 + 1, :] * f_ref[...], g_ref[...], b_ref[...])


def _ln2(x1, f, mod_l, ln_g, ln_b):
    return pl.pallas_call(
        _ln2_kernel,
        out_shape=jax.ShapeDtypeStruct((T, D), F32),
        grid=(NTILES,),
        in_specs=[
            pl.BlockSpec((TILE, D), lambda i: (i, 0)),
            pl.BlockSpec((TILE, D), lambda i: (i, 0)),
            pl.BlockSpec((1, 6, D), lambda i: (_mod_row(i * TILE), 0, 0)),
            pl.BlockSpec((1, D), lambda i: (0, 0)),
            pl.BlockSpec((1, D), lambda i: (0, 0)),
        ],
        out_specs=pl.BlockSpec((TILE, D), lambda i: (i, 0)),
        compiler_params=_params(1),
        name="ln2",
    )(x1, f, mod_l, ln_g.reshape(1, D), ln_b.reshape(1, D))


def _combine_kernel(pos_ref, x_ref, route_ref, mod_ref, g_ref, b_ref, y_hbm, o_ref, buf, sem):
    i = pl.program_id(0)

    def issue(r, carry):
        for slot in range(2):
            p = pos_ref[2 * (i * TILE + r) + slot]
            pltpu.make_async_copy(y_hbm.at[pl.ds(p, 1), :], buf.at[slot, pl.ds(r, 1), :], sem).start()
        return carry

    lax.fori_loop(0, TILE, issue, 0)
    for slot in range(2):
        pltpu.make_async_copy(y_hbm.at[pl.ds(0, TILE), :], buf.at[slot], sem).wait()
    route = route_ref[...]
    f = route[:, 2:3] * buf[0] + route[:, 3:4] * buf[1]
    m = mod_ref[0]
    o_ref[...] = _layer_norm(ALPHA * x_ref[...] + m[G2:G2 + 1, :] * f, g_ref[...], b_ref[...])


def _combine(pos, x1, route, mod_l, ln_g, ln_b, y):
    return pl.pallas_call(
        _combine_kernel,
        out_shape=jax.ShapeDtypeStruct((T, D), F32),
        grid_spec=pltpu.PrefetchScalarGridSpec(
            num_scalar_prefetch=1,
            grid=(NTILES,),
            in_specs=[
                pl.BlockSpec((TILE, D), lambda i, pos: (i, 0)),
                pl.BlockSpec((TILE, 128), lambda i, pos: (i, 0)),
                pl.BlockSpec((1, 6, D), lambda i, pos: (_mod_row(i * TILE), 0, 0)),
                pl.BlockSpec((1, D), lambda i, pos: (0, 0)),
                pl.BlockSpec((1, D), lambda i, pos: (0, 0)),
                pl.BlockSpec(memory_space=pl.ANY),
            ],
            out_specs=pl.BlockSpec((TILE, D), lambda i, pos: (i, 0)),
            scratch_shapes=[pltpu.VMEM((2, TILE, D), F32), pltpu.SemaphoreType.DMA],
        ),
        compiler_params=_params(1),
        name="combine",
    )(pos, x1, route, mod_l, ln_g.reshape(1, D), ln_b.reshape(1, D), y)


MOE_TILES = 2 * T // FFN_R + NE


def _routing_tables(route):
    eidx = route[:, 0:2].astype(jnp.int32).reshape(-1)
    onehot = (eidx[:, None] == jnp.arange(NE, dtype=jnp.int32)[None, :]).astype(jnp.int32)
    csum = jnp.cumsum(onehot, axis=0)
    rank = jnp.take_along_axis(csum, eidx[:, None], axis=1)[:, 0] - 1
    counts = csum[-1]
    ntiles = (counts + FFN_R - 1) // FFN_R
    tend = jnp.cumsum(ntiles)
    tstart = tend - ntiles
    pos = tstart[eidx] * FFN_R + rank
    n_rows = MOE_TILES * FFN_R
    src_tok = jnp.zeros((n_rows,), jnp.int32).at[pos].set(jnp.arange(2 * T, dtype=jnp.int32) // 2)
    tiles = jnp.arange(MOE_TILES, dtype=jnp.int32)
    total = tend[-1]
    t_eff = jnp.minimum(tiles, total - 1)
    tile_expert = jnp.minimum(jnp.sum((t_eff[:, None] >= tend[None, :]).astype(jnp.int32), axis=1), NE - 1)
    tile_rows = jnp.clip(counts[tile_expert] - (t_eff - tstart[tile_expert]) * FFN_R, 0, FFN_R)
    tile_rows = jnp.where(tiles < total, tile_rows, 0)
    subs = jnp.arange(n_rows // FFN_SUB, dtype=jnp.int32)
    sub_tile = subs // (FFN_R // FFN_SUB)
    live = ((subs % (FFN_R // FFN_SUB)) * FFN_SUB < tile_rows[sub_tile]).astype(jnp.int32)
    return pos, src_tok, live, tile_expert.astype(jnp.int32), tile_rows.astype(jnp.int32), t_eff


def _grid_pos_emb():
    rows = L_S // GRID_W
    r = jnp.broadcast_to(jnp.arange(rows, dtype=F32)[:, None], (rows, GRID_W)).reshape(-1)
    col = jnp.broadcast_to(jnp.arange(GRID_W, dtype=F32)[None, :], (rows, GRID_W)).reshape(-1)
    quarter = D // 4
    freqs = 1.0 / (10000.0 ** (jnp.arange(quarter, dtype=F32) / quarter))

    def enc(p):
        ang = p[:, None] * freqs[None, :]
        return jnp.concatenate([jnp.sin(ang), jnp.cos(ang)], -1)

    return jnp.concatenate([enc(r), enc(col)], -1)


def _cumsum_matrices():
    i = jnp.arange(TILE)
    same = (i[:, None] // CHUNK) == (i[None, :] // CHUNK)
    fwd = jnp.logical_and(same, i[None, :] <= i[:, None])
    bwd = jnp.logical_and(same, i[None, :] >= i[:, None])
    return fwd.astype(BF16), bwd.astype(BF16)


def kernel(x_prompt, x_sample, state_gla, c, c_ctx, w_mod, b_mod, w_in, w_a2, b_a2, gla_norm_g, conv_w, conv_b, conv_ln_g, conv_ln_b, w_pw2, b_pw2, w_out, ln1_g, ln1_b, ln2_g, ln2_b, ffn_w1, ffn_w3, ffn_w2, moe_w_router, moe_b_router, moe_w1, moe_w3, moe_w2):
    x = _assemble(x_prompt.reshape(TP, D), x_sample.reshape(T - TP, D), _grid_pos_emb())
    cond8 = jnp.zeros((8, D), F32).at[0].set(c_ctx).at[1:1 + NB_S].set(c)
    mod = _modulation(cond8, w_mod, b_mod)
    tri = _cumsum_matrices()
    dense_tiles = T // FFN_R
    dense_meta = (jnp.zeros((dense_tiles,), jnp.int32), jnp.full((dense_tiles,), FFN_R, jnp.int32),
                  jnp.arange(dense_tiles, dtype=jnp.int32))

    states = []
    for l in range(DEPTH):
        mod_l = mod[l]
        w_main = w_in[l, :, :MAIN_COLS].astype(BF16)
        w_a = jnp.pad(w_in[l, :, MAIN_COLS:MAIN_COLS + 2 * RANK], ((0, 0), (0, 128 - 2 * RANK))).astype(BF16)
        w_u = w_in[l, :, MAIN_COLS + 2 * RANK:].astype(BF16)
        z, a_lr = _inproj(x, mod_l, w_main, w_a)
        u = _uproj(x, mod_l, w_u)
        u2 = _conv_module(u, conv_w[l], conv_b[l], conv_ln_g[l], conv_ln_b[l],
                          w_pw2[l].astype(BF16), b_pw2[l])
        o_dir, s_dir = [], []
        for d in range(2):
            wa2 = jnp.zeros((2 * RANK, QC), F32).at[d * RANK:(d + 1) * RANK].set(w_a2[l, d]).astype(BF16)
            o, s_new = _gla(z, a_lr, wa2, b_a2[l, d].reshape(1, QC), tri[d], state_gla[:, l], d)
            o_dir.append(o)
            s_dir.append(s_new)
        states.append(jnp.stack(s_dir, axis=1))
        is_moe = l % 2 == 1
        i = l // 2
        router = None
        if is_moe:
            router = (jnp.pad(moe_w_router[i], ((0, 0), (0, 128 - NE))),
                      jnp.pad(moe_b_router[i], (0, 128 - NE)).reshape(1, 128))
        outs = _mixout(o_dir[0], o_dir[1], z, u2, x, mod_l, gla_norm_g[l], w_out[l].astype(BF16),
                       ln1_g[l], ln1_b[l], router)
        if is_moe:
            x1, h2, route = outs
            pos, src_tok, live, t_exp, t_rows, t_blk = _routing_tables(route)
            hs = _dispatch(src_tok, live, h2)
            y = _ffn(hs, t_exp, t_rows, t_blk, moe_w1[i], moe_w3[i], moe_w2[i])
            x = _combine(pos, x1, route, mod_l, ln2_g[l], ln2_b[l], y)
        else:
            x1, h2 = outs
            y = _ffn(h2, *dense_meta, ffn_w1[i][None], ffn_w3[i][None], ffn_w2[i][None])
            x = _ln2(x1, y, mod_l, ln2_g[l], ln2_b[l])

    y_prompt = x[:TP].reshape(NB_P, L_P, D)
    y_sample = x[TP:].reshape(NB_S, L_S, D)
    return y_prompt, y_sample, jnp.stack(states, axis=1)
```

```python
import functools

import jax
import jax.numpy as jnp
import numpy as np
from jax import lax
from jax.experimental import pallas as pl
from jax.experimental.pallas import tpu as pltpu

F32 = jnp.float32
BF16 = jnp.bfloat16

D = 2048
NB_P, L_P = 16, 256
NB_S, L_S = 4, 1024
TP = NB_P * L_P
T = TP + NB_S * L_S
DEPTH = 2
GRID_W = 64
GW = D // 2
CW = D - GW
H = 4
DV = GW // H
DK = DV // 2
QC = H * DK
RANK = 16
TAU = 16.0
CHUNK = 64
CONV_K = 31
FF = 7 * D // 2
NE = 8
ALPHA = (2 * DEPTH) ** 0.25
LN_EPS = 1e-5
RMS_EPS = 1e-6
MAIN_COLS = 2 * QC + 2 * GW

TILE = 256
NTILES = T // TILE
NTILES_P = TP // TILE
TILES_PER_S = L_S // TILE
HALO = 16

VMEM_LIMIT = 56 * 1024 * 1024

SH1, SC1, G1, SH2, SC2, G2 = range(6)


def _params(n_axes, vmem=VMEM_LIMIT):
    return pltpu.CompilerParams(dimension_semantics=("arbitrary",) * n_axes,
                                vmem_limit_bytes=vmem)


def _mod_row(tok0):
    return jnp.where(tok0 < TP, 0, 1 + (tok0 - TP) // L_S)


def _sigmoid(x):
    return 1.0 / (1.0 + jnp.exp(-x))


def _silu(x):
    return x * _sigmoid(x)


def _layer_norm(r, g, b):
    mu = jnp.mean(r, -1, keepdims=True)
    rc = r - mu
    var = jnp.mean(rc * rc, -1, keepdims=True)
    return rc * lax.rsqrt(var + LN_EPS) * g + b


def _assemble_kernel(xp_ref, xs_ref, pos_ref, o_ref):
    i = pl.program_id(0)

    @pl.when(i < NTILES_P)
    def _():
        o_ref[...] = xp_ref[...]

    @pl.when(i >= NTILES_P)
    def _():
        o_ref[...] = xs_ref[...] + pos_ref[...]


def _assemble(xp, xs, pos):
    return pl.pallas_call(
        _assemble_kernel,
        out_shape=jax.ShapeDtypeStruct((T, D), F32),
        grid=(NTILES,),
        in_specs=[
            pl.BlockSpec((TILE, D), lambda i: (jnp.minimum(i, NTILES_P - 1), 0)),
            pl.BlockSpec((TILE, D), lambda i: (jnp.maximum(i - NTILES_P, 0), 0)),
            pl.BlockSpec((TILE, D), lambda i: (jnp.maximum(i - NTILES_P, 0) % TILES_PER_S, 0)),
        ],
        out_specs=pl.BlockSpec((TILE, D), lambda i: (i, 0)),
        compiler_params=_params(1),
        name="assemble",
    )(xp, xs, pos)


MOD_TN = 1024


def _mod_kernel(c_ref, w_ref, b_ref, o_ref):
    s = _silu(c_ref[...])
    o_ref[0] = jnp.dot(s, w_ref[0], preferred_element_type=F32,
                       precision=lax.Precision.HIGHEST) + b_ref[0]


def _modulation(cond8, w_mod, b_mod):
    out = pl.pallas_call(
        _mod_kernel,
        out_shape=jax.ShapeDtypeStruct((DEPTH, 8, 6 * D), F32),
        grid=(DEPTH, 6 * D // MOD_TN),
        in_specs=[
            pl.BlockSpec((8, D), lambda l, j: (0, 0)),
            pl.BlockSpec((1, D, MOD_TN), lambda l, j: (l, 0, j)),
            pl.BlockSpec((1, 1, MOD_TN), lambda l, j: (l, 0, j)),
        ],
        out_specs=pl.BlockSpec((1, 8, MOD_TN), lambda l, j: (l, 0, j)),
        compiler_params=_params(2),
        name="modulation",
    )(cond8, w_mod, b_mod.reshape(DEPTH, 1, 6 * D))
    return out.reshape(DEPTH, 8, 6, D)


PROJ_TM = 1024
PROJ_TN = 1024
PROJ_NMAIN = MAIN_COLS // PROJ_TN
GLU_TN = 512
PROJ_NGLU = CW // GLU_TN


def _modulated(x_ref, mod_ref, shift, scale):
    m = mod_ref[0]
    return x_ref[...] * (1.0 + m[scale:scale + 1, :]) + m[shift:shift + 1, :]


def _inproj_kernel(x_ref, mod_ref, w_ref, wa_ref, wua_ref, wug_ref, z_ref, a_ref, u_ref, h_scr):
    j = pl.program_id(1)

    @pl.when(j == 0)
    def _():
        hb = _modulated(x_ref, mod_ref, SH1, SC1).astype(BF16)
        h_scr[...] = hb
        a_ref[...] = jnp.dot(hb, wa_ref[...], preferred_element_type=F32)

    @pl.when(j < PROJ_NMAIN)
    def _():
        z_ref[...] = jnp.dot(h_scr[...], w_ref[...], preferred_element_type=F32)

    @pl.when(j >= PROJ_NMAIN)
    def _():
        h = h_scr[...]
        a = jnp.dot(h, wua_ref[...], preferred_element_type=F32)
        g = jnp.dot(h, wug_ref[...], preferred_element_type=F32)
        u_ref[...] = a * _sigmoid(g)


def _inproj(x, mod_l, w_main, w_a, w_u):
    main_j = lambda j: jnp.minimum(j, PROJ_NMAIN - 1)
    glu_j = lambda j: jnp.maximum(j - PROJ_NMAIN, 0)
    return pl.pallas_call(
        _inproj_kernel,
        out_shape=(jax.ShapeDtypeStruct((T, MAIN_COLS), F32),
                   jax.ShapeDtypeStruct((T, 128), F32),
                   jax.ShapeDtypeStruct((T, CW), F32)),
        grid=(T // PROJ_TM, PROJ_NMAIN + PROJ_NGLU),
        in_specs=[
            pl.BlockSpec((PROJ_TM, D), lambda i, j: (i, 0)),
            pl.BlockSpec((1, 6, D), lambda i, j: (_mod_row(i * PROJ_TM), 0, 0)),
            pl.BlockSpec((D, PROJ_TN), lambda i, j: (0, main_j(j))),
            pl.BlockSpec((D, 128), lambda i, j: (0, 0)),
            pl.BlockSpec((D, GLU_TN), lambda i, j: (0, glu_j(j))),
            pl.BlockSpec((D, GLU_TN), lambda i, j: (0, glu_j(j) + PROJ_NGLU)),
        ],
        out_specs=(pl.BlockSpec((PROJ_TM, PROJ_TN), lambda i, j: (i, main_j(j))),
                   pl.BlockSpec((PROJ_TM, 128), lambda i, j: (i, 0)),
                   pl.BlockSpec((PROJ_TM, GLU_TN), lambda i, j: (i, glu_j(j)))),
        scratch_shapes=[pltpu.VMEM((PROJ_TM, D), BF16)],
        compiler_params=_params(2),
        name="inproj",
    )(x, mod_l, w_main, w_a, w_u, w_u)


CONV_RC = 32
CONV_CC = 256


def _conv_kernel(uc_ref, up_ref, un_ref, cw_ref, cb_ref, lg_ref, lb_ref, wp_ref, bp_ref,
                 o_ref, pad_scr, conv_scr):
    i = pl.program_id(0)
    s = jnp.maximum(i - NTILES_P, 0) % TILES_PER_S
    has_prev = jnp.logical_and(i >= NTILES_P, s != 0)
    has_next = jnp.logical_and(i >= NTILES_P, s != TILES_PER_S - 1)
    pad_scr[0:HALO, :] = jnp.where(has_prev, up_ref[...], 0.0)
    pad_scr[HALO:HALO + TILE, :] = uc_ref[...]
    pad_scr[HALO + TILE:HALO + TILE + HALO, :] = jnp.where(has_next, un_ref[...], 0.0)

    off = HALO - CONV_K // 2
    for c in range(CW // CONV_CC):
        cs = slice(c * CONV_CC, (c + 1) * CONV_CC)

        def body(r, carry, cs=cs):
            r0 = pl.multiple_of(r * CONV_RC, CONV_RC)
            win = pad_scr[pl.ds(r0, 2 * CONV_RC), cs]
            acc = None
            for b in range(8):
                part = None
                for a in range((CONV_K + off) // 8 + 1):
                    k = 8 * a + b - off
                    if 0 <= k < CONV_K:
                        term = cw_ref[k:k + 1, cs] * win[8 * a:8 * a + CONV_RC + 8, :]
                        part = term if part is None else part + term
                part = part[b:b + CONV_RC, :]
                acc = part if acc is None else acc + part
            conv_scr[pl.ds(r0, CONV_RC), cs] = acc
            return carry

        lax.fori_loop(0, TILE // CONV_RC, body, 0)

    v = conv_scr[...] + cb_ref[...]
    y = _silu(_layer_norm(v, lg_ref[...], lb_ref[...]))
    o_ref[...] = (jnp.dot(y.astype(BF16), wp_ref[...], preferred_element_type=F32) + bp_ref[...]).astype(BF16)


def _conv_module(u, conv_w, conv_b, ln_g, ln_b, w_pw2, b_pw2):
    hb = TILE // HALO
    row = lambda a: a.reshape(1, CW)
    return pl.pallas_call(
        _conv_kernel,
        out_shape=jax.ShapeDtypeStruct((T, CW), BF16),
        grid=(NTILES,),
        in_specs=[
            pl.BlockSpec((TILE, CW), lambda i: (i, 0)),
            pl.BlockSpec((HALO, CW), lambda i: (jnp.maximum(i * hb - 1, 0), 0)),
            pl.BlockSpec((HALO, CW), lambda i: (jnp.minimum((i + 1) * hb, T // HALO - 1), 0)),
            pl.BlockSpec((CONV_K, CW), lambda i: (0, 0)),
            pl.BlockSpec((1, CW), lambda i: (0, 0)),
            pl.BlockSpec((1, CW), lambda i: (0, 0)),
            pl.BlockSpec((1, CW), lambda i: (0, 0)),
            pl.BlockSpec((CW, CW), lambda i: (0, 0)),
            pl.BlockSpec((1, CW), lambda i: (0, 0)),
        ],
        out_specs=pl.BlockSpec((TILE, CW), lambda i: (i, 0)),
        scratch_shapes=[pltpu.VMEM((TILE + 2 * HALO, CW), F32), pltpu.VMEM((TILE, CW), F32)],
        compiler_params=_params(1),
        name="conv_module",
    )(u, u, u, conv_w, row(conv_b), row(ln_g), row(ln_b), w_pw2, row(b_pw2))


def _split3(x):
    hi = x.astype(BF16)
    r1 = x - hi.astype(F32)
    mid = r1.astype(BF16)
    lo = (r1 - mid.astype(F32)).astype(BF16)
    return hi, mid, lo


def _log_sigmoid(x):
    return jnp.minimum(x, 0.0) - jnp.log(1.0 + jnp.exp(-jnp.abs(x)))


def _dot_nt(a, b):
    return lax.dot_general(a, b, (((1,), (1,)), ((), ())), preferred_element_type=F32)


def _dot_tn(a, b):
    return lax.dot_general(a, b, (((0,), (0,)), ((), ())), preferred_element_type=F32)


def _gla_kernel(q_ref, k_ref, v_ref, a_ref, wa2_ref, ba2_ref, tri_ref, s0_ref,
                o_ref, snew_ref, st_scr, *, reverse):
    n = pl.program_id(0)
    tile = (NTILES - 1 - n) if reverse else n
    is_prompt = tile < NTILES_P
    spos = jnp.maximum(tile - NTILES_P, 0) % TILES_PER_S
    seq_first = spos == (TILES_PER_S - 1 if reverse else 0)

    @pl.when(is_prompt)
    def _():
        st_scr[...] = jnp.zeros_like(st_scr)

    @pl.when(jnp.logical_and(jnp.logical_not(is_prompt), seq_first))
    def _():
        for h in range(H):
            st_scr[h] = s0_ref[0, 0, h].T

    tri = tri_ref[...]
    mask = tri > 0
    a_lr = a_ref[:, 0:2 * RANK].astype(BF16)
    n_chunks = TILE // CHUNK
    order = range(n_chunks - 1, -1, -1) if reverse else range(n_chunks)
    logits = jnp.dot(a_lr, wa2_ref[...], preferred_element_type=F32) + ba2_ref[...]
    hi, mid, lo = _split3(_log_sigmoid(logits) / TAU)
    b_all = (jnp.dot(tri, hi, preferred_element_type=F32)
             + jnp.dot(tri, mid, preferred_element_type=F32)
             + jnp.dot(tri, lo, preferred_element_type=F32))
    for h in range(H):
        ks = slice(h * DK, (h + 1) * DK)
        b = b_all[:, ks]
        q = q_ref[:, ks] * (DK ** -0.5)
        k = k_ref[:, ks]
        vb = v_ref[:, h * DV:(h + 1) * DV].astype(BF16)
        qd = (q * jnp.exp(b)).astype(BF16)
        kd = (k * jnp.exp(-b)).astype(BF16)
        att = jnp.where(mask, _dot_nt(qd, kd), 0.0).astype(BF16)
        o_intra = jnp.dot(att, vb, preferred_element_type=F32)
        st = st_scr[h]
        for c in order:
            rows = slice(c * CHUNK, (c + 1) * CHUNK)
            last = c * CHUNK if reverse else (c + 1) * CHUNK - 1
            b_last = b[last:last + 1, :]
            kl = (k[rows] * jnp.exp(b_last - b[rows])).astype(BF16)
            o_ref[rows, h * DV:(h + 1) * DV] = o_intra[rows] + _dot_nt(qd[rows], st.astype(BF16))
            st = jnp.exp(b_last) * st + _dot_tn(vb[rows], kl)
        st_scr[h] = st

    @pl.when(is_prompt)
    def _():
        for h in range(H):
            snew_ref[0, h] = st_scr[h].T


def _gla(z, a_lr, wa2_dir, ba2_dir, tri_dir, s0, direction):
    reverse = direction == 1
    tile = (lambda n: NTILES - 1 - n) if reverse else (lambda n: n)
    req = lambda n: jnp.clip((tile(n) - NTILES_P) // TILES_PER_S, 0, NB_S - 1)
    return pl.pallas_call(
        functools.partial(_gla_kernel, reverse=reverse),
        out_shape=(jax.ShapeDtypeStruct((T, GW), F32),
                   jax.ShapeDtypeStruct((NB_P, H, DK, DV), F32)),
        grid=(NTILES,),
        in_specs=[
            pl.BlockSpec((TILE, QC), lambda n: (tile(n), 0)),
            pl.BlockSpec((TILE, QC), lambda n: (tile(n), 1)),
            pl.BlockSpec((TILE, GW), lambda n: (tile(n), 1)),
            pl.BlockSpec((TILE, 128), lambda n: (tile(n), 0)),
            pl.BlockSpec((2 * RANK, QC), lambda n: (0, 0)),
            pl.BlockSpec((1, QC), lambda n: (0, 0)),
            pl.BlockSpec((TILE, TILE), lambda n: (0, 0)),
            pl.BlockSpec((1, 1, H, DK, DV), lambda n: (req(n), direction, 0, 0, 0)),
        ],
        out_specs=(pl.BlockSpec((TILE, GW), lambda n: (tile(n), 0)),
                   pl.BlockSpec((1, H, DK, DV), lambda n: (jnp.minimum(tile(n), NTILES_P - 1), 0, 0, 0))),
        scratch_shapes=[pltpu.VMEM((H, DV, DK), F32)],
        compiler_params=_params(1),
        name="gla_bwd" if reverse else "gla_fwd",
    )(z, z, z, a_lr, wa2_dir, ba2_dir, tri_dir, s0)


def _mixout_kernel(of_ref, ob_ref, g_ref, u_ref, x_ref, mod_ref, gng_ref, wo_ref, l1g_ref, l1b_ref,
                   *rest, with_router):
    if with_router:
        wr_ref, br_ref, x1_ref, h2_ref, route_ref = rest
    else:
        x1_ref, h2_ref = rest
    o = of_ref[...] + ob_ref[...]
    parts = []
    for h in range(H):
        oh = o[:, h * DV:(h + 1) * DV]
        ms = jnp.mean(oh * oh, -1, keepdims=True)
        parts.append(oh * lax.rsqrt(ms + RMS_EPS) * gng_ref[...])
    on = jnp.concatenate(parts, axis=-1) * _silu(g_ref[...])
    y = (jnp.dot(on.astype(BF16), wo_ref[0:GW, :], preferred_element_type=F32)
         + jnp.dot(u_ref[...], wo_ref[GW:D, :], preferred_element_type=F32))
    m = mod_ref[0]
    x1 = _layer_norm(ALPHA * x_ref[...] + m[G1:G1 + 1, :] * y, l1g_ref[...], l1b_ref[...])
    x1_ref[...] = x1
    h2 = x1 * (1.0 + m[SC2:SC2 + 1, :]) + m[SH2:SH2 + 1, :]
    h2_ref[...] = h2.astype(h2_ref.dtype)
    if with_router:
        h_hi = h2.astype(BF16)
        h_lo = (h2 - h_hi.astype(F32)).astype(BF16)
        p_hi = jnp.dot(h_hi, wr_ref[...], preferred_element_type=F32)
        p_lo = jnp.dot(h_lo, wr_ref[...], preferred_element_type=F32)
        logits = p_hi + pltpu.roll(p_hi, 128 - NE, 1) + p_lo + br_ref[...]
        lane = lax.broadcasted_iota(jnp.int32, logits.shape, 1)
        lg = jnp.where(lane < NE, logits, -jnp.inf)
        v1 = jnp.max(lg, -1, keepdims=True)
        i1 = jnp.min(jnp.where(lg == v1, lane, 128), -1, keepdims=True)
        lg2 = jnp.where(lane == i1, -jnp.inf, lg)
        v2 = jnp.max(lg2, -1, keepdims=True)
        i2 = jnp.min(jnp.where(lg2 == v2, lane, 128), -1, keepdims=True)
        e2 = jnp.exp(v2 - v1)
        w1 = 1.0 / (1.0 + e2)
        w2 = e2 / (1.0 + e2)
        route_ref[...] = jnp.where(lane == 0, i1.astype(F32),
                         jnp.where(lane == 1, i2.astype(F32),
                         jnp.where(lane == 2, w1, jnp.where(lane == 3, w2, 0.0))))


def _mixout(o_f, o_b, z, u2, x, mod_l, gng, w_out, ln_g, ln_b, router=None):
    row = lambda a: a.reshape(1, -1)
    full = lambda shape: pl.BlockSpec(shape, lambda i: (0,) * len(shape))
    in_specs = [
        pl.BlockSpec((TILE, GW), lambda i: (i, 0)),
        pl.BlockSpec((TILE, GW), lambda i: (i, 0)),
        pl.BlockSpec((TILE, GW), lambda i: (i, 2)),
        pl.BlockSpec((TILE, CW), lambda i: (i, 0)),
        pl.BlockSpec((TILE, D), lambda i: (i, 0)),
        pl.BlockSpec((1, 6, D), lambda i: (_mod_row(i * TILE), 0, 0)),
        full((1, DV)), full((D, D)), full((1, D)), full((1, D)),
    ]
    args = [o_f, o_b, z, u2, x, mod_l, row(gng), w_out, row(ln_g), row(ln_b)]
    out_shape = [jax.ShapeDtypeStruct((T, D), F32)]
    out_specs = [pl.BlockSpec((TILE, D), lambda i: (i, 0)), pl.BlockSpec((TILE, D), lambda i: (i, 0))]
    if router is None:
        out_shape.append(jax.ShapeDtypeStruct((T, D), BF16))
    else:
        w_r, b_r = router
        in_specs += [full((D, 128)), full((1, 128))]
        args += [w_r, b_r]
        out_shape += [jax.ShapeDtypeStruct((T, D), F32), jax.ShapeDtypeStruct((T, 128), F32)]
        out_specs.append(pl.BlockSpec((TILE, 128), lambda i: (i, 0)))
    return pl.pallas_call(
        functools.partial(_mixout_kernel, with_router=router is not None),
        out_shape=tuple(out_shape),
        grid=(NTILES,),
        in_specs=in_specs,
        out_specs=tuple(out_specs),
        compiler_params=_params(1),
        name="mixout",
    )(*args)


FFN_R = 1024
FFN_SUB = 256
FFN_TF = 512


def _ffn_kernel(te_ref, tn_ref, tb_ref, h_ref, w1_ref, w3_ref, w2_ref, o_ref, g_scr, a_scr):
    s = pl.program_id(0)
    j = pl.program_id(1)
    n = tn_ref[s]
    n_sub = (n + FFN_SUB - 1) // FFN_SUB

    @pl.when(jnp.logical_and(j == 0, n > 0))
    def _():
        o_ref[...] = jnp.zeros_like(o_ref)

    for k in range(1, FFN_R // FFN_SUB + 1):
        m = k * FFN_SUB

        @pl.when(n_sub == k)
        def _(m=m):
            hc = h_ref[0:m, :]
            g_scr[0:m, :] = jnp.dot(hc, w1_ref[0].astype(BF16), preferred_element_type=F32)
            u = jnp.dot(hc, w3_ref[0].astype(BF16), preferred_element_type=F32)
            a_scr[0:m, :] = (_silu(g_scr[0:m, :]) * u).astype(BF16)
            o_ref[0:m, :] += jnp.dot(a_scr[0:m, :], w2_ref[0].astype(BF16), preferred_element_type=F32)


def _ffn(hs, tile_expert, tile_rows, tile_block, w1, w3, w2):
    n_tiles = tile_expert.shape[0]
    nj = FF // FFN_TF

    def jj(s, j, tn):
        return jnp.where(tn[s] > 0, j, nj - 1)

    once = pl.Buffered(1)
    return pl.pallas_call(
        _ffn_kernel,
        out_shape=jax.ShapeDtypeStruct((hs.shape[0], D), F32),
        grid_spec=pltpu.PrefetchScalarGridSpec(
            num_scalar_prefetch=3,
            grid=(n_tiles, nj),
            in_specs=[
                pl.BlockSpec((FFN_R, D), lambda s, j, te, tn, tb: (tb[s], 0), pipeline_mode=once),
                pl.BlockSpec((1, D, FFN_TF), lambda s, j, te, tn, tb: (te[s], 0, jj(s, j, tn))),
                pl.BlockSpec((1, D, FFN_TF), lambda s, j, te, tn, tb: (te[s], 0, jj(s, j, tn))),
                pl.BlockSpec((1, FFN_TF, D), lambda s, j, te, tn, tb: (te[s], jj(s, j, tn), 0)),
            ],
            out_specs=pl.BlockSpec((FFN_R, D), lambda s, j, te, tn, tb: (tb[s], 0), pipeline_mode=once),
            scratch_shapes=[pltpu.VMEM((FFN_R, FFN_TF), F32), pltpu.VMEM((FFN_R, FFN_TF), BF16)],
        ),
        compiler_params=_params(2),
        name="ffn",
    )(tile_expert, tile_rows, tile_block, hs, w1, w3, w2)


def _dispatch_kernel(src_ref, live_ref, h_hbm, o_ref, buf, sem):
    c = pl.program_id(0)

    @pl.when(live_ref[c] > 0)
    def _():
        def issue(r, carry):
            t = src_ref[c * FFN_SUB + r]
            pltpu.make_async_copy(h_hbm.at[pl.ds(t, 1), :], buf.at[pl.ds(r, 1), :], sem).start()
            return carry

        lax.fori_loop(0, FFN_SUB, issue, 0)
        pltpu.make_async_copy(h_hbm.at[pl.ds(0, FFN_SUB), :], buf, sem).wait()
        o_ref[...] = buf[...].astype(BF16)

    @pl.when(live_ref[c] == 0)
    def _():
        o_ref[...] = jnp.zeros_like(o_ref)


def _dispatch(src_tok, live, h2):
    n_rows = src_tok.shape[0]
    return pl.pallas_call(
        _dispatch_kernel,
        out_shape=jax.ShapeDtypeStruct((n_rows, D), BF16),
        grid_spec=pltpu.PrefetchScalarGridSpec(
            num_scalar_prefetch=2,
            grid=(n_rows // FFN_SUB,),
            in_specs=[pl.BlockSpec(memory_space=pl.ANY)],
            out_specs=pl.BlockSpec((FFN_SUB, D), lambda c, src, live: (c, 0)),
            scratch_shapes=[pltpu.VMEM((FFN_SUB, D), F32), pltpu.SemaphoreType.DMA],
        ),
        compiler_params=_params(1),
        name="dispatch",
    )(src_tok, live, h2)


def _ln2_kernel(x_ref, f_ref, mod_ref, g_ref, b_ref, o_ref):
    m = mod_ref[0]
    o_ref[...] = _layer_norm(ALPHA * x_ref[...] + m[G2:G2 + 1, :] * f_ref[...], g_ref[...], b_ref[...])


def _ln2(x1, f, mod_l, ln_g, ln_b):
    return pl.pallas_call(
        _ln2_kernel,
        out_shape=jax.ShapeDtypeStruct((T, D), F32),
        grid=(NTILES,),
        in_specs=[
            pl.BlockSpec((TILE, D), lambda i: (i, 0)),
            pl.BlockSpec((TILE, D), lambda i: (i, 0)),
            pl.BlockSpec((1, 6, D), lambda i: (_mod_row(i * TILE), 0, 0)),
            pl.BlockSpec((1, D), lambda i: (0, 0)),
            pl.BlockSpec((1, D), lambda i: (0, 0)),
        ],
        out_specs=pl.BlockSpec((TILE, D), lambda i: (i, 0)),
        compiler_params=_params(1),
        name="ln2",
    )(x1, f, mod_l, ln_g.reshape(1, D), ln_b.reshape(1, D))


def _combine_kernel(pos_ref, x_ref, route_ref, mod_ref, g_ref, b_ref, y_hbm, op_ref, os_ref, buf, sem):
    i = pl.program_id(0)

    def issue(r, carry):
        for slot in range(2):
            p = pos_ref[2 * (i * TILE + r) + slot]
            pltpu.make_async_copy(y_hbm.at[pl.ds(p, 1), :], buf.at[slot, pl.ds(r, 1), :], sem).start()
        return carry

    lax.fori_loop(0, TILE, issue, 0)
    for slot in range(2):
        pltpu.make_async_copy(y_hbm.at[pl.ds(0, TILE), :], buf.at[slot], sem).wait()
    route = route_ref[...]
    f = route[:, 2:3] * buf[0] + route[:, 3:4] * buf[1]
    m = mod_ref[0]
    out = _layer_norm(ALPHA * x_ref[...] + m[G2:G2 + 1, :] * f, g_ref[...], b_ref[...])

    @pl.when(i < NTILES_P)
    def _():
        op_ref[...] = out

    @pl.when(i >= NTILES_P)
    def _():
        os_ref[...] = out


def _combine(pos, x1, route, mod_l, ln_g, ln_b, y):
    return pl.pallas_call(
        _combine_kernel,
        out_shape=(jax.ShapeDtypeStruct((TP, D), F32), jax.ShapeDtypeStruct((T - TP, D), F32)),
        grid_spec=pltpu.PrefetchScalarGridSpec(
            num_scalar_prefetch=1,
            grid=(NTILES,),
            in_specs=[
                pl.BlockSpec((TILE, D), lambda i, pos: (i, 0)),
                pl.BlockSpec((TILE, 128), lambda i, pos: (i, 0)),
                pl.BlockSpec((1, 6, D), lambda i, pos: (_mod_row(i * TILE), 0, 0)),
                pl.BlockSpec((1, D), lambda i, pos: (0, 0)),
                pl.BlockSpec((1, D), lambda i, pos: (0, 0)),
                pl.BlockSpec(memory_space=pl.ANY),
            ],
            out_specs=(pl.BlockSpec((TILE, D), lambda i, pos: (jnp.minimum(i, NTILES_P - 1), 0)),
                       pl.BlockSpec((TILE, D), lambda i, pos: (jnp.maximum(i - NTILES_P, 0), 0))),
            scratch_shapes=[pltpu.VMEM((2, TILE, D), F32), pltpu.SemaphoreType.DMA],
        ),
        compiler_params=_params(1),
        name="combine",
    )(pos, x1, route, mod_l, ln_g.reshape(1, D), ln_b.reshape(1, D), y)


MOE_TILES = 2 * T // FFN_R + NE


def _routing_tables(route):
    eidx = route[:, 0:2].astype(jnp.int32).reshape(-1)
    onehot = (eidx[:, None] == jnp.arange(NE, dtype=jnp.int32)[None, :]).astype(jnp.int32)
    csum = jnp.cumsum(onehot, axis=0)
    rank = jnp.take_along_axis(csum, eidx[:, None], axis=1)[:, 0] - 1
    counts = csum[-1]
    ntiles = (counts + FFN_R - 1) // FFN_R
    tend = jnp.cumsum(ntiles)
    tstart = tend - ntiles
    pos = tstart[eidx] * FFN_R + rank
    n_rows = MOE_TILES * FFN_R
    src_tok = jnp.zeros((n_rows,), jnp.int32).at[pos].set(jnp.arange(2 * T, dtype=jnp.int32) // 2)
    tiles = jnp.arange(MOE_TILES, dtype=jnp.int32)
    total = tend[-1]
    t_eff = jnp.minimum(tiles, total - 1)
    tile_expert = jnp.minimum(jnp.sum((t_eff[:, None] >= tend[None, :]).astype(jnp.int32), axis=1), NE - 1)
    tile_rows = jnp.clip(counts[tile_expert] - (t_eff - tstart[tile_expert]) * FFN_R, 0, FFN_R)
    tile_rows = jnp.where(tiles < total, tile_rows, 0)
    subs = jnp.arange(n_rows // FFN_SUB, dtype=jnp.int32)
    sub_tile = subs // (FFN_R // FFN_SUB)
    live = ((subs % (FFN_R // FFN_SUB)) * FFN_SUB < tile_rows[sub_tile]).astype(jnp.int32)
    return pos, src_tok, live, tile_expert.astype(jnp.int32), tile_rows.astype(jnp.int32), t_eff


def _grid_pos_emb():
    rows = L_S // GRID_W
    r = np.repeat(np.arange(rows, dtype=np.float64), GRID_W)
    col = np.tile(np.arange(GRID_W, dtype=np.float64), rows)
    quarter = D // 4
    freqs = 1.0 / (10000.0 ** (np.arange(quarter, dtype=np.float64) / quarter))

    def enc(p):
        ang = p[:, None] * freqs[None, :]
        return np.concatenate([np.sin(ang), np.cos(ang)], -1)

    return jnp.asarray(np.concatenate([enc(r), enc(col)], -1), dtype=F32)


def _cumsum_matrices():
    i = jnp.arange(TILE)
    same = (i[:, None] // CHUNK) == (i[None, :] // CHUNK)
    fwd = jnp.logical_and(same, i[None, :] <= i[:, None])
    bwd = jnp.logical_and(same, i[None, :] >= i[:, None])
    return fwd.astype(BF16), bwd.astype(BF16)


def kernel(x_prompt, x_sample, state_gla, c, c_ctx, w_mod, b_mod, w_in, w_a2, b_a2, gla_norm_g, conv_w, conv_b, conv_ln_g, conv_ln_b, w_pw2, b_pw2, w_out, ln1_g, ln1_b, ln2_g, ln2_b, ffn_w1, ffn_w3, ffn_w2, moe_w_router, moe_b_router, moe_w1, moe_w3, moe_w2):
    x = _assemble(x_prompt.reshape(TP, D), x_sample.reshape(T - TP, D), _grid_pos_emb())
    cond8 = jnp.zeros((8, D), F32).at[0].set(c_ctx).at[1:1 + NB_S].set(c)
    mod = _modulation(cond8, w_mod, b_mod)
    tri = _cumsum_matrices()
    dense_tiles = T // FFN_R
    dense_meta = (jnp.zeros((dense_tiles,), jnp.int32), jnp.full((dense_tiles,), FFN_R, jnp.int32),
                  jnp.arange(dense_tiles, dtype=jnp.int32))

    states = []
    for l in range(DEPTH):
        mod_l = mod[l]
        w_main = w_in[l, :, :MAIN_COLS].astype(BF16)
        w_a = jnp.pad(w_in[l, :, MAIN_COLS:MAIN_COLS + 2 * RANK], ((0, 0), (0, 128 - 2 * RANK))).astype(BF16)
        w_u = w_in[l, :, MAIN_COLS + 2 * RANK:].astype(BF16)
        z, a_lr, u = _inproj(x, mod_l, w_main, w_a, w_u)
        u2 = _conv_module(u, conv_w[l], conv_b[l], conv_ln_g[l], conv_ln_b[l],
                          w_pw2[l].astype(BF16), b_pw2[l])
        o_dir, s_dir = [], []
        for d in range(2):
            wa2 = jnp.zeros((2 * RANK, QC), F32).at[d * RANK:(d + 1) * RANK].set(w_a2[l, d]).astype(BF16)
            o, s_new = _gla(z, a_lr, wa2, b_a2[l, d].reshape(1, QC), tri[d], state_gla[:, l], d)
            o_dir.append(o)
            s_dir.append(s_new)
        states.append(jnp.stack(s_dir, axis=1))
        is_moe = l % 2 == 1
        i = l // 2
        router = None
        if is_moe:
            wr_hi = moe_w_router[i].astype(BF16)
            wr_lo = (moe_w_router[i] - wr_hi.astype(F32)).astype(BF16)
            router = (jnp.pad(jnp.concatenate([wr_hi, wr_lo], axis=1), ((0, 0), (0, 128 - 2 * NE))),
                      jnp.pad(moe_b_router[i], (0, 128 - NE)).reshape(1, 128))
        outs = _mixout(o_dir[0], o_dir[1], z, u2, x, mod_l, gla_norm_g[l], w_out[l].astype(BF16),
                       ln1_g[l], ln1_b[l], router)
        if is_moe:
            x1, h2, route = outs
            pos, src_tok, live, t_exp, t_rows, t_blk = _routing_tables(route)
            hs = _dispatch(src_tok, live, h2)
            y = _ffn(hs, t_exp, t_rows, t_blk, moe_w1[i], moe_w3[i], moe_w2[i])
            out_p, out_s = _combine(pos, x1, route, mod_l, ln2_g[l], ln2_b[l], y)
            x = jnp.concatenate([out_p, out_s], axis=0) if l + 1 < DEPTH else None
        else:
            x1, h2 = outs
            y = _ffn(h2, *dense_meta, ffn_w1[i][None], ffn_w3[i][None], ffn_w2[i][None])
            x = _ln2(x1, y, mod_l, ln2_g[l], ln2_b[l])

    if x is not None:
        out_p, out_s = x[:TP], x[TP:]
    return out_p.reshape(NB_P, L_P, D), out_s.reshape(NB_S, L_S, D), jnp.stack(states, axis=1)
```

```python
import functools

import jax
import jax.numpy as jnp
import numpy as np
from jax import lax
from jax.experimental import pallas as pl
from jax.experimental.pallas import tpu as pltpu

F32 = jnp.float32
BF16 = jnp.bfloat16

D = 2048
NB_P, L_P = 16, 256
NB_S, L_S = 4, 1024
TP = NB_P * L_P
T = TP + NB_S * L_S
DEPTH = 2
GRID_W = 64
GW = D // 2
CW = D - GW
H = 4
DV = GW // H
DK = DV // 2
QC = H * DK
RANK = 16
TAU = 16.0
CHUNK = 64
CONV_K = 31
FF = 7 * D // 2
NE = 8
ALPHA = (2 * DEPTH) ** 0.25
LN_EPS = 1e-5
RMS_EPS = 1e-6
MAIN_COLS = 2 * QC + 2 * GW

TILE = 256
NTILES = T // TILE
NTILES_P = TP // TILE
TILES_PER_S = L_S // TILE
HALO = 16

VMEM_LIMIT = 56 * 1024 * 1024

SH1, SC1, G1, SH2, SC2, G2 = range(6)


def _params(n_axes, vmem=VMEM_LIMIT):
    return pltpu.CompilerParams(dimension_semantics=("arbitrary",) * n_axes,
                                vmem_limit_bytes=vmem)


def _mod_row(tok0):
    return jnp.where(tok0 < TP, 0, 1 + (tok0 - TP) // L_S)


def _sigmoid(x):
    return 1.0 / (1.0 + jnp.exp(-x))


def _silu(x):
    return x * _sigmoid(x)


def _layer_norm(r, g, b):
    mu = jnp.mean(r, -1, keepdims=True)
    rc = r - mu
    var = jnp.mean(rc * rc, -1, keepdims=True)
    return rc * lax.rsqrt(var + LN_EPS) * g + b


def _assemble_kernel(xp_ref, xs_ref, pos_ref, o_ref):
    i = pl.program_id(0)

    @pl.when(i < NTILES_P)
    def _():
        o_ref[...] = xp_ref[...]

    @pl.when(i >= NTILES_P)
    def _():
        o_ref[...] = xs_ref[...] + pos_ref[...]


def _assemble(xp, xs, pos):
    return pl.pallas_call(
        _assemble_kernel,
        out_shape=jax.ShapeDtypeStruct((T, D), F32),
        grid=(NTILES,),
        in_specs=[
            pl.BlockSpec((TILE, D), lambda i: (jnp.minimum(i, NTILES_P - 1), 0)),
            pl.BlockSpec((TILE, D), lambda i: (jnp.maximum(i - NTILES_P, 0), 0)),
            pl.BlockSpec((TILE, D), lambda i: (jnp.maximum(i - NTILES_P, 0) % TILES_PER_S, 0)),
        ],
        out_specs=pl.BlockSpec((TILE, D), lambda i: (i, 0)),
        compiler_params=_params(1),
        name="assemble",
    )(xp, xs, pos)


MOD_TN = 1024


def _mod_kernel(c_ref, w_ref, b_ref, o_ref):
    s = _silu(c_ref[...])
    o_ref[0] = jnp.dot(s, w_ref[0], preferred_element_type=F32,
                       precision=lax.Precision.HIGHEST) + b_ref[0]


def _modulation(cond8, w_mod, b_mod):
    out = pl.pallas_call(
        _mod_kernel,
        out_shape=jax.ShapeDtypeStruct((DEPTH, 8, 6 * D), F32),
        grid=(DEPTH, 6 * D // MOD_TN),
        in_specs=[
            pl.BlockSpec((8, D), lambda l, j: (0, 0)),
            pl.BlockSpec((1, D, MOD_TN), lambda l, j: (l, 0, j)),
            pl.BlockSpec((1, 1, MOD_TN), lambda l, j: (l, 0, j)),
        ],
        out_specs=pl.BlockSpec((1, 8, MOD_TN), lambda l, j: (l, 0, j)),
        compiler_params=_params(2),
        name="modulation",
    )(cond8, w_mod, b_mod.reshape(DEPTH, 1, 6 * D))
    return out.reshape(DEPTH, 8, 6, D)


PROJ_TM = 1024
PROJ_TN = 512
PROJ_NMAIN = MAIN_COLS // PROJ_TN
GLU_TN = 512
PROJ_NGLU = CW // GLU_TN


def _modulated(x_ref, mod_ref, shift, scale):
    m = mod_ref[0]
    return x_ref[...] * (1.0 + m[scale:scale + 1, :]) + m[shift:shift + 1, :]


def _inproj_kernel(x_ref, mod_ref, w_ref, wa_ref, wua_ref, wug_ref, z_ref, a_ref, u_ref, h_scr):
    j = pl.program_id(1)

    @pl.when(j == 0)
    def _():
        hb = _modulated(x_ref, mod_ref, SH1, SC1).astype(BF16)
        h_scr[...] = hb
        a_ref[...] = jnp.dot(hb, wa_ref[...], preferred_element_type=F32)

    @pl.when(j < PROJ_NMAIN)
    def _():
        z_ref[...] = jnp.dot(h_scr[...], w_ref[0].astype(BF16), preferred_element_type=F32)

    @pl.when(j >= PROJ_NMAIN)
    def _():
        h = h_scr[...]
        a = jnp.dot(h, wua_ref[...], preferred_element_type=F32)
        g = jnp.dot(h, wug_ref[...], preferred_element_type=F32)
        u_ref[...] = a * _sigmoid(g)


def _inproj(x, mod_l, w_in, layer, w_a, w_u):
    main_j = lambda j: jnp.minimum(j, PROJ_NMAIN - 1)
    glu_j = lambda j: jnp.maximum(j - PROJ_NMAIN, 0)
    return pl.pallas_call(
        _inproj_kernel,
        out_shape=(jax.ShapeDtypeStruct((T, MAIN_COLS), F32),
                   jax.ShapeDtypeStruct((T, 128), F32),
                   jax.ShapeDtypeStruct((T, CW), F32)),
        grid=(T // PROJ_TM, PROJ_NMAIN + PROJ_NGLU),
        in_specs=[
            pl.BlockSpec((PROJ_TM, D), lambda i, j: (i, 0)),
            pl.BlockSpec((1, 6, D), lambda i, j: (_mod_row(i * PROJ_TM), 0, 0)),
            pl.BlockSpec((1, D, PROJ_TN), lambda i, j: (layer, 0, main_j(j))),
            pl.BlockSpec((D, 128), lambda i, j: (0, 0)),
            pl.BlockSpec((D, GLU_TN), lambda i, j: (0, glu_j(j))),
            pl.BlockSpec((D, GLU_TN), lambda i, j: (0, glu_j(j) + PROJ_NGLU)),
        ],
        out_specs=(pl.BlockSpec((PROJ_TM, PROJ_TN), lambda i, j: (i, main_j(j))),
                   pl.BlockSpec((PROJ_TM, 128), lambda i, j: (i, 0)),
                   pl.BlockSpec((PROJ_TM, GLU_TN), lambda i, j: (i, glu_j(j)))),
        scratch_shapes=[pltpu.VMEM((PROJ_TM, D), BF16)],
        compiler_params=_params(2),
        name="inproj",
    )(x, mod_l, w_in, w_a, w_u, w_u)


CONV_RC = 32
CONV_CC = 256


def _conv_kernel(uc_ref, up_ref, un_ref, cw_ref, cb_ref, lg_ref, lb_ref, wp_ref, bp_ref,
                 o_ref, pad_scr, conv_scr):
    i = pl.program_id(0)
    s = jnp.maximum(i - NTILES_P, 0) % TILES_PER_S
    has_prev = jnp.logical_and(i >= NTILES_P, s != 0)
    has_next = jnp.logical_and(i >= NTILES_P, s != TILES_PER_S - 1)
    pad_scr[0:HALO, :] = jnp.where(has_prev, up_ref[...], 0.0)
    pad_scr[HALO:HALO + TILE, :] = uc_ref[...]
    pad_scr[HALO + TILE:HALO + TILE + HALO, :] = jnp.where(has_next, un_ref[...], 0.0)

    off = HALO - CONV_K // 2
    for c in range(CW // CONV_CC):
        cs = slice(c * CONV_CC, (c + 1) * CONV_CC)

        def body(r, carry, cs=cs):
            r0 = pl.multiple_of(r * CONV_RC, CONV_RC)
            win = pad_scr[pl.ds(r0, 2 * CONV_RC), cs]
            acc = None
            for b in range(8):
                part = None
                for a in range((CONV_K + off) // 8 + 1):
                    k = 8 * a + b - off
                    if 0 <= k < CONV_K:
                        term = cw_ref[k:k + 1, cs] * win[8 * a:8 * a + CONV_RC + 8, :]
                        part = term if part is None else part + term
                part = part[b:b + CONV_RC, :]
                acc = part if acc is None else acc + part
            conv_scr[pl.ds(r0, CONV_RC), cs] = acc
            return carry

        lax.fori_loop(0, TILE // CONV_RC, body, 0)

    v = conv_scr[...] + cb_ref[...]
    y = _silu(_layer_norm(v, lg_ref[...], lb_ref[...]))
    o_ref[...] = (jnp.dot(y.astype(BF16), wp_ref[...], preferred_element_type=F32) + bp_ref[...]).astype(BF16)


def _conv_module(u, conv_w, conv_b, ln_g, ln_b, w_pw2, b_pw2):
    hb = TILE // HALO
    row = lambda a: a.reshape(1, CW)
    return pl.pallas_call(
        _conv_kernel,
        out_shape=jax.ShapeDtypeStruct((T, CW), BF16),
        grid=(NTILES,),
        in_specs=[
            pl.BlockSpec((TILE, CW), lambda i: (i, 0)),
            pl.BlockSpec((HALO, CW), lambda i: (jnp.maximum(i * hb - 1, 0), 0)),
            pl.BlockSpec((HALO, CW), lambda i: (jnp.minimum((i + 1) * hb, T // HALO - 1), 0)),
            pl.BlockSpec((CONV_K, CW), lambda i: (0, 0)),
            pl.BlockSpec((1, CW), lambda i: (0, 0)),
            pl.BlockSpec((1, CW), lambda i: (0, 0)),
            pl.BlockSpec((1, CW), lambda i: (0, 0)),
            pl.BlockSpec((CW, CW), lambda i: (0, 0)),
            pl.BlockSpec((1, CW), lambda i: (0, 0)),
        ],
        out_specs=pl.BlockSpec((TILE, CW), lambda i: (i, 0)),
        scratch_shapes=[pltpu.VMEM((TILE + 2 * HALO, CW), F32), pltpu.VMEM((TILE, CW), F32)],
        compiler_params=_params(1),
        name="conv_module",
    )(u, u, u, conv_w, row(conv_b), row(ln_g), row(ln_b), w_pw2, row(b_pw2))


def _split3(x):
    hi = x.astype(BF16)
    r1 = x - hi.astype(F32)
    mid = r1.astype(BF16)
    lo = (r1 - mid.astype(F32)).astype(BF16)
    return hi, mid, lo


def _log_sigmoid(x):
    return jnp.minimum(x, 0.0) - jnp.log(1.0 + jnp.exp(-jnp.abs(x)))


def _dot_nt(a, b):
    return lax.dot_general(a, b, (((1,), (1,)), ((), ())), preferred_element_type=F32)


def _dot_tn(a, b):
    return lax.dot_general(a, b, (((0,), (0,)), ((), ())), preferred_element_type=F32)


def _gla_kernel(q_ref, k_ref, v_ref, a_ref, wa2_ref, ba2_ref, tri_ref, s0_ref,
                o_ref, snew_ref, st_scr, *, reverse):
    n = pl.program_id(0)
    tile = (NTILES - 1 - n) if reverse else n
    is_prompt = tile < NTILES_P
    spos = jnp.maximum(tile - NTILES_P, 0) % TILES_PER_S
    seq_first = spos == (TILES_PER_S - 1 if reverse else 0)

    @pl.when(is_prompt)
    def _():
        st_scr[...] = jnp.zeros_like(st_scr)

    @pl.when(jnp.logical_and(jnp.logical_not(is_prompt), seq_first))
    def _():
        for h in range(H):
            st_scr[h] = s0_ref[0, 0, h].T

    tri = tri_ref[...]
    mask = tri > 0
    a_lr = a_ref[:, 0:2 * RANK].astype(BF16)
    n_chunks = TILE // CHUNK
    order = range(n_chunks - 1, -1, -1) if reverse else range(n_chunks)
    logits = jnp.dot(a_lr, wa2_ref[...], preferred_element_type=F32) + ba2_ref[...]
    hi, mid, lo = _split3(_log_sigmoid(logits) / TAU)
    b_all = (jnp.dot(tri, hi, preferred_element_type=F32)
             + jnp.dot(tri, mid, preferred_element_type=F32)
             + jnp.dot(tri, lo, preferred_element_type=F32))
    for h in range(H):
        ks = slice(h * DK, (h + 1) * DK)
        b = b_all[:, ks]
        q = q_ref[:, ks] * (DK ** -0.5)
        k = k_ref[:, ks]
        vb = v_ref[:, h * DV:(h + 1) * DV].astype(BF16)
        qd = (q * jnp.exp(b)).astype(BF16)
        kd = (k * jnp.exp(-b)).astype(BF16)
        att = jnp.where(mask, _dot_nt(qd, kd), 0.0).astype(BF16)
        o_intra = jnp.dot(att, vb, preferred_element_type=F32)
        st = st_scr[h]
        for c in order:
            rows = slice(c * CHUNK, (c + 1) * CHUNK)
            last = c * CHUNK if reverse else (c + 1) * CHUNK - 1
            b_last = b[last:last + 1, :]
            kl = (k[rows] * jnp.exp(b_last - b[rows])).astype(BF16)
            o_ref[rows, h * DV:(h + 1) * DV] = o_intra[rows] + _dot_nt(qd[rows], st.astype(BF16))
            st = jnp.exp(b_last) * st + _dot_tn(vb[rows], kl)
        st_scr[h] = st

    @pl.when(is_prompt)
    def _():
        for h in range(H):
            snew_ref[0, h] = st_scr[h].T


def _gla(z, a_lr, wa2_dir, ba2_dir, tri_dir, s0, direction):
    reverse = direction == 1
    tile = (lambda n: NTILES - 1 - n) if reverse else (lambda n: n)
    req = lambda n: jnp.clip((tile(n) - NTILES_P) // TILES_PER_S, 0, NB_S - 1)
    return pl.pallas_call(
        functools.partial(_gla_kernel, reverse=reverse),
        out_shape=(jax.ShapeDtypeStruct((T, GW), F32),
                   jax.ShapeDtypeStruct((NB_P, H, DK, DV), F32)),
        grid=(NTILES,),
        in_specs=[
            pl.BlockSpec((TILE, QC), lambda n: (tile(n), 0)),
            pl.BlockSpec((TILE, QC), lambda n: (tile(n), 1)),
            pl.BlockSpec((TILE, GW), lambda n: (tile(n), 1)),
            pl.BlockSpec((TILE, 128), lambda n: (tile(n), 0)),
            pl.BlockSpec((2 * RANK, QC), lambda n: (0, 0)),
            pl.BlockSpec((1, QC), lambda n: (0, 0)),
            pl.BlockSpec((TILE, TILE), lambda n: (0, 0)),
            pl.BlockSpec((1, 1, H, DK, DV), lambda n: (req(n), direction, 0, 0, 0)),
        ],
        out_specs=(pl.BlockSpec((TILE, GW), lambda n: (tile(n), 0)),
                   pl.BlockSpec((1, H, DK, DV), lambda n: (jnp.minimum(tile(n), NTILES_P - 1), 0, 0, 0))),
        scratch_shapes=[pltpu.VMEM((H, DV, DK), F32)],
        compiler_params=_params(1),
        name="gla_bwd" if reverse else "gla_fwd",
    )(z, z, z, a_lr, wa2_dir, ba2_dir, tri_dir, s0)


def _mixout_kernel(of_ref, ob_ref, g_ref, u_ref, x_ref, mod_ref, gng_ref, wo_ref, l1g_ref, l1b_ref,
                   *rest, with_router):
    if with_router:
        wr_ref, br_ref, ltri_ref, x1_ref, h2_ref, route_ref, cnt_ref, run_scr = rest
    else:
        x1_ref, h2_ref = rest
    o = of_ref[...] + ob_ref[...]
    parts = []
    for h in range(H):
        oh = o[:, h * DV:(h + 1) * DV]
        ms = jnp.mean(oh * oh, -1, keepdims=True)
        parts.append(oh * lax.rsqrt(ms + RMS_EPS) * gng_ref[...])
    on = jnp.concatenate(parts, axis=-1) * _silu(g_ref[...])
    y = (jnp.dot(on.astype(BF16), wo_ref[0:GW, :], preferred_element_type=F32)
         + jnp.dot(u_ref[...], wo_ref[GW:D, :], preferred_element_type=F32))
    m = mod_ref[0]
    x1 = _layer_norm(ALPHA * x_ref[...] + m[G1:G1 + 1, :] * y, l1g_ref[...], l1b_ref[...])
    x1_ref[...] = x1
    h2 = x1 * (1.0 + m[SC2:SC2 + 1, :]) + m[SH2:SH2 + 1, :]
    h2_ref[...] = h2.astype(h2_ref.dtype)
    if with_router:
        h_hi = h2.astype(BF16)
        h_lo = (h2 - h_hi.astype(F32)).astype(BF16)
        p_hi = jnp.dot(h_hi, wr_ref[...], preferred_element_type=F32)
        p_lo = jnp.dot(h_lo, wr_ref[...], preferred_element_type=F32)
        logits = p_hi + pltpu.roll(p_hi, 128 - NE, 1) + p_lo + br_ref[...]
        lane = lax.broadcasted_iota(jnp.int32, logits.shape, 1)
        lg = jnp.where(lane < NE, logits, -jnp.inf)
        v1 = jnp.max(lg, -1, keepdims=True)
        i1 = jnp.min(jnp.where(lg == v1, lane, 128), -1, keepdims=True)
        lg2 = jnp.where(lane == i1, -jnp.inf, lg)
        v2 = jnp.max(lg2, -1, keepdims=True)
        i2 = jnp.min(jnp.where(lg2 == v2, lane, 128), -1, keepdims=True)
        e2 = jnp.exp(v2 - v1)
        w1 = 1.0 / (1.0 + e2)
        w2 = e2 / (1.0 + e2)
        @pl.when(pl.program_id(0) == 0)
        def _():
            run_scr[...] = jnp.zeros_like(run_scr)

        hit1 = lane == i1
        hit2 = lane == i2
        one1 = jnp.where(hit1, 1.0, 0.0)
        one2 = jnp.where(hit2, 1.0, 0.0)
        before1 = jnp.dot(ltri_ref[...], one1.astype(BF16), preferred_element_type=F32)
        before2 = jnp.dot(ltri_ref[...], one2.astype(BF16), preferred_element_type=F32)
        tot1 = jnp.sum(one1, axis=0, keepdims=True)
        tot2 = jnp.sum(one2, axis=0, keepdims=True)
        run = run_scr[...]
        rank1 = jnp.sum(jnp.where(hit1, before1 + run, 0.0), -1, keepdims=True)
        rank2 = jnp.sum(jnp.where(hit2, before2 + (run + tot1), 0.0), -1, keepdims=True)
        run = run + tot1 + tot2
        run_scr[...] = run
        cnt_ref[...] = jnp.broadcast_to(run, cnt_ref.shape)
        route_ref[...] = jnp.where(lane == 0, i1.astype(F32),
                         jnp.where(lane == 1, i2.astype(F32),
                         jnp.where(lane == 2, w1,
                         jnp.where(lane == 3, w2,
                         jnp.where(lane == 4, rank1, jnp.where(lane == 5, rank2, 0.0))))))


def _mixout(o_f, o_b, z, u2, x, mod_l, gng, w_out, ln_g, ln_b, router=None):
    row = lambda a: a.reshape(1, -1)
    full = lambda shape: pl.BlockSpec(shape, lambda i: (0,) * len(shape))
    in_specs = [
        pl.BlockSpec((TILE, GW), lambda i: (i, 0)),
        pl.BlockSpec((TILE, GW), lambda i: (i, 0)),
        pl.BlockSpec((TILE, GW), lambda i: (i, 2)),
        pl.BlockSpec((TILE, CW), lambda i: (i, 0)),
        pl.BlockSpec((TILE, D), lambda i: (i, 0)),
        pl.BlockSpec((1, 6, D), lambda i: (_mod_row(i * TILE), 0, 0)),
        full((1, DV)), full((D, D)), full((1, D)), full((1, D)),
    ]
    args = [o_f, o_b, z, u2, x, mod_l, row(gng), w_out, row(ln_g), row(ln_b)]
    out_shape = [jax.ShapeDtypeStruct((T, D), F32)]
    out_specs = [pl.BlockSpec((TILE, D), lambda i: (i, 0)), pl.BlockSpec((TILE, D), lambda i: (i, 0))]
    scratch = []
    if router is None:
        out_shape.append(jax.ShapeDtypeStruct((T, D), BF16))
    else:
        w_r, b_r = router
        idx = jnp.arange(TILE)
        ltri = (idx[None, :] < idx[:, None]).astype(BF16)
        in_specs += [full((D, 128)), full((1, 128)), full((TILE, TILE))]
        args += [w_r, b_r, ltri]
        out_shape += [jax.ShapeDtypeStruct((T, D), F32),
                      jax.ShapeDtypeStruct((T, 128), F32), jax.ShapeDtypeStruct((8, 128), F32)]
        out_specs += [pl.BlockSpec((TILE, 128), lambda i: (i, 0)), pl.BlockSpec((8, 128), lambda i: (0, 0))]
        scratch = [pltpu.VMEM((1, 128), F32)]
    return pl.pallas_call(
        functools.partial(_mixout_kernel, with_router=router is not None),
        out_shape=tuple(out_shape),
        grid=(NTILES,),
        in_specs=in_specs,
        out_specs=tuple(out_specs),
        scratch_shapes=scratch,
        compiler_params=_params(1),
        name="mixout",
    )(*args)


FFN_R = 1024
FFN_SUB = 256
FFN_TF = 512


def _ffn_kernel(te_ref, tn_ref, tb_ref, h_ref, w1_ref, w3_ref, w2_ref, o_ref, g_scr, a_scr, *h_scr, wide_rows):
    s = pl.program_id(0)
    j = pl.program_id(1)
    n = tn_ref[s]
    n_sub = (n + FFN_SUB - 1) // FFN_SUB

    @pl.when(jnp.logical_and(j == 0, n > 0))
    def _():
        o_ref[...] = jnp.zeros_like(o_ref)

    rows_ref = h_scr[0] if wide_rows else h_ref
    for k in range(1, FFN_R // FFN_SUB + 1):
        m = k * FFN_SUB

        if wide_rows:
            @pl.when(jnp.logical_and(n_sub == k, j == 0))
            def _(m=m):
                rows_ref[0:m, :] = h_ref[0:m, :].astype(BF16)

        @pl.when(n_sub == k)
        def _(m=m):
            hc = rows_ref[0:m, :]
            g_scr[0:m, :] = jnp.dot(hc, w1_ref[0].astype(BF16), preferred_element_type=F32)
            u = jnp.dot(hc, w3_ref[0].astype(BF16), preferred_element_type=F32)
            a_scr[0:m, :] = (_silu(g_scr[0:m, :]) * u).astype(BF16)
            o_ref[0:m, :] += jnp.dot(a_scr[0:m, :], w2_ref[0].astype(BF16), preferred_element_type=F32)


def _ffn(hs, tile_expert, tile_rows, tile_block, w1, w3, w2):
    n_tiles = tile_expert.shape[0]
    nj = FF // FFN_TF

    def jj(s, j, tn):
        return jnp.where(tn[s] > 0, j, nj - 1)

    once = pl.Buffered(1)
    wide_rows = hs.dtype == F32
    scratch = [pltpu.VMEM((FFN_R, FFN_TF), F32), pltpu.VMEM((FFN_R, FFN_TF), BF16)]
    if wide_rows:
        scratch.append(pltpu.VMEM((FFN_R, D), BF16))
    return pl.pallas_call(
        functools.partial(_ffn_kernel, wide_rows=wide_rows),
        out_shape=jax.ShapeDtypeStruct((hs.shape[0], D), F32),
        grid_spec=pltpu.PrefetchScalarGridSpec(
            num_scalar_prefetch=3,
            grid=(n_tiles, nj),
            in_specs=[
                pl.BlockSpec((FFN_R, hs.shape[1]), lambda s, j, te, tn, tb: (tb[s], 0), pipeline_mode=once),
                pl.BlockSpec((1, D, FFN_TF), lambda s, j, te, tn, tb: (te[s], 0, jj(s, j, tn))),
                pl.BlockSpec((1, D, FFN_TF), lambda s, j, te, tn, tb: (te[s], 0, jj(s, j, tn))),
                pl.BlockSpec((1, FFN_TF, D), lambda s, j, te, tn, tb: (te[s], jj(s, j, tn), 0)),
            ],
            out_specs=pl.BlockSpec((FFN_R, D), lambda s, j, te, tn, tb: (tb[s], 0), pipeline_mode=once),
            scratch_shapes=scratch,
        ),
        compiler_params=_params(2),
        name="ffn",
    )(tile_expert, tile_rows, tile_block, hs, w1, w3, w2)


DMA_UNROLL = 8


def _dispatch_copy(h_hbm, o_hbm, sem, src_rows, dst_rows):
    return pltpu.make_async_copy(h_hbm.at[src_rows, :], o_hbm.at[dst_rows, :], sem)


def _dispatch_kernel(pos_ref, pad_ref, h_hbm, o_hbm, zero_scr, sem, zsem):
    i = pl.program_id(0)
    n_copies = 2 * TILE

    @pl.when(i == 0)
    def _():
        zero_scr[...] = jnp.zeros_like(zero_scr)
        for e in range(NE):
            first, count = pad_ref[e], pad_ref[NE + e]

            def zero_copy(r, first=first):
                return pltpu.make_async_copy(zero_scr, o_hbm.at[pl.ds(first + r, 1), :], zsem)

            def start(r, carry):
                zero_copy(r).start()
                return carry

            def wait(r, carry):
                zero_copy(r).wait()
                return carry

            lax.fori_loop(0, count, start, 0)
            lax.fori_loop(0, count, wait, 0)

    def issue(r, carry):
        t = i * TILE + r
        for slot in range(2):
            p = pos_ref[2 * t + slot]
            _dispatch_copy(h_hbm, o_hbm, sem.at[i % 2], pl.ds(t, 1), pl.ds(p, 1)).start()
        return carry

    lax.fori_loop(0, TILE, issue, 0, unroll=DMA_UNROLL)

    def wait_step(step):
        _dispatch_copy(h_hbm, o_hbm, sem.at[step % 2], pl.ds(0, n_copies), pl.ds(0, n_copies)).wait()

    @pl.when(i > 0)
    def _():
        wait_step(i - 1)

    @pl.when(i == NTILES - 1)
    def _():
        wait_step(i)


def _dispatch(pos, pad, h2):
    return pl.pallas_call(
        _dispatch_kernel,
        out_shape=jax.ShapeDtypeStruct((MOE_TILES * FFN_R, D), F32),
        grid_spec=pltpu.PrefetchScalarGridSpec(
            num_scalar_prefetch=2,
            grid=(NTILES,),
            in_specs=[pl.BlockSpec(memory_space=pl.ANY)],
            out_specs=pl.BlockSpec(memory_space=pl.ANY),
            scratch_shapes=[pltpu.VMEM((1, D), F32), pltpu.SemaphoreType.DMA((2,)), pltpu.SemaphoreType.DMA],
        ),
        compiler_params=_params(1),
        name="dispatch",
    )(pos, pad, h2)


def _ln2_kernel(x_ref, f_ref, mod_ref, g_ref, b_ref, o_ref):
    m = mod_ref[0]
    o_ref[...] = _layer_norm(ALPHA * x_ref[...] + m[G2:G2 + 1, :] * f_ref[...], g_ref[...], b_ref[...])


def _ln2(x1, f, mod_l, ln_g, ln_b):
    return pl.pallas_call(
        _ln2_kernel,
        out_shape=jax.ShapeDtypeStruct((T, D), F32),
        grid=(NTILES,),
        in_specs=[
            pl.BlockSpec((TILE, D), lambda i: (i, 0)),
            pl.BlockSpec((TILE, D), lambda i: (i, 0)),
            pl.BlockSpec((1, 6, D), lambda i: (_mod_row(i * TILE), 0, 0)),
            pl.BlockSpec((1, D), lambda i: (0, 0)),
            pl.BlockSpec((1, D), lambda i: (0, 0)),
        ],
        out_specs=pl.BlockSpec((TILE, D), lambda i: (i, 0)),
        compiler_params=_params(1),
        name="ln2",
    )(x1, f, mod_l, ln_g.reshape(1, D), ln_b.reshape(1, D))


def _combine_kernel(pos_ref, x_ref, route_ref, mod_ref, g_ref, b_ref, y_hbm, op_ref, os_ref, buf, sem):
    i = pl.program_id(0)

    def gather(tile):
        def issue(r, carry):
            for slot in range(2):
                p = pos_ref[2 * (tile * TILE + r) + slot]
                pltpu.make_async_copy(y_hbm.at[pl.ds(p, 1), :], buf.at[tile % 2, slot, pl.ds(r, 1), :],
                                      sem.at[tile % 2]).start()
            return carry

        lax.fori_loop(0, TILE, issue, 0, unroll=DMA_UNROLL)

    @pl.when(i == 0)
    def _():
        gather(i)

    @pl.when(i + 1 < NTILES)
    def _():
        gather(i + 1)

    for slot in range(2):
        pltpu.make_async_copy(y_hbm.at[pl.ds(0, TILE), :], buf.at[i % 2, slot], sem.at[i % 2]).wait()
    route = route_ref[...]
    f = route[:, 2:3] * buf[i % 2, 0] + route[:, 3:4] * buf[i % 2, 1]
    m = mod_ref[0]
    out = _layer_norm(ALPHA * x_ref[...] + m[G2:G2 + 1, :] * f, g_ref[...], b_ref[...])

    @pl.when(i < NTILES_P)
    def _():
        op_ref[...] = out

    @pl.when(i >= NTILES_P)
    def _():
        os_ref[...] = out


def _combine(pos, x1, route, mod_l, ln_g, ln_b, y):
    return pl.pallas_call(
        _combine_kernel,
        out_shape=(jax.ShapeDtypeStruct((TP, D), F32), jax.ShapeDtypeStruct((T - TP, D), F32)),
        grid_spec=pltpu.PrefetchScalarGridSpec(
            num_scalar_prefetch=1,
            grid=(NTILES,),
            in_specs=[
                pl.BlockSpec((TILE, D), lambda i, pos: (i, 0)),
                pl.BlockSpec((TILE, 128), lambda i, pos: (i, 0)),
                pl.BlockSpec((1, 6, D), lambda i, pos: (_mod_row(i * TILE), 0, 0)),
                pl.BlockSpec((1, D), lambda i, pos: (0, 0)),
                pl.BlockSpec((1, D), lambda i, pos: (0, 0)),
                pl.BlockSpec(memory_space=pl.ANY),
            ],
            out_specs=(pl.BlockSpec((TILE, D), lambda i, pos: (jnp.minimum(i, NTILES_P - 1), 0)),
                       pl.BlockSpec((TILE, D), lambda i, pos: (jnp.maximum(i - NTILES_P, 0), 0))),
            scratch_shapes=[pltpu.VMEM((2, 2, TILE, D), F32), pltpu.SemaphoreType.DMA((2,))],
        ),
        compiler_params=_params(1),
        name="combine",
    )(pos, x1, route, mod_l, ln_g.reshape(1, D), ln_b.reshape(1, D), y)


MOE_TILES = 2 * T // FFN_R + NE


def _routing_tables(route, counts_f):
    eidx = route[:, 0:2].astype(jnp.int32)
    rank = route[:, 4:6].astype(jnp.int32)
    counts = counts_f[0, :NE].astype(jnp.int32)
    ntiles = (counts + FFN_R - 1) // FFN_R
    tend = jnp.cumsum(ntiles)
    tstart = tend - ntiles
    experts = jnp.arange(NE, dtype=jnp.int32)
    first_row = jnp.sum(jnp.where(eidx[..., None] == experts, tstart * FFN_R, 0), axis=-1)
    pos = (first_row + rank).reshape(-1)
    tiles = jnp.arange(MOE_TILES, dtype=jnp.int32)
    total = tend[-1]
    t_eff = jnp.minimum(tiles, total - 1)
    tile_expert = jnp.minimum(jnp.sum((t_eff[:, None] >= tend[None, :]).astype(jnp.int32), axis=1), NE - 1)
    tile_rows = jnp.clip(counts[tile_expert] - (t_eff - tstart[tile_expert]) * FFN_R, 0, FFN_R)
    tile_rows = jnp.where(tiles < total, tile_rows, 0)
    pad = jnp.concatenate([tstart * FFN_R + counts, (-counts) % FFN_SUB]).astype(jnp.int32)
    return pos, pad, tile_expert.astype(jnp.int32), tile_rows.astype(jnp.int32), t_eff


def _grid_pos_emb():
    rows = L_S // GRID_W
    r = np.repeat(np.arange(rows, dtype=np.float64), GRID_W)
    col = np.tile(np.arange(GRID_W, dtype=np.float64), rows)
    quarter = D // 4
    freqs = 1.0 / (10000.0 ** (np.arange(quarter, dtype=np.float64) / quarter))

    def enc(p):
        ang = p[:, None] * freqs[None, :]
        return np.concatenate([np.sin(ang), np.cos(ang)], -1)

    return jnp.asarray(np.concatenate([enc(r), enc(col)], -1), dtype=F32)


def _cumsum_matrices():
    i = jnp.arange(TILE)
    same = (i[:, None] // CHUNK) == (i[None, :] // CHUNK)
    fwd = jnp.logical_and(same, i[None, :] <= i[:, None])
    bwd = jnp.logical_and(same, i[None, :] >= i[:, None])
    return fwd.astype(BF16), bwd.astype(BF16)


def kernel(x_prompt, x_sample, state_gla, c, c_ctx, w_mod, b_mod, w_in, w_a2, b_a2, gla_norm_g, conv_w, conv_b, conv_ln_g, conv_ln_b, w_pw2, b_pw2, w_out, ln1_g, ln1_b, ln2_g, ln2_b, ffn_w1, ffn_w3, ffn_w2, moe_w_router, moe_b_router, moe_w1, moe_w3, moe_w2):
    x = _assemble(x_prompt.reshape(TP, D), x_sample.reshape(T - TP, D), _grid_pos_emb())
    cond8 = jnp.zeros((8, D), F32).at[0].set(c_ctx).at[1:1 + NB_S].set(c)
    mod = _modulation(cond8, w_mod, b_mod)
    tri = _cumsum_matrices()
    dense_tiles = T // FFN_R
    dense_meta = (jnp.zeros((dense_tiles,), jnp.int32), jnp.full((dense_tiles,), FFN_R, jnp.int32),
                  jnp.arange(dense_tiles, dtype=jnp.int32))

    states = []
    for l in range(DEPTH):
        mod_l = mod[l]
        w_a = jnp.pad(w_in[l, :, MAIN_COLS:MAIN_COLS + 2 * RANK], ((0, 0), (0, 128 - 2 * RANK))).astype(BF16)
        w_u = w_in[l, :, MAIN_COLS + 2 * RANK:].astype(BF16)
        z, a_lr, u = _inproj(x, mod_l, w_in, l, w_a, w_u)
        u2 = _conv_module(u, conv_w[l], conv_b[l], conv_ln_g[l], conv_ln_b[l],
                          w_pw2[l].astype(BF16), b_pw2[l])
        o_dir, s_dir = [], []
        for d in range(2):
            wa2 = jnp.zeros((2 * RANK, QC), F32).at[d * RANK:(d + 1) * RANK].set(w_a2[l, d]).astype(BF16)
            o, s_new = _gla(z, a_lr, wa2, b_a2[l, d].reshape(1, QC), tri[d], state_gla[:, l], d)
            o_dir.append(o)
            s_dir.append(s_new)
        states.append(jnp.stack(s_dir, axis=1))
        is_moe = l % 2 == 1
        i = l // 2
        router = None
        if is_moe:
            wr_hi = moe_w_router[i].astype(BF16)
            wr_lo = (moe_w_router[i] - wr_hi.astype(F32)).astype(BF16)
            router = (jnp.pad(jnp.concatenate([wr_hi, wr_lo], axis=1), ((0, 0), (0, 128 - 2 * NE))),
                      jnp.pad(moe_b_router[i], (0, 128 - NE)).reshape(1, 128))
        outs = _mixout(o_dir[0], o_dir[1], z, u2, x, mod_l, gla_norm_g[l], w_out[l].astype(BF16),
                       ln1_g[l], ln1_b[l], router)
        if is_moe:
            x1, h2, route, counts = outs
            pos, pad, t_exp, t_rows, t_blk = _routing_tables(route, counts)
            hs = _dispatch(pos, pad, h2)
            y = _ffn(hs, t_exp, t_rows, t_blk, moe_w1[i], moe_w3[i], moe_w2[i])
            out_p, out_s = _combine(pos, x1, route, mod_l, ln2_g[l], ln2_b[l], y)
            x = jnp.concatenate([out_p, out_s], axis=0) if l + 1 < DEPTH else None
        else:
            x1, h2 = outs
            y = _ffn(h2, *dense_meta, ffn_w1[i][None], ffn_w3[i][None], ffn_w2[i][None])
            x = _ln2(x1, y, mod_l, ln2_g[l], ln2_b[l])

    if x is not None:
        out_p, out_s = x[:TP], x[TP:]
    return out_p.reshape(NB_P, L_P, D), out_s.reshape(NB_S, L_S, D), jnp.stack(states, axis=1)
```

```python
import functools

import jax
import jax.numpy as jnp
import numpy as np
from jax import lax
from jax.experimental import pallas as pl
from jax.experimental.pallas import tpu as pltpu

F32 = jnp.float32
BF16 = jnp.bfloat16

D = 2048
NB_P, L_P = 16, 256
NB_S, L_S = 4, 1024
TP = NB_P * L_P
T = TP + NB_S * L_S
DEPTH = 2
GRID_W = 64
GW = D // 2
CW = D - GW
H = 4
DV = GW // H
DK = DV // 2
QC = H * DK
RANK = 16
TAU = 16.0
CHUNK = 64
CONV_K = 31
FF = 7 * D // 2
NE = 8
ALPHA = (2 * DEPTH) ** 0.25
LN_EPS = 1e-5
RMS_EPS = 1e-6
MAIN_COLS = 2 * QC + 2 * GW

TILE = 256
NTILES = T // TILE
NTILES_P = TP // TILE
TILES_PER_S = L_S // TILE
HALO = 16

VMEM_LIMIT = 56 * 1024 * 1024

SH1, SC1, G1, SH2, SC2, G2 = range(6)


def _params(n_axes, vmem=VMEM_LIMIT):
    return pltpu.CompilerParams(dimension_semantics=("arbitrary",) * n_axes,
                                vmem_limit_bytes=vmem)


def _mod_row(tok0):
    return jnp.where(tok0 < TP, 0, 1 + (tok0 - TP) // L_S)


def _sigmoid(x):
    return 1.0 / (1.0 + jnp.exp(-x))


def _silu(x):
    return x * _sigmoid(x)


def _layer_norm(r, g, b):
    mu = jnp.mean(r, -1, keepdims=True)
    rc = r - mu
    var = jnp.mean(rc * rc, -1, keepdims=True)
    return rc * lax.rsqrt(var + LN_EPS) * g + b


def _assemble_kernel(xp_ref, xs_ref, pos_ref, o_ref):
    i = pl.program_id(0)

    @pl.when(i < NTILES_P)
    def _():
        o_ref[...] = xp_ref[...]

    @pl.when(i >= NTILES_P)
    def _():
        o_ref[...] = xs_ref[...] + pos_ref[...]


def _assemble(xp, xs, pos):
    return pl.pallas_call(
        _assemble_kernel,
        out_shape=jax.ShapeDtypeStruct((T, D), F32),
        grid=(NTILES,),
        in_specs=[
            pl.BlockSpec((TILE, D), lambda i: (jnp.minimum(i, NTILES_P - 1), 0)),
            pl.BlockSpec((TILE, D), lambda i: (jnp.maximum(i - NTILES_P, 0), 0)),
            pl.BlockSpec((TILE, D), lambda i: (jnp.maximum(i - NTILES_P, 0) % TILES_PER_S, 0)),
        ],
        out_specs=pl.BlockSpec((TILE, D), lambda i: (i, 0)),
        compiler_params=_params(1),
        name="assemble",
    )(xp, xs, pos)


MOD_TN = 1024


def _mod_kernel(c_ref, w_ref, b_ref, o_ref):
    s = _silu(c_ref[...])
    o_ref[0] = jnp.dot(s, w_ref[0], preferred_element_type=F32,
                       precision=lax.Precision.HIGHEST) + b_ref[0]


def _modulation(cond8, w_mod, b_mod):
    out = pl.pallas_call(
        _mod_kernel,
        out_shape=jax.ShapeDtypeStruct((DEPTH, 8, 6 * D), F32),
        grid=(DEPTH, 6 * D // MOD_TN),
        in_specs=[
            pl.BlockSpec((8, D), lambda l, j: (0, 0)),
            pl.BlockSpec((1, D, MOD_TN), lambda l, j: (l, 0, j)),
            pl.BlockSpec((1, 1, MOD_TN), lambda l, j: (l, 0, j)),
        ],
        out_specs=pl.BlockSpec((1, 8, MOD_TN), lambda l, j: (l, 0, j)),
        compiler_params=_params(2),
        name="modulation",
    )(cond8, w_mod, b_mod.reshape(DEPTH, 1, 6 * D))
    return out.reshape(DEPTH, 8, 6, D)


PROJ_TM = 1024
PROJ_TN = 512
PROJ_NMAIN = MAIN_COLS // PROJ_TN
GLU_TN = 256
PROJ_NGLU = CW // GLU_TN


def _modulated(x_ref, mod_ref, shift, scale):
    m = mod_ref[0]
    return x_ref[...] * (1.0 + m[scale:scale + 1, :]) + m[shift:shift + 1, :]


def _inproj_kernel(x_ref, mod_ref, w_ref, wa_ref, wua_ref, wug_ref, z_ref, a_ref, u_ref, h_scr):
    j = pl.program_id(1)

    @pl.when(j == 0)
    def _():
        hb = _modulated(x_ref, mod_ref, SH1, SC1).astype(BF16)
        h_scr[...] = hb
        a_ref[...] = jnp.dot(hb, wa_ref[...], preferred_element_type=F32)

    @pl.when(j < PROJ_NMAIN)
    def _():
        z_ref[...] = jnp.dot(h_scr[...], w_ref[...].astype(BF16), preferred_element_type=F32)

    @pl.when(j >= PROJ_NMAIN)
    def _():
        h = h_scr[...]
        a = jnp.dot(h, wua_ref[...].astype(BF16), preferred_element_type=F32)
        g = jnp.dot(h, wug_ref[...].astype(BF16), preferred_element_type=F32)
        u_ref[...] = a * _sigmoid(g)


def _inproj(x, mod_l, w_main, w_a, w_u):
    main_j = lambda j: jnp.minimum(j, PROJ_NMAIN - 1)
    glu_j = lambda j: jnp.maximum(j - PROJ_NMAIN, 0)
    return pl.pallas_call(
        _inproj_kernel,
        out_shape=(jax.ShapeDtypeStruct((T, MAIN_COLS), F32),
                   jax.ShapeDtypeStruct((T, 128), F32),
                   jax.ShapeDtypeStruct((T, CW), F32)),
        grid=(T // PROJ_TM, PROJ_NMAIN + PROJ_NGLU),
        in_specs=[
            pl.BlockSpec((PROJ_TM, D), lambda i, j: (i, 0)),
            pl.BlockSpec((1, 6, D), lambda i, j: (_mod_row(i * PROJ_TM), 0, 0)),
            pl.BlockSpec((D, PROJ_TN), lambda i, j: (0, main_j(j))),
            pl.BlockSpec((D, 128), lambda i, j: (0, 0)),
            pl.BlockSpec((D, GLU_TN), lambda i, j: (0, glu_j(j))),
            pl.BlockSpec((D, GLU_TN), lambda i, j: (0, glu_j(j) + PROJ_NGLU)),
        ],
        out_specs=(pl.BlockSpec((PROJ_TM, PROJ_TN), lambda i, j: (i, main_j(j))),
                   pl.BlockSpec((PROJ_TM, 128), lambda i, j: (i, 0)),
                   pl.BlockSpec((PROJ_TM, GLU_TN), lambda i, j: (i, glu_j(j)))),
        scratch_shapes=[pltpu.VMEM((PROJ_TM, D), BF16)],
        compiler_params=_params(2),
        name="inproj",
    )(x, mod_l, w_main, w_a, w_u, w_u)


CONV_RC = 32
CONV_CC = 256


def _conv_kernel(uc_ref, up_ref, un_ref, cw_ref, cb_ref, lg_ref, lb_ref, wp_ref, bp_ref,
                 o_ref, pad_scr, conv_scr):
    i = pl.program_id(0)
    s = jnp.maximum(i - NTILES_P, 0) % TILES_PER_S
    has_prev = jnp.logical_and(i >= NTILES_P, s != 0)
    has_next = jnp.logical_and(i >= NTILES_P, s != TILES_PER_S - 1)
    pad_scr[0:HALO, :] = jnp.where(has_prev, up_ref[...], 0.0)
    pad_scr[HALO:HALO + TILE, :] = uc_ref[...]
    pad_scr[HALO + TILE:HALO + TILE + HALO, :] = jnp.where(has_next, un_ref[...], 0.0)

    off = HALO - CONV_K // 2
    for c in range(CW // CONV_CC):
        cs = slice(c * CONV_CC, (c + 1) * CONV_CC)

        def body(r, carry, cs=cs):
            r0 = pl.multiple_of(r * CONV_RC, CONV_RC)
            win = pad_scr[pl.ds(r0, 2 * CONV_RC), cs]
            acc = None
            for b in range(8):
                part = None
                for a in range((CONV_K + off) // 8 + 1):
                    k = 8 * a + b - off
                    if 0 <= k < CONV_K:
                        term = cw_ref[k:k + 1, cs] * win[8 * a:8 * a + CONV_RC + 8, :]
                        part = term if part is None else part + term
                part = part[b:b + CONV_RC, :]
                acc = part if acc is None else acc + part
            conv_scr[pl.ds(r0, CONV_RC), cs] = acc
            return carry

        lax.fori_loop(0, TILE // CONV_RC, body, 0)

    v = conv_scr[...] + cb_ref[...]
    y = _silu(_layer_norm(v, lg_ref[...], lb_ref[...]))
    o_ref[...] = (jnp.dot(y.astype(BF16), wp_ref[...], preferred_element_type=F32) + bp_ref[...]).astype(BF16)


def _conv_module(u, conv_w, conv_b, ln_g, ln_b, w_pw2, b_pw2):
    hb = TILE // HALO
    row = lambda a: a.reshape(1, CW)
    return pl.pallas_call(
        _conv_kernel,
        out_shape=jax.ShapeDtypeStruct((T, CW), BF16),
        grid=(NTILES,),
        in_specs=[
            pl.BlockSpec((TILE, CW), lambda i: (i, 0)),
            pl.BlockSpec((HALO, CW), lambda i: (jnp.maximum(i * hb - 1, 0), 0)),
            pl.BlockSpec((HALO, CW), lambda i: (jnp.minimum((i + 1) * hb, T // HALO - 1), 0)),
            pl.BlockSpec((CONV_K, CW), lambda i: (0, 0)),
            pl.BlockSpec((1, CW), lambda i: (0, 0)),
            pl.BlockSpec((1, CW), lambda i: (0, 0)),
            pl.BlockSpec((1, CW), lambda i: (0, 0)),
            pl.BlockSpec((CW, CW), lambda i: (0, 0)),
            pl.BlockSpec((1, CW), lambda i: (0, 0)),
        ],
        out_specs=pl.BlockSpec((TILE, CW), lambda i: (i, 0)),
        scratch_shapes=[pltpu.VMEM((TILE + 2 * HALO, CW), F32), pltpu.VMEM((TILE, CW), F32)],
        compiler_params=_params(1),
        name="conv_module",
    )(u, u, u, conv_w, row(conv_b), row(ln_g), row(ln_b), w_pw2, row(b_pw2))


def _split3(x):
    hi = x.astype(BF16)
    r1 = x - hi.astype(F32)
    mid = r1.astype(BF16)
    lo = (r1 - mid.astype(F32)).astype(BF16)
    return hi, mid, lo


def _log_sigmoid(x):
    return jnp.minimum(x, 0.0) - jnp.log(1.0 + jnp.exp(-jnp.abs(x)))


def _dot_nt(a, b):
    return lax.dot_general(a, b, (((1,), (1,)), ((), ())), preferred_element_type=F32)


def _dot_tn(a, b):
    return lax.dot_general(a, b, (((0,), (0,)), ((), ())), preferred_element_type=F32)


def _gla_kernel(q_ref, k_ref, v_ref, a_ref, wa2_ref, ba2_ref, tri_ref, s0_ref,
                o_ref, snew_ref, st_scr, *, reverse):
    n = pl.program_id(0)
    tile = (NTILES - 1 - n) if reverse else n
    is_prompt = tile < NTILES_P
    spos = jnp.maximum(tile - NTILES_P, 0) % TILES_PER_S
    seq_first = spos == (TILES_PER_S - 1 if reverse else 0)

    @pl.when(is_prompt)
    def _():
        st_scr[...] = jnp.zeros_like(st_scr)

    @pl.when(jnp.logical_and(jnp.logical_not(is_prompt), seq_first))
    def _():
        for h in range(H):
            st_scr[h] = s0_ref[0, 0, h].T

    tri = tri_ref[...]
    mask = tri > 0
    a_lr = a_ref[:, 0:2 * RANK].astype(BF16)
    n_chunks = TILE // CHUNK
    order = range(n_chunks - 1, -1, -1) if reverse else range(n_chunks)
    logits = jnp.dot(a_lr, wa2_ref[...], preferred_element_type=F32) + ba2_ref[...]
    hi, mid, lo = _split3(_log_sigmoid(logits) / TAU)
    b_all = (jnp.dot(tri, hi, preferred_element_type=F32)
             + jnp.dot(tri, mid, preferred_element_type=F32)
             + jnp.dot(tri, lo, preferred_element_type=F32))
    for h in range(H):
        ks = slice(h * DK, (h + 1) * DK)
        b = b_all[:, ks]
        q = q_ref[:, ks] * (DK ** -0.5)
        k = k_ref[:, ks]
        vb = v_ref[:, h * DV:(h + 1) * DV].astype(BF16)
        qd = (q * jnp.exp(b)).astype(BF16)
        kd = (k * jnp.exp(-b)).astype(BF16)
        att = jnp.where(mask, _dot_nt(qd, kd), 0.0).astype(BF16)
        o_intra = jnp.dot(att, vb, preferred_element_type=F32)
        st = st_scr[h]
        for c in order:
            rows = slice(c * CHUNK, (c + 1) * CHUNK)
            last = c * CHUNK if reverse else (c + 1) * CHUNK - 1
            b_last = b[last:last + 1, :]
            kl = (k[rows] * jnp.exp(b_last - b[rows])).astype(BF16)
            o_ref[rows, h * DV:(h + 1) * DV] = o_intra[rows] + _dot_nt(qd[rows], st.astype(BF16))
            st = jnp.exp(b_last) * st + _dot_tn(vb[rows], kl)
        st_scr[h] = st

    @pl.when(is_prompt)
    def _():
        for h in range(H):
            snew_ref[0, h] = st_scr[h].T


def _gla(z, a_lr, wa2_dir, ba2_dir, tri_dir, s0, direction):
    reverse = direction == 1
    tile = (lambda n: NTILES - 1 - n) if reverse else (lambda n: n)
    req = lambda n: jnp.clip((tile(n) - NTILES_P) // TILES_PER_S, 0, NB_S - 1)
    return pl.pallas_call(
        functools.partial(_gla_kernel, reverse=reverse),
        out_shape=(jax.ShapeDtypeStruct((T, GW), F32),
                   jax.ShapeDtypeStruct((NB_P, H, DK, DV), F32)),
        grid=(NTILES,),
        in_specs=[
            pl.BlockSpec((TILE, QC), lambda n: (tile(n), 0)),
            pl.BlockSpec((TILE, QC), lambda n: (tile(n), 1)),
            pl.BlockSpec((TILE, GW), lambda n: (tile(n), 1)),
            pl.BlockSpec((TILE, 128), lambda n: (tile(n), 0)),
            pl.BlockSpec((2 * RANK, QC), lambda n: (0, 0)),
            pl.BlockSpec((1, QC), lambda n: (0, 0)),
            pl.BlockSpec((TILE, TILE), lambda n: (0, 0)),
            pl.BlockSpec((1, 1, H, DK, DV), lambda n: (req(n), direction, 0, 0, 0)),
        ],
        out_specs=(pl.BlockSpec((TILE, GW), lambda n: (tile(n), 0)),
                   pl.BlockSpec((1, H, DK, DV), lambda n: (jnp.minimum(tile(n), NTILES_P - 1), 0, 0, 0))),
        scratch_shapes=[pltpu.VMEM((H, DV, DK), F32)],
        compiler_params=_params(1),
        name="gla_bwd" if reverse else "gla_fwd",
    )(z, z, z, a_lr, wa2_dir, ba2_dir, tri_dir, s0)


def _mixout_kernel(of_ref, ob_ref, g_ref, u_ref, x_ref, mod_ref, gng_ref, wo_ref, l1g_ref, l1b_ref,
                   *rest, with_router):
    if with_router:
        wr_ref, br_ref, ltri_ref, x1_ref, h2_ref, route_ref, cnt_ref, run_scr = rest
    else:
        x1_ref, h2_ref = rest
    o = of_ref[...] + ob_ref[...]
    parts = []
    for h in range(H):
        oh = o[:, h * DV:(h + 1) * DV]
        ms = jnp.mean(oh * oh, -1, keepdims=True)
        parts.append(oh * lax.rsqrt(ms + RMS_EPS) * gng_ref[...])
    on = jnp.concatenate(parts, axis=-1) * _silu(g_ref[...])
    y = (jnp.dot(on.astype(BF16), wo_ref[0:GW, :], preferred_element_type=F32)
         + jnp.dot(u_ref[...], wo_ref[GW:D, :], preferred_element_type=F32))
    m = mod_ref[0]
    x1 = _layer_norm(ALPHA * x_ref[...] + m[G1:G1 + 1, :] * y, l1g_ref[...], l1b_ref[...])
    x1_ref[...] = x1
    h2 = x1 * (1.0 + m[SC2:SC2 + 1, :]) + m[SH2:SH2 + 1, :]
    h2_ref[...] = h2.astype(h2_ref.dtype)
    if with_router:
        h_hi = h2.astype(BF16)
        h_lo = (h2 - h_hi.astype(F32)).astype(BF16)
        p_hi = jnp.dot(h_hi, wr_ref[...], preferred_element_type=F32)
        p_lo = jnp.dot(h_lo, wr_ref[...], preferred_element_type=F32)
        logits = p_hi + pltpu.roll(p_hi, 128 - NE, 1) + p_lo + br_ref[...]
        lane = lax.broadcasted_iota(jnp.int32, logits.shape, 1)
        lg = jnp.where(lane < NE, logits, -jnp.inf)
        v1 = jnp.max(lg, -1, keepdims=True)
        i1 = jnp.min(jnp.where(lg == v1, lane, 128), -1, keepdims=True)
        lg2 = jnp.where(lane == i1, -jnp.inf, lg)
        v2 = jnp.max(lg2, -1, keepdims=True)
        i2 = jnp.min(jnp.where(lg2 == v2, lane, 128), -1, keepdims=True)
        e2 = jnp.exp(v2 - v1)
        w1 = 1.0 / (1.0 + e2)
        w2 = e2 / (1.0 + e2)
        @pl.when(pl.program_id(0) == 0)
        def _():
            run_scr[...] = jnp.zeros_like(run_scr)

        hit1 = lane == i1
        hit2 = lane == i2
        one1 = jnp.where(hit1, 1.0, 0.0)
        one2 = jnp.where(hit2, 1.0, 0.0)
        before1 = jnp.dot(ltri_ref[...], one1.astype(BF16), preferred_element_type=F32)
        before2 = jnp.dot(ltri_ref[...], one2.astype(BF16), preferred_element_type=F32)
        tot1 = jnp.sum(one1, axis=0, keepdims=True)
        tot2 = jnp.sum(one2, axis=0, keepdims=True)
        run = run_scr[...]
        rank1 = jnp.sum(jnp.where(hit1, before1 + run, 0.0), -1, keepdims=True)
        rank2 = jnp.sum(jnp.where(hit2, before2 + (run + tot1), 0.0), -1, keepdims=True)
        run = run + tot1 + tot2
        run_scr[...] = run
        cnt_ref[...] = jnp.broadcast_to(run, cnt_ref.shape)
        route_ref[...] = jnp.where(lane == 0, i1.astype(F32),
                         jnp.where(lane == 1, i2.astype(F32),
                         jnp.where(lane == 2, w1,
                         jnp.where(lane == 3, w2,
                         jnp.where(lane == 4, rank1, jnp.where(lane == 5, rank2, 0.0))))))


def _mixout(o_f, o_b, z, u2, x, mod_l, gng, w_out, ln_g, ln_b, router=None):
    row = lambda a: a.reshape(1, -1)
    full = lambda shape: pl.BlockSpec(shape, lambda i: (0,) * len(shape))
    in_specs = [
        pl.BlockSpec((TILE, GW), lambda i: (i, 0)),
        pl.BlockSpec((TILE, GW), lambda i: (i, 0)),
        pl.BlockSpec((TILE, GW), lambda i: (i, 2)),
        pl.BlockSpec((TILE, CW), lambda i: (i, 0)),
        pl.BlockSpec((TILE, D), lambda i: (i, 0)),
        pl.BlockSpec((1, 6, D), lambda i: (_mod_row(i * TILE), 0, 0)),
        full((1, DV)), full((D, D)), full((1, D)), full((1, D)),
    ]
    args = [o_f, o_b, z, u2, x, mod_l, row(gng), w_out, row(ln_g), row(ln_b)]
    out_shape = [jax.ShapeDtypeStruct((T, D), F32)]
    out_specs = [pl.BlockSpec((TILE, D), lambda i: (i, 0)), pl.BlockSpec((TILE, D), lambda i: (i, 0))]
    scratch = []
    if router is None:
        out_shape.append(jax.ShapeDtypeStruct((T, D), BF16))
    else:
        w_r, b_r = router
        idx = jnp.arange(TILE)
        ltri = (idx[None, :] < idx[:, None]).astype(BF16)
        in_specs += [full((D, 128)), full((1, 128)), full((TILE, TILE))]
        args += [w_r, b_r, ltri]
        out_shape += [jax.ShapeDtypeStruct((T, D), F32),
                      jax.ShapeDtypeStruct((T, 128), F32), jax.ShapeDtypeStruct((8, 128), F32)]
        out_specs += [pl.BlockSpec((TILE, 128), lambda i: (i, 0)), pl.BlockSpec((8, 128), lambda i: (0, 0))]
        scratch = [pltpu.VMEM((1, 128), F32)]
    return pl.pallas_call(
        functools.partial(_mixout_kernel, with_router=router is not None),
        out_shape=tuple(out_shape),
        grid=(NTILES,),
        in_specs=in_specs,
        out_specs=tuple(out_specs),
        scratch_shapes=scratch,
        compiler_params=_params(1),
        name="mixout",
    )(*args)


FFN_R = 1024
FFN_SUB = 256
FFN_TF = 512


def _ffn_kernel(te_ref, tn_ref, tb_ref, h_ref, w1_ref, w3_ref, w2_ref, o_ref, g_scr, a_scr, *h_scr, wide_rows):
    s = pl.program_id(0)
    j = pl.program_id(1)
    n = tn_ref[s]
    n_sub = (n + FFN_SUB - 1) // FFN_SUB

    @pl.when(jnp.logical_and(j == 0, n > 0))
    def _():
        o_ref[...] = jnp.zeros_like(o_ref)

    rows_ref = h_scr[0] if wide_rows else h_ref
    for k in range(1, FFN_R // FFN_SUB + 1):
        m = k * FFN_SUB

        if wide_rows:
            @pl.when(jnp.logical_and(n_sub == k, j == 0))
            def _(m=m):
                rows_ref[0:m, :] = h_ref[0:m, :].astype(BF16)

        @pl.when(n_sub == k)
        def _(m=m):
            hc = rows_ref[0:m, :]
            g_scr[0:m, :] = jnp.dot(hc, w1_ref[0].astype(BF16), preferred_element_type=F32)
            u = jnp.dot(hc, w3_ref[0].astype(BF16), preferred_element_type=F32)
            a_scr[0:m, :] = (_silu(g_scr[0:m, :]) * u).astype(BF16)
            o_ref[0:m, :] += jnp.dot(a_scr[0:m, :], w2_ref[0].astype(BF16), preferred_element_type=F32)


def _ffn(hs, tile_expert, tile_rows, tile_block, w1, w3, w2):
    n_tiles = tile_expert.shape[0]
    nj = FF // FFN_TF

    def jj(s, j, tn):
        return jnp.where(tn[s] > 0, j, nj - 1)

    once = pl.Buffered(1)
    wide_rows = hs.dtype == F32
    scratch = [pltpu.VMEM((FFN_R, FFN_TF), F32), pltpu.VMEM((FFN_R, FFN_TF), BF16)]
    if wide_rows:
        scratch.append(pltpu.VMEM((FFN_R, D), BF16))
    return pl.pallas_call(
        functools.partial(_ffn_kernel, wide_rows=wide_rows),
        out_shape=jax.ShapeDtypeStruct((hs.shape[0], D), F32),
        grid_spec=pltpu.PrefetchScalarGridSpec(
            num_scalar_prefetch=3,
            grid=(n_tiles, nj),
            in_specs=[
                pl.BlockSpec((FFN_R, hs.shape[1]), lambda s, j, te, tn, tb: (tb[s], 0), pipeline_mode=once),
                pl.BlockSpec((1, D, FFN_TF), lambda s, j, te, tn, tb: (te[s], 0, jj(s, j, tn))),
                pl.BlockSpec((1, D, FFN_TF), lambda s, j, te, tn, tb: (te[s], 0, jj(s, j, tn))),
                pl.BlockSpec((1, FFN_TF, D), lambda s, j, te, tn, tb: (te[s], jj(s, j, tn), 0)),
            ],
            out_specs=pl.BlockSpec((FFN_R, D), lambda s, j, te, tn, tb: (tb[s], 0), pipeline_mode=once),
            scratch_shapes=scratch,
        ),
        compiler_params=_params(2),
        name="ffn",
    )(tile_expert, tile_rows, tile_block, hs, w1, w3, w2)


DMA_UNROLL = 8


def _dispatch_kernel(pos_ref, pad_ref, h_ref, o_hbm, zero_scr, sem, zsem):
    i = pl.program_id(0)

    @pl.when(i == 0)
    def _():
        zero_scr[...] = jnp.zeros_like(zero_scr)
        for e in range(NE):
            first, count = pad_ref[e], pad_ref[NE + e]

            def zero_copy(r, first=first):
                return pltpu.make_async_copy(zero_scr, o_hbm.at[pl.ds(first + r, 1), :], zsem)

            def start(r, carry):
                zero_copy(r).start()
                return carry

            def wait(r, carry):
                zero_copy(r).wait()
                return carry

            lax.fori_loop(0, count, start, 0)
            lax.fori_loop(0, count, wait, 0)

    def issue(r, carry):
        for slot in range(2):
            p = pos_ref[2 * (i * TILE + r) + slot]
            pltpu.make_async_copy(h_ref.at[pl.ds(r, 1), :], o_hbm.at[pl.ds(p, 1), :], sem).start()
        return carry

    lax.fori_loop(0, TILE, issue, 0, unroll=DMA_UNROLL)
    for slot in range(2):
        pltpu.make_async_copy(h_ref, o_hbm.at[pl.ds(0, TILE), :], sem).wait()


def _dispatch(pos, pad, h2):
    return pl.pallas_call(
        _dispatch_kernel,
        out_shape=jax.ShapeDtypeStruct((MOE_TILES * FFN_R, D), F32),
        grid_spec=pltpu.PrefetchScalarGridSpec(
            num_scalar_prefetch=2,
            grid=(NTILES,),
            in_specs=[pl.BlockSpec((TILE, D), lambda i, pos, pad: (i, 0))],
            out_specs=pl.BlockSpec(memory_space=pl.ANY),
            scratch_shapes=[pltpu.VMEM((1, D), F32), pltpu.SemaphoreType.DMA, pltpu.SemaphoreType.DMA],
        ),
        compiler_params=_params(1),
        name="dispatch",
    )(pos, pad, h2)


def _ln2_kernel(x_ref, f_ref, mod_ref, g_ref, b_ref, o_ref):
    m = mod_ref[0]
    o_ref[...] = _layer_norm(ALPHA * x_ref[...] + m[G2:G2 + 1, :] * f_ref[...], g_ref[...], b_ref[...])


def _ln2(x1, f, mod_l, ln_g, ln_b):
    return pl.pallas_call(
        _ln2_kernel,
        out_shape=jax.ShapeDtypeStruct((T, D), F32),
        grid=(NTILES,),
        in_specs=[
            pl.BlockSpec((TILE, D), lambda i: (i, 0)),
            pl.BlockSpec((TILE, D), lambda i: (i, 0)),
            pl.BlockSpec((1, 6, D), lambda i: (_mod_row(i * TILE), 0, 0)),
            pl.BlockSpec((1, D), lambda i: (0, 0)),
            pl.BlockSpec((1, D), lambda i: (0, 0)),
        ],
        out_specs=pl.BlockSpec((TILE, D), lambda i: (i, 0)),
        compiler_params=_params(1),
        name="ln2",
    )(x1, f, mod_l, ln_g.reshape(1, D), ln_b.reshape(1, D))


def _combine_kernel(pos_ref, x_ref, route_ref, mod_ref, g_ref, b_ref, y_hbm, op_ref, os_ref, buf, sem):
    i = pl.program_id(0)

    def gather(tile):
        def issue(r, carry):
            for slot in range(2):
                p = pos_ref[2 * (tile * TILE + r) + slot]
                pltpu.make_async_copy(y_hbm.at[pl.ds(p, 1), :], buf.at[tile % 2, slot, pl.ds(r, 1), :],
                                      sem.at[tile % 2]).start()
            return carry

        lax.fori_loop(0, TILE, issue, 0, unroll=DMA_UNROLL)

    @pl.when(i == 0)
    def _():
        gather(i)

    @pl.when(i + 1 < NTILES)
    def _():
        gather(i + 1)

    for slot in range(2):
        pltpu.make_async_copy(y_hbm.at[pl.ds(0, TILE), :], buf.at[i % 2, slot], sem.at[i % 2]).wait()
    route = route_ref[...]
    f = route[:, 2:3] * buf[i % 2, 0] + route[:, 3:4] * buf[i % 2, 1]
    m = mod_ref[0]
    out = _layer_norm(ALPHA * x_ref[...] + m[G2:G2 + 1, :] * f, g_ref[...], b_ref[...])

    @pl.when(i < NTILES_P)
    def _():
        op_ref[...] = out

    @pl.when(i >= NTILES_P)
    def _():
        os_ref[...] = out


def _combine(pos, x1, route, mod_l, ln_g, ln_b, y):
    return pl.pallas_call(
        _combine_kernel,
        out_shape=(jax.ShapeDtypeStruct((TP, D), F32), jax.ShapeDtypeStruct((T - TP, D), F32)),
        grid_spec=pltpu.PrefetchScalarGridSpec(
            num_scalar_prefetch=1,
            grid=(NTILES,),
            in_specs=[
                pl.BlockSpec((TILE, D), lambda i, pos: (i, 0)),
                pl.BlockSpec((TILE, 128), lambda i, pos: (i, 0)),
                pl.BlockSpec((1, 6, D), lambda i, pos: (_mod_row(i * TILE), 0, 0)),
                pl.BlockSpec((1, D), lambda i, pos: (0, 0)),
                pl.BlockSpec((1, D), lambda i, pos: (0, 0)),
                pl.BlockSpec(memory_space=pl.ANY),
            ],
            out_specs=(pl.BlockSpec((TILE, D), lambda i, pos: (jnp.minimum(i, NTILES_P - 1), 0)),
                       pl.BlockSpec((TILE, D), lambda i, pos: (jnp.maximum(i - NTILES_P, 0), 0))),
            scratch_shapes=[pltpu.VMEM((2, 2, TILE, D), F32), pltpu.SemaphoreType.DMA((2,))],
        ),
        compiler_params=_params(1),
        name="combine",
    )(pos, x1, route, mod_l, ln_g.reshape(1, D), ln_b.reshape(1, D), y)


MOE_TILES = 2 * T // FFN_R + NE


def _routing_tables(route, counts_f):
    eidx = route[:, 0:2].astype(jnp.int32)
    rank = route[:, 4:6].astype(jnp.int32)
    counts = counts_f[0, :NE].astype(jnp.int32)
    ntiles = (counts + FFN_R - 1) // FFN_R
    tend = jnp.cumsum(ntiles)
    tstart = tend - ntiles
    experts = jnp.arange(NE, dtype=jnp.int32)
    first_row = jnp.sum(jnp.where(eidx[..., None] == experts, tstart * FFN_R, 0), axis=-1)
    pos = (first_row + rank).reshape(-1)
    tiles = jnp.arange(MOE_TILES, dtype=jnp.int32)
    total = tend[-1]
    t_eff = jnp.minimum(tiles, total - 1)
    tile_expert = jnp.minimum(jnp.sum((t_eff[:, None] >= tend[None, :]).astype(jnp.int32), axis=1), NE - 1)
    tile_rows = jnp.clip(counts[tile_expert] - (t_eff - tstart[tile_expert]) * FFN_R, 0, FFN_R)
    tile_rows = jnp.where(tiles < total, tile_rows, 0)
    pad = jnp.concatenate([tstart * FFN_R + counts, (-counts) % FFN_SUB]).astype(jnp.int32)
    return pos, pad, tile_expert.astype(jnp.int32), tile_rows.astype(jnp.int32), t_eff


def _grid_pos_emb():
    rows = L_S // GRID_W
    r = np.repeat(np.arange(rows, dtype=np.float64), GRID_W)
    col = np.tile(np.arange(GRID_W, dtype=np.float64), rows)
    quarter = D // 4
    freqs = 1.0 / (10000.0 ** (np.arange(quarter, dtype=np.float64) / quarter))

    def enc(p):
        ang = p[:, None] * freqs[None, :]
        return np.concatenate([np.sin(ang), np.cos(ang)], -1)

    return jnp.asarray(np.concatenate([enc(r), enc(col)], -1), dtype=F32)


def _cumsum_matrices():
    i = jnp.arange(TILE)
    same = (i[:, None] // CHUNK) == (i[None, :] // CHUNK)
    fwd = jnp.logical_and(same, i[None, :] <= i[:, None])
    bwd = jnp.logical_and(same, i[None, :] >= i[:, None])
    return fwd.astype(BF16), bwd.astype(BF16)


def kernel(x_prompt, x_sample, state_gla, c, c_ctx, w_mod, b_mod, w_in, w_a2, b_a2, gla_norm_g, conv_w, conv_b, conv_ln_g, conv_ln_b, w_pw2, b_pw2, w_out, ln1_g, ln1_b, ln2_g, ln2_b, ffn_w1, ffn_w3, ffn_w2, moe_w_router, moe_b_router, moe_w1, moe_w3, moe_w2):
    x = _assemble(x_prompt.reshape(TP, D), x_sample.reshape(T - TP, D), _grid_pos_emb())
    cond8 = jnp.zeros((8, D), F32).at[0].set(c_ctx).at[1:1 + NB_S].set(c)
    mod = _modulation(cond8, w_mod, b_mod)
    tri = _cumsum_matrices()
    dense_tiles = T // FFN_R
    dense_meta = (jnp.zeros((dense_tiles,), jnp.int32), jnp.full((dense_tiles,), FFN_R, jnp.int32),
                  jnp.arange(dense_tiles, dtype=jnp.int32))

    states = []
    for l in range(DEPTH):
        mod_l = mod[l]
        w_a = jnp.pad(w_in[l, :, MAIN_COLS:MAIN_COLS + 2 * RANK], ((0, 0), (0, 128 - 2 * RANK))).astype(BF16)
        z, a_lr, u = _inproj(x, mod_l, w_in[l, :, :MAIN_COLS], w_a, w_in[l, :, MAIN_COLS + 2 * RANK:])
        u2 = _conv_module(u, conv_w[l], conv_b[l], conv_ln_g[l], conv_ln_b[l],
                          w_pw2[l].astype(BF16), b_pw2[l])
        o_dir, s_dir = [], []
        for d in range(2):
            wa2 = jnp.zeros((2 * RANK, QC), F32).at[d * RANK:(d + 1) * RANK].set(w_a2[l, d]).astype(BF16)
            o, s_new = _gla(z, a_lr, wa2, b_a2[l, d].reshape(1, QC), tri[d], state_gla[:, l], d)
            o_dir.append(o)
            s_dir.append(s_new)
        states.append(jnp.stack(s_dir, axis=1))
        is_moe = l % 2 == 1
        i = l // 2
        router = None
        if is_moe:
            wr_hi = moe_w_router[i].astype(BF16)
            wr_lo = (moe_w_router[i] - wr_hi.astype(F32)).astype(BF16)
            router = (jnp.pad(jnp.concatenate([wr_hi, wr_lo], axis=1), ((0, 0), (0, 128 - 2 * NE))),
                      jnp.pad(moe_b_router[i], (0, 128 - NE)).reshape(1, 128))
        outs = _mixout(o_dir[0], o_dir[1], z, u2, x, mod_l, gla_norm_g[l], w_out[l].astype(BF16),
                       ln1_g[l], ln1_b[l], router)
        if is_moe:
            x1, h2, route, counts = outs
            pos, pad, t_exp, t_rows, t_blk = _routing_tables(route, counts)
            hs = _dispatch(pos, pad, h2)
            y = _ffn(hs, t_exp, t_rows, t_blk, moe_w1[i], moe_w3[i], moe_w2[i])
            out_p, out_s = _combine(pos, x1, route, mod_l, ln2_g[l], ln2_b[l], y)
            x = jnp.concatenate([out_p, out_s], axis=0) if l + 1 < DEPTH else None
        else:
            x1, h2 = outs
            y = _ffn(h2, *dense_meta, ffn_w1[i][None], ffn_w3[i][None], ffn_w2[i][None])
            x = _ln2(x1, y, mod_l, ln2_g[l], ln2_b[l])

    if x is not None:
        out_p, out_s = x[:TP], x[TP:]
    return out_p.reshape(NB_P, L_P, D), out_s.reshape(NB_S, L_S, D), jnp.stack(states, axis=1)
```

```python
import functools

import jax
import jax.numpy as jnp
import numpy as np
from jax import lax
from jax.experimental import pallas as pl
from jax.experimental.pallas import tpu as pltpu

F32 = jnp.float32
BF16 = jnp.bfloat16

D = 2048
NB_P, L_P = 16, 256
NB_S, L_S = 4, 1024
TP = NB_P * L_P
T = TP + NB_S * L_S
DEPTH = 2
GRID_W = 64
GW = D // 2
CW = D - GW
H = 4
DV = GW // H
DK = DV // 2
QC = H * DK
RANK = 16
TAU = 16.0
CHUNK = 64
CONV_K = 31
FF = 7 * D // 2
NE = 8
ALPHA = (2 * DEPTH) ** 0.25
LN_EPS = 1e-5
RMS_EPS = 1e-6
MAIN_COLS = 2 * QC + 2 * GW

TILE = 256
NTILES = T // TILE
NTILES_P = TP // TILE
TILES_PER_S = L_S // TILE
HALO = 16

VMEM_LIMIT = 56 * 1024 * 1024

SH1, SC1, G1, SH2, SC2, G2 = range(6)


def _params(n_axes, vmem=VMEM_LIMIT):
    return pltpu.CompilerParams(dimension_semantics=("arbitrary",) * n_axes,
                                vmem_limit_bytes=vmem)


def _mod_row(tok0):
    return jnp.where(tok0 < TP, 0, 1 + (tok0 - TP) // L_S)


def _sigmoid(x):
    return 1.0 / (1.0 + jnp.exp(-x))


def _silu(x):
    return x * _sigmoid(x)


def _layer_norm(r, g, b):
    mu = jnp.mean(r, -1, keepdims=True)
    rc = r - mu
    var = jnp.mean(rc * rc, -1, keepdims=True)
    return rc * lax.rsqrt(var + LN_EPS) * g + b


def _assemble_kernel(xp_ref, xs_ref, pos_ref, o_ref):
    i = pl.program_id(0)

    @pl.when(i < NTILES_P)
    def _():
        o_ref[...] = xp_ref[...]

    @pl.when(i >= NTILES_P)
    def _():
        o_ref[...] = xs_ref[...] + pos_ref[...]


def _assemble(xp, xs, pos):
    return pl.pallas_call(
        _assemble_kernel,
        out_shape=jax.ShapeDtypeStruct((T, D), F32),
        grid=(NTILES,),
        in_specs=[
            pl.BlockSpec((TILE, D), lambda i: (jnp.minimum(i, NTILES_P - 1), 0)),
            pl.BlockSpec((TILE, D), lambda i: (jnp.maximum(i - NTILES_P, 0), 0)),
            pl.BlockSpec((TILE, D), lambda i: (jnp.maximum(i - NTILES_P, 0) % TILES_PER_S, 0)),
        ],
        out_specs=pl.BlockSpec((TILE, D), lambda i: (i, 0)),
        compiler_params=_params(1),
        name="assemble",
    )(xp, xs, pos)


MOD_TN = 1024


def _mod_kernel(c_ref, w_ref, b_ref, o_ref):
    s = _silu(c_ref[...])
    s_hi = s.astype(BF16)
    s_lo = (s - s_hi.astype(F32)).astype(BF16)
    w = w_ref[0]
    w_hi = w.astype(BF16)
    w_lo = (w - w_hi.astype(F32)).astype(BF16)
    o_ref[0] = (jnp.dot(s_hi, w_hi, preferred_element_type=F32)
                + jnp.dot(s_hi, w_lo, preferred_element_type=F32)
                + jnp.dot(s_lo, w_hi, preferred_element_type=F32)) + b_ref[0]


def _modulation(cond8, w_mod, b_mod):
    out = pl.pallas_call(
        _mod_kernel,
        out_shape=jax.ShapeDtypeStruct((DEPTH, 8, 6 * D), F32),
        grid=(DEPTH, 6 * D // MOD_TN),
        in_specs=[
            pl.BlockSpec((8, D), lambda l, j: (0, 0)),
            pl.BlockSpec((1, D, MOD_TN), lambda l, j: (l, 0, j)),
            pl.BlockSpec((1, 1, MOD_TN), lambda l, j: (l, 0, j)),
        ],
        out_specs=pl.BlockSpec((1, 8, MOD_TN), lambda l, j: (l, 0, j)),
        compiler_params=_params(2),
        name="modulation",
    )(cond8, w_mod, b_mod.reshape(DEPTH, 1, 6 * D))
    return out.reshape(DEPTH, 8, 6, D)


PROJ_TM = 1024
PROJ_TN = 1024
PROJ_NMAIN = MAIN_COLS // PROJ_TN
GLU_TN = 512
PROJ_NGLU = CW // GLU_TN

IN_COLS = MAIN_COLS + 2 * RANK + 2 * CW
PREP_COLS = MAIN_COLS + 2 * CW + 128
PREP_ROWS = 256


def _wprep_kernel(w_ref, o_ref):
    conv0 = MAIN_COLS + 2 * RANK
    o_ref[0, :, 0:MAIN_COLS] = w_ref[0, :, 0:MAIN_COLS].astype(BF16)
    o_ref[0, :, MAIN_COLS:MAIN_COLS + 2 * CW] = w_ref[0, :, conv0:conv0 + 2 * CW].astype(BF16)
    a = w_ref[0, :, MAIN_COLS:MAIN_COLS + 128]
    lane = lax.broadcasted_iota(jnp.int32, a.shape, 1)
    o_ref[0, :, MAIN_COLS + 2 * CW:PREP_COLS] = jnp.where(lane < 2 * RANK, a, 0.0).astype(BF16)


def _wprep(w_in):
    return pl.pallas_call(
        _wprep_kernel,
        out_shape=jax.ShapeDtypeStruct((DEPTH, D, PREP_COLS), BF16),
        grid=(DEPTH, D // PREP_ROWS),
        in_specs=[pl.BlockSpec((1, PREP_ROWS, IN_COLS), lambda l, r: (l, r, 0))],
        out_specs=pl.BlockSpec((1, PREP_ROWS, PREP_COLS), lambda l, r: (l, r, 0)),
        compiler_params=_params(2),
        name="wprep",
    )(w_in)


def _modulated(x_ref, mod_ref, shift, scale):
    m = mod_ref[0]
    return x_ref[...] * (1.0 + m[scale:scale + 1, :]) + m[shift:shift + 1, :]


def _inproj_kernel(x_ref, mod_ref, w_ref, wa_ref, wua_ref, wug_ref, z_ref, a_ref, u_ref, h_scr):
    j = pl.program_id(1)

    @pl.when(j == 0)
    def _():
        hb = _modulated(x_ref, mod_ref, SH1, SC1).astype(BF16)
        h_scr[...] = hb
        a_ref[...] = jnp.dot(hb, wa_ref[0], preferred_element_type=F32)

    @pl.when(j < PROJ_NMAIN)
    def _():
        z_ref[...] = jnp.dot(h_scr[...], w_ref[0], preferred_element_type=F32)

    @pl.when(j >= PROJ_NMAIN)
    def _():
        h = h_scr[...]
        a = jnp.dot(h, wua_ref[0], preferred_element_type=F32)
        g = jnp.dot(h, wug_ref[0], preferred_element_type=F32)
        u_ref[...] = a * _sigmoid(g)


def _inproj(x, mod_l, w_prep, layer):
    main_j = lambda j: jnp.minimum(j, PROJ_NMAIN - 1)
    glu_j = lambda j: jnp.maximum(j - PROJ_NMAIN, 0)
    val0 = MAIN_COLS // GLU_TN
    gate0 = (MAIN_COLS + CW) // GLU_TN
    low_rank = (MAIN_COLS + 2 * CW) // 128
    return pl.pallas_call(
        _inproj_kernel,
        out_shape=(jax.ShapeDtypeStruct((T, MAIN_COLS), F32),
                   jax.ShapeDtypeStruct((T, 128), F32),
                   jax.ShapeDtypeStruct((T, CW), F32)),
        grid=(T // PROJ_TM, PROJ_NMAIN + PROJ_NGLU),
        in_specs=[
            pl.BlockSpec((PROJ_TM, D), lambda i, j: (i, 0)),
            pl.BlockSpec((1, 6, D), lambda i, j: (_mod_row(i * PROJ_TM), 0, 0)),
            pl.BlockSpec((1, D, PROJ_TN), lambda i, j: (layer, 0, main_j(j))),
            pl.BlockSpec((1, D, 128), lambda i, j: (layer, 0, low_rank)),
            pl.BlockSpec((1, D, GLU_TN), lambda i, j: (layer, 0, val0 + glu_j(j))),
            pl.BlockSpec((1, D, GLU_TN), lambda i, j: (layer, 0, gate0 + glu_j(j))),
        ],
        out_specs=(pl.BlockSpec((PROJ_TM, PROJ_TN), lambda i, j: (i, main_j(j))),
                   pl.BlockSpec((PROJ_TM, 128), lambda i, j: (i, 0)),
                   pl.BlockSpec((PROJ_TM, GLU_TN), lambda i, j: (i, glu_j(j)))),
        scratch_shapes=[pltpu.VMEM((PROJ_TM, D), BF16)],
        compiler_params=_params(2),
        name="inproj",
    )(x, mod_l, w_prep, w_prep, w_prep, w_prep)


CONV_RC = 32
CONV_CC = 256


def _conv_kernel(uc_ref, up_ref, un_ref, cw_ref, cb_ref, lg_ref, lb_ref, wp_ref, bp_ref,
                 o_ref, pad_scr, conv_scr):
    i = pl.program_id(0)
    s = jnp.maximum(i - NTILES_P, 0) % TILES_PER_S
    has_prev = jnp.logical_and(i >= NTILES_P, s != 0)
    has_next = jnp.logical_and(i >= NTILES_P, s != TILES_PER_S - 1)
    pad_scr[0:HALO, :] = jnp.where(has_prev, up_ref[...], 0.0)
    pad_scr[HALO:HALO + TILE, :] = uc_ref[...]
    pad_scr[HALO + TILE:HALO + TILE + HALO, :] = jnp.where(has_next, un_ref[...], 0.0)

    off = HALO - CONV_K // 2
    for c in range(CW // CONV_CC):
        cs = slice(c * CONV_CC, (c + 1) * CONV_CC)

        def body(r, carry, cs=cs):
            r0 = pl.multiple_of(r * CONV_RC, CONV_RC)
            win = pad_scr[pl.ds(r0, 2 * CONV_RC), cs]
            acc = None
            for b in range(8):
                part = None
                for a in range((CONV_K + off) // 8 + 1):
                    k = 8 * a + b - off
                    if 0 <= k < CONV_K:
                        term = cw_ref[k:k + 1, cs] * win[8 * a:8 * a + CONV_RC + 8, :]
                        part = term if part is None else part + term
                part = part[b:b + CONV_RC, :]
                acc = part if acc is None else acc + part
            conv_scr[pl.ds(r0, CONV_RC), cs] = acc
            return carry

        lax.fori_loop(0, TILE // CONV_RC, body, 0)

    v = conv_scr[...] + cb_ref[...]
    y = _silu(_layer_norm(v, lg_ref[...], lb_ref[...]))
    o_ref[...] = (jnp.dot(y.astype(BF16), wp_ref[...], preferred_element_type=F32) + bp_ref[...]).astype(BF16)


def _conv_module(u, conv_w, conv_b, ln_g, ln_b, w_pw2, b_pw2):
    hb = TILE // HALO
    row = lambda a: a.reshape(1, CW)
    return pl.pallas_call(
        _conv_kernel,
        out_shape=jax.ShapeDtypeStruct((T, CW), BF16),
        grid=(NTILES,),
        in_specs=[
            pl.BlockSpec((TILE, CW), lambda i: (i, 0)),
            pl.BlockSpec((HALO, CW), lambda i: (jnp.maximum(i * hb - 1, 0), 0)),
            pl.BlockSpec((HALO, CW), lambda i: (jnp.minimum((i + 1) * hb, T // HALO - 1), 0)),
            pl.BlockSpec((CONV_K, CW), lambda i: (0, 0)),
            pl.BlockSpec((1, CW), lambda i: (0, 0)),
            pl.BlockSpec((1, CW), lambda i: (0, 0)),
            pl.BlockSpec((1, CW), lambda i: (0, 0)),
            pl.BlockSpec((CW, CW), lambda i: (0, 0)),
            pl.BlockSpec((1, CW), lambda i: (0, 0)),
        ],
        out_specs=pl.BlockSpec((TILE, CW), lambda i: (i, 0)),
        scratch_shapes=[pltpu.VMEM((TILE + 2 * HALO, CW), F32), pltpu.VMEM((TILE, CW), F32)],
        compiler_params=_params(1),
        name="conv_module",
    )(u, u, u, conv_w, row(conv_b), row(ln_g), row(ln_b), w_pw2, row(b_pw2))


def _split3(x):
    hi = x.astype(BF16)
    r1 = x - hi.astype(F32)
    mid = r1.astype(BF16)
    lo = (r1 - mid.astype(F32)).astype(BF16)
    return hi, mid, lo


def _log_sigmoid(x):
    return jnp.minimum(x, 0.0) - jnp.log(1.0 + jnp.exp(-jnp.abs(x)))


def _dot_nt(a, b):
    return lax.dot_general(a, b, (((1,), (1,)), ((), ())), preferred_element_type=F32)


def _dot_tn(a, b):
    return lax.dot_general(a, b, (((0,), (0,)), ((), ())), preferred_element_type=F32)


def _gla_kernel(q_ref, k_ref, v_ref, a_ref, wa2_ref, ba2_ref, tri_ref, s0_ref,
                o_ref, snew_ref, st_scr, *, reverse):
    n = pl.program_id(0)
    tile = (NTILES - 1 - n) if reverse else n
    is_prompt = tile < NTILES_P
    spos = jnp.maximum(tile - NTILES_P, 0) % TILES_PER_S
    seq_first = spos == (TILES_PER_S - 1 if reverse else 0)

    @pl.when(is_prompt)
    def _():
        st_scr[...] = jnp.zeros_like(st_scr)

    @pl.when(jnp.logical_and(jnp.logical_not(is_prompt), seq_first))
    def _():
        for h in range(H):
            st_scr[h] = s0_ref[0, 0, h].T

    tri = tri_ref[...]
    mask = tri > 0
    a_lr = a_ref[:, 0:2 * RANK].astype(BF16)
    n_chunks = TILE // CHUNK
    order = range(n_chunks - 1, -1, -1) if reverse else range(n_chunks)
    logits = jnp.dot(a_lr, wa2_ref[...], preferred_element_type=F32) + ba2_ref[...]
    hi, mid, lo = _split3(_log_sigmoid(logits) / TAU)
    b_all = (jnp.dot(tri, hi, preferred_element_type=F32)
             + jnp.dot(tri, mid, preferred_element_type=F32)
             + jnp.dot(tri, lo, preferred_element_type=F32))
    for h in range(H):
        ks = slice(h * DK, (h + 1) * DK)
        b = b_all[:, ks]
        q = q_ref[:, ks] * (DK ** -0.5)
        k = k_ref[:, ks]
        vb = v_ref[:, h * DV:(h + 1) * DV].astype(BF16)
        qd = (q * jnp.exp(b)).astype(BF16)
        kd = (k * jnp.exp(-b)).astype(BF16)
        att = jnp.where(mask, _dot_nt(qd, kd), 0.0).astype(BF16)
        o_intra = jnp.dot(att, vb, preferred_element_type=F32)
        st = st_scr[h]
        for c in order:
            rows = slice(c * CHUNK, (c + 1) * CHUNK)
            last = c * CHUNK if reverse else (c + 1) * CHUNK - 1
            b_last = b[last:last + 1, :]
            kl = (k[rows] * jnp.exp(b_last - b[rows])).astype(BF16)
            o_ref[rows, h * DV:(h + 1) * DV] = o_intra[rows] + _dot_nt(qd[rows], st.astype(BF16))
            st = jnp.exp(b_last) * st + _dot_tn(vb[rows], kl)
        st_scr[h] = st

    @pl.when(is_prompt)
    def _():
        for h in range(H):
            snew_ref[0, h] = st_scr[h].T


def _gla(z, a_lr, wa2_dir, ba2_dir, tri_dir, s0, direction):
    reverse = direction == 1
    tile = (lambda n: NTILES - 1 - n) if reverse else (lambda n: n)
    req = lambda n: jnp.clip((tile(n) - NTILES_P) // TILES_PER_S, 0, NB_S - 1)
    return pl.pallas_call(
        functools.partial(_gla_kernel, reverse=reverse),
        out_shape=(jax.ShapeDtypeStruct((T, GW), F32),
                   jax.ShapeDtypeStruct((NB_P, H, DK, DV), F32)),
        grid=(NTILES,),
        in_specs=[
            pl.BlockSpec((TILE, QC), lambda n: (tile(n), 0)),
            pl.BlockSpec((TILE, QC), lambda n: (tile(n), 1)),
            pl.BlockSpec((TILE, GW), lambda n: (tile(n), 1)),
            pl.BlockSpec((TILE, 128), lambda n: (tile(n), 0)),
            pl.BlockSpec((2 * RANK, QC), lambda n: (0, 0)),
            pl.BlockSpec((1, QC), lambda n: (0, 0)),
            pl.BlockSpec((TILE, TILE), lambda n: (0, 0)),
            pl.BlockSpec((1, 1, H, DK, DV), lambda n: (req(n), direction, 0, 0, 0)),
        ],
        out_specs=(pl.BlockSpec((TILE, GW), lambda n: (tile(n), 0)),
                   pl.BlockSpec((1, H, DK, DV), lambda n: (jnp.minimum(tile(n), NTILES_P - 1), 0, 0, 0))),
        scratch_shapes=[pltpu.VMEM((H, DV, DK), F32)],
        compiler_params=_params(1),
        name="gla_bwd" if reverse else "gla_fwd",
    )(z, z, z, a_lr, wa2_dir, ba2_dir, tri_dir, s0)


def _mixout_kernel(of_ref, ob_ref, g_ref, u_ref, x_ref, mod_ref, gng_ref, wo_ref, l1g_ref, l1b_ref,
                   *rest, with_router):
    if with_router:
        wr_ref, br_ref, ltri_ref, x1_ref, h2_ref, route_ref, cnt_ref, run_scr = rest
    else:
        x1_ref, h2_ref = rest
    o = of_ref[...] + ob_ref[...]
    parts = []
    for h in range(H):
        oh = o[:, h * DV:(h + 1) * DV]
        ms = jnp.mean(oh * oh, -1, keepdims=True)
        parts.append(oh * lax.rsqrt(ms + RMS_EPS) * gng_ref[...])
    on = jnp.concatenate(parts, axis=-1) * _silu(g_ref[...])
    y = (jnp.dot(on.astype(BF16), wo_ref[0:GW, :], preferred_element_type=F32)
         + jnp.dot(u_ref[...], wo_ref[GW:D, :], preferred_element_type=F32))
    m = mod_ref[0]
    x1 = _layer_norm(ALPHA * x_ref[...] + m[G1:G1 + 1, :] * y, l1g_ref[...], l1b_ref[...])
    x1_ref[...] = x1
    h2 = x1 * (1.0 + m[SC2:SC2 + 1, :]) + m[SH2:SH2 + 1, :]
    h2_ref[...] = h2.astype(h2_ref.dtype)
    if with_router:
        h_hi = h2.astype(BF16)
        h_lo = (h2 - h_hi.astype(F32)).astype(BF16)
        p_hi = jnp.dot(h_hi, wr_ref[...], preferred_element_type=F32)
        p_lo = jnp.dot(h_lo, wr_ref[...], preferred_element_type=F32)
        logits = p_hi + pltpu.roll(p_hi, 128 - NE, 1) + p_lo + br_ref[...]
        lane = lax.broadcasted_iota(jnp.int32, logits.shape, 1)
        lg = jnp.where(lane < NE, logits, -jnp.inf)
        v1 = jnp.max(lg, -1, keepdims=True)
        i1 = jnp.min(jnp.where(lg == v1, lane, 128), -1, keepdims=True)
        lg2 = jnp.where(lane == i1, -jnp.inf, lg)
        v2 = jnp.max(lg2, -1, keepdims=True)
        i2 = jnp.min(jnp.where(lg2 == v2, lane, 128), -1, keepdims=True)
        e2 = jnp.exp(v2 - v1)
        w1 = 1.0 / (1.0 + e2)
        w2 = e2 / (1.0 + e2)
        @pl.when(pl.program_id(0) == 0)
        def _():
            run_scr[...] = jnp.zeros_like(run_scr)

        hit1 = lane == i1
        hit2 = lane == i2
        one1 = jnp.where(hit1, 1.0, 0.0)
        one2 = jnp.where(hit2, 1.0, 0.0)
        before1 = jnp.dot(ltri_ref[...], one1.astype(BF16), preferred_element_type=F32)
        before2 = jnp.dot(ltri_ref[...], one2.astype(BF16), preferred_element_type=F32)
        tot1 = jnp.sum(one1, axis=0, keepdims=True)
        tot2 = jnp.sum(one2, axis=0, keepdims=True)
        run = run_scr[...]
        rank1 = jnp.sum(jnp.where(hit1, before1 + run, 0.0), -1, keepdims=True)
        rank2 = jnp.sum(jnp.where(hit2, before2 + (run + tot1), 0.0), -1, keepdims=True)
        run = run + tot1 + tot2
        run_scr[...] = run
        cnt_ref[...] = jnp.broadcast_to(run, cnt_ref.shape)
        route_ref[...] = jnp.where(lane == 0, i1.astype(F32),
                         jnp.where(lane == 1, i2.astype(F32),
                         jnp.where(lane == 2, w1,
                         jnp.where(lane == 3, w2,
                         jnp.where(lane == 4, rank1, jnp.where(lane == 5, rank2, 0.0))))))


def _mixout(o_f, o_b, z, u2, x, mod_l, gng, w_out, ln_g, ln_b, router=None):
    row = lambda a: a.reshape(1, -1)
    full = lambda shape: pl.BlockSpec(shape, lambda i: (0,) * len(shape))
    in_specs = [
        pl.BlockSpec((TILE, GW), lambda i: (i, 0)),
        pl.BlockSpec((TILE, GW), lambda i: (i, 0)),
        pl.BlockSpec((TILE, GW), lambda i: (i, 2)),
        pl.BlockSpec((TILE, CW), lambda i: (i, 0)),
        pl.BlockSpec((TILE, D), lambda i: (i, 0)),
        pl.BlockSpec((1, 6, D), lambda i: (_mod_row(i * TILE), 0, 0)),
        full((1, DV)), full((D, D)), full((1, D)), full((1, D)),
    ]
    args = [o_f, o_b, z, u2, x, mod_l, row(gng), w_out, row(ln_g), row(ln_b)]
    out_shape = [jax.ShapeDtypeStruct((T, D), F32)]
    out_specs = [pl.BlockSpec((TILE, D), lambda i: (i, 0)), pl.BlockSpec((TILE, D), lambda i: (i, 0))]
    scratch = []
    if router is None:
        out_shape.append(jax.ShapeDtypeStruct((T, D), BF16))
    else:
        w_r, b_r = router
        idx = jnp.arange(TILE)
        ltri = (idx[None, :] < idx[:, None]).astype(BF16)
        in_specs += [full((D, 128)), full((1, 128)), full((TILE, TILE))]
        args += [w_r, b_r, ltri]
        out_shape += [jax.ShapeDtypeStruct((T, D), F32),
                      jax.ShapeDtypeStruct((T, 128), F32), jax.ShapeDtypeStruct((8, 128), F32)]
        out_specs += [pl.BlockSpec((TILE, 128), lambda i: (i, 0)), pl.BlockSpec((8, 128), lambda i: (0, 0))]
        scratch = [pltpu.VMEM((1, 128), F32)]
    return pl.pallas_call(
        functools.partial(_mixout_kernel, with_router=router is not None),
        out_shape=tuple(out_shape),
        grid=(NTILES,),
        in_specs=in_specs,
        out_specs=tuple(out_specs),
        scratch_shapes=scratch,
        compiler_params=_params(1),
        name="mixout",
    )(*args)


FFN_R = 1024
FFN_SUB = 256
FFN_TF = 512


def _ffn_kernel(te_ref, tn_ref, tb_ref, h_ref, w1_ref, w3_ref, w2_ref, o_ref, g_scr, a_scr, *h_scr, wide_rows):
    s = pl.program_id(0)
    j = pl.program_id(1)
    n = tn_ref[s]
    n_sub = (n + FFN_SUB - 1) // FFN_SUB

    @pl.when(jnp.logical_and(j == 0, n > 0))
    def _():
        o_ref[...] = jnp.zeros_like(o_ref)

    rows_ref = h_scr[0] if wide_rows else h_ref
    for k in range(1, FFN_R // FFN_SUB + 1):
        m = k * FFN_SUB

        if wide_rows:
            @pl.when(jnp.logical_and(n_sub == k, j == 0))
            def _(m=m):
                rows_ref[0:m, :] = h_ref[0:m, :].astype(BF16)

        @pl.when(n_sub == k)
        def _(m=m):
            hc = rows_ref[0:m, :]
            g_scr[0:m, :] = jnp.dot(hc, w1_ref[0].astype(BF16), preferred_element_type=F32)
            u = jnp.dot(hc, w3_ref[0].astype(BF16), preferred_element_type=F32)
            a_scr[0:m, :] = (_silu(g_scr[0:m, :]) * u).astype(BF16)
            o_ref[0:m, :] += jnp.dot(a_scr[0:m, :], w2_ref[0].astype(BF16), preferred_element_type=F32)


def _ffn(hs, tile_expert, tile_rows, tile_block, w1, w3, w2):
    n_tiles = tile_expert.shape[0]
    nj = FF // FFN_TF

    def jj(s, j, tn):
        return jnp.where(tn[s] > 0, j, nj - 1)

    once = pl.Buffered(1)
    wide_rows = hs.dtype == F32
    scratch = [pltpu.VMEM((FFN_R, FFN_TF), F32), pltpu.VMEM((FFN_R, FFN_TF), BF16)]
    if wide_rows:
        scratch.append(pltpu.VMEM((FFN_R, D), BF16))
    return pl.pallas_call(
        functools.partial(_ffn_kernel, wide_rows=wide_rows),
        out_shape=jax.ShapeDtypeStruct((hs.shape[0], D), F32),
        grid_spec=pltpu.PrefetchScalarGridSpec(
            num_scalar_prefetch=3,
            grid=(n_tiles, nj),
            in_specs=[
                pl.BlockSpec((FFN_R, hs.shape[1]), lambda s, j, te, tn, tb: (tb[s], 0), pipeline_mode=once),
                pl.BlockSpec((1, D, FFN_TF), lambda s, j, te, tn, tb: (te[s], 0, jj(s, j, tn))),
                pl.BlockSpec((1, D, FFN_TF), lambda s, j, te, tn, tb: (te[s], 0, jj(s, j, tn))),
                pl.BlockSpec((1, FFN_TF, D), lambda s, j, te, tn, tb: (te[s], jj(s, j, tn), 0)),
            ],
            out_specs=pl.BlockSpec((FFN_R, D), lambda s, j, te, tn, tb: (tb[s], 0), pipeline_mode=once),
            scratch_shapes=scratch,
        ),
        compiler_params=_params(2),
        name="ffn",
    )(tile_expert, tile_rows, tile_block, hs, w1, w3, w2)


DMA_UNROLL = 8


def _dispatch_kernel(pos_ref, pad_ref, h_ref, o_hbm, zero_scr, sem, zsem):
    i = pl.program_id(0)

    @pl.when(i == 0)
    def _():
        zero_scr[...] = jnp.zeros_like(zero_scr)
        for e in range(NE):
            first, count = pad_ref[e], pad_ref[NE + e]

            def zero_copy(r, first=first):
                return pltpu.make_async_copy(zero_scr, o_hbm.at[pl.ds(first + r, 1), :], zsem)

            def start(r, carry):
                zero_copy(r).start()
                return carry

            def wait(r, carry):
                zero_copy(r).wait()
                return carry

            lax.fori_loop(0, count, start, 0)
            lax.fori_loop(0, count, wait, 0)

    def issue(r, carry):
        for slot in range(2):
            p = pos_ref[2 * (i * TILE + r) + slot]
            pltpu.make_async_copy(h_ref.at[pl.ds(r, 1), :], o_hbm.at[pl.ds(p, 1), :], sem).start()
        return carry

    lax.fori_loop(0, TILE, issue, 0, unroll=DMA_UNROLL)
    for slot in range(2):
        pltpu.make_async_copy(h_ref, o_hbm.at[pl.ds(0, TILE), :], sem).wait()


def _dispatch(pos, pad, h2):
    return pl.pallas_call(
        _dispatch_kernel,
        out_shape=jax.ShapeDtypeStruct((MOE_TILES * FFN_R, D), F32),
        grid_spec=pltpu.PrefetchScalarGridSpec(
            num_scalar_prefetch=2,
            grid=(NTILES,),
            in_specs=[pl.BlockSpec((TILE, D), lambda i, pos, pad: (i, 0))],
            out_specs=pl.BlockSpec(memory_space=pl.ANY),
            scratch_shapes=[pltpu.VMEM((1, D), F32), pltpu.SemaphoreType.DMA, pltpu.SemaphoreType.DMA],
        ),
        compiler_params=_params(1),
        name="dispatch",
    )(pos, pad, h2)


def _ln2_kernel(x_ref, f_ref, mod_ref, g_ref, b_ref, o_ref):
    m = mod_ref[0]
    o_ref[...] = _layer_norm(ALPHA * x_ref[...] + m[G2:G2 + 1, :] * f_ref[...], g_ref[...], b_ref[...])


def _ln2(x1, f, mod_l, ln_g, ln_b):
    return pl.pallas_call(
        _ln2_kernel,
        out_shape=jax.ShapeDtypeStruct((T, D), F32),
        grid=(NTILES,),
        in_specs=[
            pl.BlockSpec((TILE, D), lambda i: (i, 0)),
            pl.BlockSpec((TILE, D), lambda i: (i, 0)),
            pl.BlockSpec((1, 6, D), lambda i: (_mod_row(i * TILE), 0, 0)),
            pl.BlockSpec((1, D), lambda i: (0, 0)),
            pl.BlockSpec((1, D), lambda i: (0, 0)),
        ],
        out_specs=pl.BlockSpec((TILE, D), lambda i: (i, 0)),
        compiler_params=_params(1),
        name="ln2",
    )(x1, f, mod_l, ln_g.reshape(1, D), ln_b.reshape(1, D))


def _combine_kernel(pos_ref, x_ref, route_ref, mod_ref, g_ref, b_ref, y_hbm, op_ref, os_ref, buf, sem):
    i = pl.program_id(0)

    def gather(tile):
        def issue(r, carry):
            for slot in range(2):
                p = pos_ref[2 * (tile * TILE + r) + slot]
                pltpu.make_async_copy(y_hbm.at[pl.ds(p, 1), :], buf.at[tile % 2, slot, pl.ds(r, 1), :],
                                      sem.at[tile % 2]).start()
            return carry

        lax.fori_loop(0, TILE, issue, 0, unroll=DMA_UNROLL)

    @pl.when(i == 0)
    def _():
        gather(i)

    @pl.when(i + 1 < NTILES)
    def _():
        gather(i + 1)

    for slot in range(2):
        pltpu.make_async_copy(y_hbm.at[pl.ds(0, TILE), :], buf.at[i % 2, slot], sem.at[i % 2]).wait()
    route = route_ref[...]
    f = route[:, 2:3] * buf[i % 2, 0] + route[:, 3:4] * buf[i % 2, 1]
    m = mod_ref[0]
    out = _layer_norm(ALPHA * x_ref[...] + m[G2:G2 + 1, :] * f, g_ref[...], b_ref[...])

    @pl.when(i < NTILES_P)
    def _():
        op_ref[...] = out

    @pl.when(i >= NTILES_P)
    def _():
        os_ref[...] = out


def _combine(pos, x1, route, mod_l, ln_g, ln_b, y):
    return pl.pallas_call(
        _combine_kernel,
        out_shape=(jax.ShapeDtypeStruct((TP, D), F32), jax.ShapeDtypeStruct((T - TP, D), F32)),
        grid_spec=pltpu.PrefetchScalarGridSpec(
            num_scalar_prefetch=1,
            grid=(NTILES,),
            in_specs=[
                pl.BlockSpec((TILE, D), lambda i, pos: (i, 0)),
                pl.BlockSpec((TILE, 128), lambda i, pos: (i, 0)),
                pl.BlockSpec((1, 6, D), lambda i, pos: (_mod_row(i * TILE), 0, 0)),
                pl.BlockSpec((1, D), lambda i, pos: (0, 0)),
                pl.BlockSpec((1, D), lambda i, pos: (0, 0)),
                pl.BlockSpec(memory_space=pl.ANY),
            ],
            out_specs=(pl.BlockSpec((TILE, D), lambda i, pos: (jnp.minimum(i, NTILES_P - 1), 0)),
                       pl.BlockSpec((TILE, D), lambda i, pos: (jnp.maximum(i - NTILES_P, 0), 0))),
            scratch_shapes=[pltpu.VMEM((2, 2, TILE, D), F32), pltpu.SemaphoreType.DMA((2,))],
        ),
        compiler_params=_params(1),
        name="combine",
    )(pos, x1, route, mod_l, ln_g.reshape(1, D), ln_b.reshape(1, D), y)


MOE_TILES = 2 * T // FFN_R + NE


def _routing_tables(route, counts_f):
    eidx = route[:, 0:2].astype(jnp.int32)
    rank = route[:, 4:6].astype(jnp.int32)
    counts = counts_f[0, :NE].astype(jnp.int32)
    ntiles = (counts + FFN_R - 1) // FFN_R
    tend = jnp.cumsum(ntiles)
    tstart = tend - ntiles
    experts = jnp.arange(NE, dtype=jnp.int32)
    first_row = jnp.sum(jnp.where(eidx[..., None] == experts, tstart * FFN_R, 0), axis=-1)
    pos = (first_row + rank).reshape(-1)
    tiles = jnp.arange(MOE_TILES, dtype=jnp.int32)
    total = tend[-1]
    t_eff = jnp.minimum(tiles, total - 1)
    tile_expert = jnp.minimum(jnp.sum((t_eff[:, None] >= tend[None, :]).astype(jnp.int32), axis=1), NE - 1)
    tile_rows = jnp.clip(counts[tile_expert] - (t_eff - tstart[tile_expert]) * FFN_R, 0, FFN_R)
    tile_rows = jnp.where(tiles < total, tile_rows, 0)
    pad = jnp.concatenate([tstart * FFN_R + counts, (-counts) % FFN_SUB]).astype(jnp.int32)
    return pos, pad, tile_expert.astype(jnp.int32), tile_rows.astype(jnp.int32), t_eff


def _grid_pos_emb():
    rows = L_S // GRID_W
    r = np.repeat(np.arange(rows, dtype=np.float64), GRID_W)
    col = np.tile(np.arange(GRID_W, dtype=np.float64), rows)
    quarter = D // 4
    freqs = 1.0 / (10000.0 ** (np.arange(quarter, dtype=np.float64) / quarter))

    def enc(p):
        ang = p[:, None] * freqs[None, :]
        return np.concatenate([np.sin(ang), np.cos(ang)], -1)

    return jnp.asarray(np.concatenate([enc(r), enc(col)], -1), dtype=F32)


def _cumsum_matrices():
    i = jnp.arange(TILE)
    same = (i[:, None] // CHUNK) == (i[None, :] // CHUNK)
    fwd = jnp.logical_and(same, i[None, :] <= i[:, None])
    bwd = jnp.logical_and(same, i[None, :] >= i[:, None])
    return fwd.astype(BF16), bwd.astype(BF16)


def kernel(x_prompt, x_sample, state_gla, c, c_ctx, w_mod, b_mod, w_in, w_a2, b_a2, gla_norm_g, conv_w, conv_b, conv_ln_g, conv_ln_b, w_pw2, b_pw2, w_out, ln1_g, ln1_b, ln2_g, ln2_b, ffn_w1, ffn_w3, ffn_w2, moe_w_router, moe_b_router, moe_w1, moe_w3, moe_w2):
    x = _assemble(x_prompt.reshape(TP, D), x_sample.reshape(T - TP, D), _grid_pos_emb())
    cond8 = jnp.zeros((8, D), F32).at[0].set(c_ctx).at[1:1 + NB_S].set(c)
    mod = _modulation(cond8, w_mod, b_mod)
    tri = _cumsum_matrices()
    w_prep = _wprep(w_in)
    dense_tiles = T // FFN_R
    dense_meta = (jnp.zeros((dense_tiles,), jnp.int32), jnp.full((dense_tiles,), FFN_R, jnp.int32),
                  jnp.arange(dense_tiles, dtype=jnp.int32))

    states = []
    for l in range(DEPTH):
        mod_l = mod[l]
        z, a_lr, u = _inproj(x, mod_l, w_prep, l)
        u2 = _conv_module(u, conv_w[l], conv_b[l], conv_ln_g[l], conv_ln_b[l],
                          w_pw2[l].astype(BF16), b_pw2[l])
        o_dir, s_dir = [], []
        for d in range(2):
            wa2 = jnp.zeros((2 * RANK, QC), F32).at[d * RANK:(d + 1) * RANK].set(w_a2[l, d]).astype(BF16)
            o, s_new = _gla(z, a_lr, wa2, b_a2[l, d].reshape(1, QC), tri[d], state_gla[:, l], d)
            o_dir.append(o)
            s_dir.append(s_new)
        states.append(jnp.stack(s_dir, axis=1))
        is_moe = l % 2 == 1
        i = l // 2
        router = None
        if is_moe:
            wr_hi = moe_w_router[i].astype(BF16)
            wr_lo = (moe_w_router[i] - wr_hi.astype(F32)).astype(BF16)
            router = (jnp.pad(jnp.concatenate([wr_hi, wr_lo], axis=1), ((0, 0), (0, 128 - 2 * NE))),
                      jnp.pad(moe_b_router[i], (0, 128 - NE)).reshape(1, 128))
        outs = _mixout(o_dir[0], o_dir[1], z, u2, x, mod_l, gla_norm_g[l], w_out[l].astype(BF16),
                       ln1_g[l], ln1_b[l], router)
        if is_moe:
            x1, h2, route, counts = outs
            pos, pad, t_exp, t_rows, t_blk = _routing_tables(route, counts)
            hs = _dispatch(pos, pad, h2)
            y = _ffn(hs, t_exp, t_rows, t_blk, moe_w1[i], moe_w3[i], moe_w2[i])
            out_p, out_s = _combine(pos, x1, route, mod_l, ln2_g[l], ln2_b[l], y)
            x = jnp.concatenate([out_p, out_s], axis=0) if l + 1 < DEPTH else None
        else:
            x1, h2 = outs
            y = _ffn(h2, *dense_meta, ffn_w1[i][None], ffn_w3[i][None], ffn_w2[i][None])
            x = _ln2(x1, y, mod_l, ln2_g[l], ln2_b[l])

    if x is not None:
        out_p, out_s = x[:TP], x[TP:]
    return out_p.reshape(NB_P, L_P, D), out_s.reshape(NB_S, L_S, D), jnp.stack(states, axis=1)
```

```python
import functools

import jax
import jax.numpy as jnp
import numpy as np
from jax import lax
from jax.experimental import pallas as pl
from jax.experimental.pallas import tpu as pltpu

F32 = jnp.float32
BF16 = jnp.bfloat16

D = 2048
NB_P, L_P = 16, 256
NB_S, L_S = 4, 1024
TP = NB_P * L_P
T = TP + NB_S * L_S
DEPTH = 2
GRID_W = 64
GW = D // 2
CW = D - GW
H = 4
DV = GW // H
DK = DV // 2
QC = H * DK
RANK = 16
TAU = 16.0
CHUNK = 64
CONV_K = 31
FF = 7 * D // 2
NE = 8
ALPHA = (2 * DEPTH) ** 0.25
LN_EPS = 1e-5
RMS_EPS = 1e-6
MAIN_COLS = 2 * QC + 2 * GW

TILE = 256
NTILES = T // TILE
NTILES_P = TP // TILE
TILES_PER_S = L_S // TILE
HALO = 16

VMEM_LIMIT = 56 * 1024 * 1024

SH1, SC1, G1, SH2, SC2, G2 = range(6)


def _params(n_axes, vmem=VMEM_LIMIT):
    return pltpu.CompilerParams(dimension_semantics=("arbitrary",) * n_axes,
                                vmem_limit_bytes=vmem)


def _mod_row(tok0):
    return jnp.where(tok0 < TP, 0, 1 + (tok0 - TP) // L_S)


def _sigmoid(x):
    return 1.0 / (1.0 + jnp.exp(-x))


def _silu(x):
    return x * _sigmoid(x)


def _layer_norm(r, g, b):
    mu = jnp.mean(r, -1, keepdims=True)
    rc = r - mu
    var = jnp.mean(rc * rc, -1, keepdims=True)
    return rc * lax.rsqrt(var + LN_EPS) * g + b


def _assemble_kernel(xp_ref, xs_ref, pos_ref, o_ref):
    i = pl.program_id(0)

    @pl.when(i < NTILES_P)
    def _():
        o_ref[...] = xp_ref[...]

    @pl.when(i >= NTILES_P)
    def _():
        o_ref[...] = xs_ref[...] + pos_ref[...]


def _assemble(xp, xs, pos):
    return pl.pallas_call(
        _assemble_kernel,
        out_shape=jax.ShapeDtypeStruct((T, D), F32),
        grid=(NTILES,),
        in_specs=[
            pl.BlockSpec((TILE, D), lambda i: (jnp.minimum(i, NTILES_P - 1), 0)),
            pl.BlockSpec((TILE, D), lambda i: (jnp.maximum(i - NTILES_P, 0), 0)),
            pl.BlockSpec((TILE, D), lambda i: (jnp.maximum(i - NTILES_P, 0) % TILES_PER_S, 0)),
        ],
        out_specs=pl.BlockSpec((TILE, D), lambda i: (i, 0)),
        compiler_params=_params(1),
        name="assemble",
    )(xp, xs, pos)


MOD_TN = 1024


def _mod_kernel(c_ref, w_ref, b_ref, o_ref):
    s = _silu(c_ref[...])
    s_hi = s.astype(BF16)
    s_lo = (s - s_hi.astype(F32)).astype(BF16)
    w = w_ref[0]
    w_hi = w.astype(BF16)
    w_lo = (w - w_hi.astype(F32)).astype(BF16)
    o_ref[0] = (jnp.dot(s_hi, w_hi, preferred_element_type=F32)
                + jnp.dot(s_hi, w_lo, preferred_element_type=F32)
                + jnp.dot(s_lo, w_hi, preferred_element_type=F32)) + b_ref[0]


def _modulation(cond8, w_mod, b_mod):
    out = pl.pallas_call(
        _mod_kernel,
        out_shape=jax.ShapeDtypeStruct((DEPTH, 8, 6 * D), F32),
        grid=(DEPTH, 6 * D // MOD_TN),
        in_specs=[
            pl.BlockSpec((8, D), lambda l, j: (0, 0)),
            pl.BlockSpec((1, D, MOD_TN), lambda l, j: (l, 0, j)),
            pl.BlockSpec((1, 1, MOD_TN), lambda l, j: (l, 0, j)),
        ],
        out_specs=pl.BlockSpec((1, 8, MOD_TN), lambda l, j: (l, 0, j)),
        compiler_params=_params(2),
        name="modulation",
    )(cond8, w_mod, b_mod.reshape(DEPTH, 1, 6 * D))
    return out.reshape(DEPTH, 8, 6, D)


PROJ_TM = 1024
PROJ_TN = 1024
PROJ_NMAIN = MAIN_COLS // PROJ_TN
GLU_TN = 512
PROJ_NGLU = CW // GLU_TN

TAIL_COLS = 2 * RANK + 2 * CW
PREP_COLS = 2 * CW + 128
PREP_ROWS = 256


def _wprep_kernel(w_ref, o_ref):
    o_ref[0, :, 0:2 * CW] = w_ref[0, :, 2 * RANK:TAIL_COLS].astype(BF16)
    a = w_ref[0, :, 0:128]
    lane = lax.broadcasted_iota(jnp.int32, a.shape, 1)
    o_ref[0, :, 2 * CW:PREP_COLS] = jnp.where(lane < 2 * RANK, a, 0.0).astype(BF16)


def _wprep(w_tail):
    return pl.pallas_call(
        _wprep_kernel,
        out_shape=jax.ShapeDtypeStruct((DEPTH, D, PREP_COLS), BF16),
        grid=(DEPTH, D // PREP_ROWS),
        in_specs=[pl.BlockSpec((1, PREP_ROWS, TAIL_COLS), lambda l, r: (l, r, 0))],
        out_specs=pl.BlockSpec((1, PREP_ROWS, PREP_COLS), lambda l, r: (l, r, 0)),
        compiler_params=_params(2),
        name="wprep",
    )(w_tail)


def _modulated(x_ref, mod_ref, shift, scale):
    m = mod_ref[0]
    return x_ref[...] * (1.0 + m[scale:scale + 1, :]) + m[shift:shift + 1, :]


def _inproj_kernel(x_ref, mod_ref, w_ref, wa_ref, wua_ref, wug_ref, z_ref, a_ref, u_ref, h_scr):
    j = pl.program_id(1)

    @pl.when(j == 0)
    def _():
        hb = _modulated(x_ref, mod_ref, SH1, SC1).astype(BF16)
        h_scr[...] = hb
        a_ref[...] = jnp.dot(hb, wa_ref[0], preferred_element_type=F32)

    @pl.when(j < PROJ_NMAIN)
    def _():
        z_ref[...] = jnp.dot(h_scr[...], w_ref[0], preferred_element_type=F32)

    @pl.when(j >= PROJ_NMAIN)
    def _():
        h = h_scr[...]
        a = jnp.dot(h, wua_ref[0], preferred_element_type=F32)
        g = jnp.dot(h, wug_ref[0], preferred_element_type=F32)
        u_ref[...] = a * _sigmoid(g)


def _inproj(x, mod_l, w_main, w_prep, layer):
    main_j = lambda j: jnp.minimum(j, PROJ_NMAIN - 1)
    glu_j = lambda j: jnp.maximum(j - PROJ_NMAIN, 0)
    val0 = 0
    gate0 = CW // GLU_TN
    low_rank = 2 * CW // 128
    return pl.pallas_call(
        _inproj_kernel,
        out_shape=(jax.ShapeDtypeStruct((T, MAIN_COLS), F32),
                   jax.ShapeDtypeStruct((T, 128), F32),
                   jax.ShapeDtypeStruct((T, CW), F32)),
        grid=(T // PROJ_TM, PROJ_NMAIN + PROJ_NGLU),
        in_specs=[
            pl.BlockSpec((PROJ_TM, D), lambda i, j: (i, 0)),
            pl.BlockSpec((1, 6, D), lambda i, j: (_mod_row(i * PROJ_TM), 0, 0)),
            pl.BlockSpec((1, D, PROJ_TN), lambda i, j: (layer, 0, main_j(j))),
            pl.BlockSpec((1, D, 128), lambda i, j: (layer, 0, low_rank)),
            pl.BlockSpec((1, D, GLU_TN), lambda i, j: (layer, 0, val0 + glu_j(j))),
            pl.BlockSpec((1, D, GLU_TN), lambda i, j: (layer, 0, gate0 + glu_j(j))),
        ],
        out_specs=(pl.BlockSpec((PROJ_TM, PROJ_TN), lambda i, j: (i, main_j(j))),
                   pl.BlockSpec((PROJ_TM, 128), lambda i, j: (i, 0)),
                   pl.BlockSpec((PROJ_TM, GLU_TN), lambda i, j: (i, glu_j(j)))),
        scratch_shapes=[pltpu.VMEM((PROJ_TM, D), BF16)],
        compiler_params=_params(2),
        name="inproj",
    )(x, mod_l, w_main, w_prep, w_prep, w_prep)


CONV_RC = 32
CONV_CC = 256


def _conv_kernel(uc_ref, up_ref, un_ref, cw_ref, cb_ref, lg_ref, lb_ref, wp_ref, bp_ref,
                 o_ref, pad_scr, conv_scr):
    i = pl.program_id(0)
    s = jnp.maximum(i - NTILES_P, 0) % TILES_PER_S
    has_prev = jnp.logical_and(i >= NTILES_P, s != 0)
    has_next = jnp.logical_and(i >= NTILES_P, s != TILES_PER_S - 1)
    pad_scr[0:HALO, :] = jnp.where(has_prev, up_ref[...], 0.0)
    pad_scr[HALO:HALO + TILE, :] = uc_ref[...]
    pad_scr[HALO + TILE:HALO + TILE + HALO, :] = jnp.where(has_next, un_ref[...], 0.0)

    off = HALO - CONV_K // 2
    for c in range(CW // CONV_CC):
        cs = slice(c * CONV_CC, (c + 1) * CONV_CC)

        def body(r, carry, cs=cs):
            r0 = pl.multiple_of(r * CONV_RC, CONV_RC)
            win = pad_scr[pl.ds(r0, 2 * CONV_RC), cs]
            acc = None
            for b in range(8):
                part = None
                for a in range((CONV_K + off) // 8 + 1):
                    k = 8 * a + b - off
                    if 0 <= k < CONV_K:
                        term = cw_ref[k:k + 1, cs] * win[8 * a:8 * a + CONV_RC + 8, :]
                        part = term if part is None else part + term
                part = part[b:b + CONV_RC, :]
                acc = part if acc is None else acc + part
            conv_scr[pl.ds(r0, CONV_RC), cs] = acc
            return carry

        lax.fori_loop(0, TILE // CONV_RC, body, 0)

    v = conv_scr[...] + cb_ref[...]
    y = _silu(_layer_norm(v, lg_ref[...], lb_ref[...]))
    o_ref[...] = (jnp.dot(y.astype(BF16), wp_ref[...], preferred_element_type=F32) + bp_ref[...]).astype(BF16)


def _conv_module(u, conv_w, conv_b, ln_g, ln_b, w_pw2, b_pw2):
    hb = TILE // HALO
    row = lambda a: a.reshape(1, CW)
    return pl.pallas_call(
        _conv_kernel,
        out_shape=jax.ShapeDtypeStruct((T, CW), BF16),
        grid=(NTILES,),
        in_specs=[
            pl.BlockSpec((TILE, CW), lambda i: (i, 0)),
            pl.BlockSpec((HALO, CW), lambda i: (jnp.maximum(i * hb - 1, 0), 0)),
            pl.BlockSpec((HALO, CW), lambda i: (jnp.minimum((i + 1) * hb, T // HALO - 1), 0)),
            pl.BlockSpec((CONV_K, CW), lambda i: (0, 0)),
            pl.BlockSpec((1, CW), lambda i: (0, 0)),
            pl.BlockSpec((1, CW), lambda i: (0, 0)),
            pl.BlockSpec((1, CW), lambda i: (0, 0)),
            pl.BlockSpec((CW, CW), lambda i: (0, 0)),
            pl.BlockSpec((1, CW), lambda i: (0, 0)),
        ],
        out_specs=pl.BlockSpec((TILE, CW), lambda i: (i, 0)),
        scratch_shapes=[pltpu.VMEM((TILE + 2 * HALO, CW), F32), pltpu.VMEM((TILE, CW), F32)],
        compiler_params=_params(1),
        name="conv_module",
    )(u, u, u, conv_w, row(conv_b), row(ln_g), row(ln_b), w_pw2, row(b_pw2))


def _split3(x):
    hi = x.astype(BF16)
    r1 = x - hi.astype(F32)
    mid = r1.astype(BF16)
    lo = (r1 - mid.astype(F32)).astype(BF16)
    return hi, mid, lo


def _log_sigmoid(x):
    return jnp.minimum(x, 0.0) - jnp.log(1.0 + jnp.exp(-jnp.abs(x)))


def _dot_nt(a, b):
    return lax.dot_general(a, b, (((1,), (1,)), ((), ())), preferred_element_type=F32)


def _dot_tn(a, b):
    return lax.dot_general(a, b, (((0,), (0,)), ((), ())), preferred_element_type=F32)


def _gla_kernel(q_ref, k_ref, v_ref, a_ref, wa2_ref, ba2_ref, tri_ref, s0_ref,
                o_ref, snew_ref, st_scr, *, reverse):
    n = pl.program_id(0)
    tile = (NTILES - 1 - n) if reverse else n
    is_prompt = tile < NTILES_P
    spos = jnp.maximum(tile - NTILES_P, 0) % TILES_PER_S
    seq_first = spos == (TILES_PER_S - 1 if reverse else 0)

    @pl.when(is_prompt)
    def _():
        st_scr[...] = jnp.zeros_like(st_scr)

    @pl.when(jnp.logical_and(jnp.logical_not(is_prompt), seq_first))
    def _():
        for h in range(H):
            st_scr[h] = s0_ref[0, 0, h].T

    tri = tri_ref[...]
    mask = tri > 0
    a_lr = a_ref[:, 0:2 * RANK].astype(BF16)
    n_chunks = TILE // CHUNK
    order = range(n_chunks - 1, -1, -1) if reverse else range(n_chunks)
    logits = jnp.dot(a_lr, wa2_ref[...], preferred_element_type=F32) + ba2_ref[...]
    hi, mid, lo = _split3(_log_sigmoid(logits) / TAU)
    b_all = (jnp.dot(tri, hi, preferred_element_type=F32)
             + jnp.dot(tri, mid, preferred_element_type=F32)
             + jnp.dot(tri, lo, preferred_element_type=F32))
    for h in range(H):
        ks = slice(h * DK, (h + 1) * DK)
        b = b_all[:, ks]
        q = q_ref[:, ks] * (DK ** -0.5)
        k = k_ref[:, ks]
        vb = v_ref[:, h * DV:(h + 1) * DV].astype(BF16)
        qd = (q * jnp.exp(b)).astype(BF16)
        kd = (k * jnp.exp(-b)).astype(BF16)
        att = jnp.where(mask, _dot_nt(qd, kd), 0.0).astype(BF16)
        o_intra = jnp.dot(att, vb, preferred_element_type=F32)
        st = st_scr[h]
        for c in order:
            rows = slice(c * CHUNK, (c + 1) * CHUNK)
            last = c * CHUNK if reverse else (c + 1) * CHUNK - 1
            b_last = b[last:last + 1, :]
            kl = (k[rows] * jnp.exp(b_last - b[rows])).astype(BF16)
            o_ref[rows, h * DV:(h + 1) * DV] = o_intra[rows] + _dot_nt(qd[rows], st.astype(BF16))
            st = jnp.exp(b_last) * st + _dot_tn(vb[rows], kl)
        st_scr[h] = st

    @pl.when(is_prompt)
    def _():
        for h in range(H):
            snew_ref[0, h] = st_scr[h].T


def _gla(z, a_lr, wa2_dir, ba2_dir, tri_dir, s0, direction):
    reverse = direction == 1
    tile = (lambda n: NTILES - 1 - n) if reverse else (lambda n: n)
    req = lambda n: jnp.clip((tile(n) - NTILES_P) // TILES_PER_S, 0, NB_S - 1)
    return pl.pallas_call(
        functools.partial(_gla_kernel, reverse=reverse),
        out_shape=(jax.ShapeDtypeStruct((T, GW), F32),
                   jax.ShapeDtypeStruct((NB_P, H, DK, DV), F32)),
        grid=(NTILES,),
        in_specs=[
            pl.BlockSpec((TILE, QC), lambda n: (tile(n), 0)),
            pl.BlockSpec((TILE, QC), lambda n: (tile(n), 1)),
            pl.BlockSpec((TILE, GW), lambda n: (tile(n), 1)),
            pl.BlockSpec((TILE, 128), lambda n: (tile(n), 0)),
            pl.BlockSpec((2 * RANK, QC), lambda n: (0, 0)),
            pl.BlockSpec((1, QC), lambda n: (0, 0)),
            pl.BlockSpec((TILE, TILE), lambda n: (0, 0)),
            pl.BlockSpec((1, 1, H, DK, DV), lambda n: (req(n), direction, 0, 0, 0)),
        ],
        out_specs=(pl.BlockSpec((TILE, GW), lambda n: (tile(n), 0)),
                   pl.BlockSpec((1, H, DK, DV), lambda n: (jnp.minimum(tile(n), NTILES_P - 1), 0, 0, 0))),
        scratch_shapes=[pltpu.VMEM((H, DV, DK), F32)],
        compiler_params=_params(1),
        name="gla_bwd" if reverse else "gla_fwd",
    )(z, z, z, a_lr, wa2_dir, ba2_dir, tri_dir, s0)


MIX_GROUPS = 2


def _mixout_kernel(of_ref, ob_ref, g_ref, u_ref, x_ref, mod_ref, gng_ref, wo_ref, l1g_ref, l1b_ref,
                   *rest, with_router):
    if with_router:
        wr_ref, br_ref, ltri_ref, x1_ref, h2_ref, route_ref, cnt_ref, run_scr = rest
    else:
        x1_ref, h2_ref = rest
    m = mod_ref[0]
    h2_parts = []
    for p in range(MIX_GROUPS):
        rs = slice(p * (TILE // MIX_GROUPS), (p + 1) * (TILE // MIX_GROUPS))
        o = of_ref[rs, :] + ob_ref[rs, :]
        parts = []
        for h in range(H):
            oh = o[:, h * DV:(h + 1) * DV]
            ms = jnp.mean(oh * oh, -1, keepdims=True)
            parts.append(oh * lax.rsqrt(ms + RMS_EPS) * gng_ref[...])
        on = jnp.concatenate(parts, axis=-1) * _silu(g_ref[rs, :])
        y = (jnp.dot(on.astype(BF16), wo_ref[0:GW, :], preferred_element_type=F32)
             + jnp.dot(u_ref[rs, :], wo_ref[GW:D, :], preferred_element_type=F32))
        x1 = _layer_norm(ALPHA * x_ref[rs, :] + m[G1:G1 + 1, :] * y, l1g_ref[...], l1b_ref[...])
        x1_ref[rs, :] = x1
        h2_part = x1 * (1.0 + m[SC2:SC2 + 1, :]) + m[SH2:SH2 + 1, :]
        h2_ref[rs, :] = h2_part.astype(h2_ref.dtype)
        h2_parts.append(h2_part)
    if with_router:
        h2 = jnp.concatenate(h2_parts, axis=0)
        h_hi = h2.astype(BF16)
        h_lo = (h2 - h_hi.astype(F32)).astype(BF16)
        p_hi = jnp.dot(h_hi, wr_ref[...], preferred_element_type=F32)
        p_lo = jnp.dot(h_lo, wr_ref[...], preferred_element_type=F32)
        logits = p_hi + pltpu.roll(p_hi, 128 - NE, 1) + p_lo + br_ref[...]
        lane = lax.broadcasted_iota(jnp.int32, logits.shape, 1)
        lg = jnp.where(lane < NE, logits, -jnp.inf)
        v1 = jnp.max(lg, -1, keepdims=True)
        i1 = jnp.min(jnp.where(lg == v1, lane, 128), -1, keepdims=True)
        lg2 = jnp.where(lane == i1, -jnp.inf, lg)
        v2 = jnp.max(lg2, -1, keepdims=True)
        i2 = jnp.min(jnp.where(lg2 == v2, lane, 128), -1, keepdims=True)
        e2 = jnp.exp(v2 - v1)
        w1 = 1.0 / (1.0 + e2)
        w2 = e2 / (1.0 + e2)
        @pl.when(pl.program_id(0) == 0)
        def _():
            run_scr[...] = jnp.zeros_like(run_scr)

        hit1 = lane == i1
        hit2 = lane == i2
        one1 = jnp.where(hit1, 1.0, 0.0)
        one2 = jnp.where(hit2, 1.0, 0.0)
        before1 = jnp.dot(ltri_ref[...], one1.astype(BF16), preferred_element_type=F32)
        before2 = jnp.dot(ltri_ref[...], one2.astype(BF16), preferred_element_type=F32)
        tot1 = jnp.sum(one1, axis=0, keepdims=True)
        tot2 = jnp.sum(one2, axis=0, keepdims=True)
        run = run_scr[...]
        rank1 = jnp.sum(jnp.where(hit1, before1 + run, 0.0), -1, keepdims=True)
        rank2 = jnp.sum(jnp.where(hit2, before2 + (run + tot1), 0.0), -1, keepdims=True)
        run = run + tot1 + tot2
        run_scr[...] = run
        cnt_ref[...] = jnp.broadcast_to(run, cnt_ref.shape)
        route_ref[...] = jnp.where(lane == 0, i1.astype(F32),
                         jnp.where(lane == 1, i2.astype(F32),
                         jnp.where(lane == 2, w1,
                         jnp.where(lane == 3, w2,
                         jnp.where(lane == 4, rank1, jnp.where(lane == 5, rank2, 0.0))))))


def _mixout(o_f, o_b, z, u2, x, mod_l, gng, w_out, ln_g, ln_b, router=None):
    row = lambda a: a.reshape(1, -1)
    full = lambda shape: pl.BlockSpec(shape, lambda i: (0,) * len(shape))
    in_specs = [
        pl.BlockSpec((TILE, GW), lambda i: (i, 0)),
        pl.BlockSpec((TILE, GW), lambda i: (i, 0)),
        pl.BlockSpec((TILE, GW), lambda i: (i, 2)),
        pl.BlockSpec((TILE, CW), lambda i: (i, 0)),
        pl.BlockSpec((TILE, D), lambda i: (i, 0)),
        pl.BlockSpec((1, 6, D), lambda i: (_mod_row(i * TILE), 0, 0)),
        full((1, DV)), full((D, D)), full((1, D)), full((1, D)),
    ]
    args = [o_f, o_b, z, u2, x, mod_l, row(gng), w_out, row(ln_g), row(ln_b)]
    out_shape = [jax.ShapeDtypeStruct((T, D), F32)]
    out_specs = [pl.BlockSpec((TILE, D), lambda i: (i, 0)), pl.BlockSpec((TILE, D), lambda i: (i, 0))]
    scratch = []
    if router is None:
        out_shape.append(jax.ShapeDtypeStruct((T, D), BF16))
    else:
        w_r, b_r = router
        idx = jnp.arange(TILE)
        ltri = (idx[None, :] < idx[:, None]).astype(BF16)
        in_specs += [full((D, 128)), full((1, 128)), full((TILE, TILE))]
        args += [w_r, b_r, ltri]
        out_shape += [jax.ShapeDtypeStruct((T, D), F32),
                      jax.ShapeDtypeStruct((T, 128), F32), jax.ShapeDtypeStruct((8, 128), F32)]
        out_specs += [pl.BlockSpec((TILE, 128), lambda i: (i, 0)), pl.BlockSpec((8, 128), lambda i: (0, 0))]
        scratch = [pltpu.VMEM((1, 128), F32)]
    return pl.pallas_call(
        functools.partial(_mixout_kernel, with_router=router is not None),
        out_shape=tuple(out_shape),
        grid=(NTILES,),
        in_specs=in_specs,
        out_specs=tuple(out_specs),
        scratch_shapes=scratch,
        compiler_params=_params(1),
        name="mixout",
    )(*args)


FFN_R = 1024
FFN_SUB = 256
FFN_TF = 512


def _ffn_kernel(te_ref, tn_ref, tb_ref, h_ref, w1_ref, w3_ref, w2_ref, o_ref, g_scr, a_scr, *h_scr, wide_rows):
    s = pl.program_id(0)
    j = pl.program_id(1)
    n = tn_ref[s]
    n_sub = (n + FFN_SUB - 1) // FFN_SUB

    @pl.when(jnp.logical_and(j == 0, n > 0))
    def _():
        o_ref[...] = jnp.zeros_like(o_ref)

    rows_ref = h_scr[0] if wide_rows else h_ref
    for k in range(1, FFN_R // FFN_SUB + 1):
        m = k * FFN_SUB

        if wide_rows:
            @pl.when(jnp.logical_and(n_sub == k, j == 0))
            def _(m=m):
                rows_ref[0:m, :] = h_ref[0:m, :].astype(BF16)

        @pl.when(n_sub == k)
        def _(m=m):
            hc = rows_ref[0:m, :]
            g_scr[0:m, :] = jnp.dot(hc, w1_ref[0].astype(BF16), preferred_element_type=F32)
            u = jnp.dot(hc, w3_ref[0].astype(BF16), preferred_element_type=F32)
            a_scr[0:m, :] = (_silu(g_scr[0:m, :]) * u).astype(BF16)
            o_ref[0:m, :] += jnp.dot(a_scr[0:m, :], w2_ref[0].astype(BF16), preferred_element_type=F32)


def _ffn(hs, tile_expert, tile_rows, tile_block, w1, w3, w2):
    n_tiles = tile_expert.shape[0]
    nj = FF // FFN_TF

    def jj(s, j, tn):
        return jnp.where(tn[s] > 0, j, nj - 1)

    once = pl.Buffered(1)
    wide_rows = hs.dtype == F32
    scratch = [pltpu.VMEM((FFN_R, FFN_TF), F32), pltpu.VMEM((FFN_R, FFN_TF), BF16)]
    if wide_rows:
        scratch.append(pltpu.VMEM((FFN_R, D), BF16))
    return pl.pallas_call(
        functools.partial(_ffn_kernel, wide_rows=wide_rows),
        out_shape=jax.ShapeDtypeStruct((hs.shape[0], D), F32),
        grid_spec=pltpu.PrefetchScalarGridSpec(
            num_scalar_prefetch=3,
            grid=(n_tiles, nj),
            in_specs=[
                pl.BlockSpec((FFN_R, hs.shape[1]), lambda s, j, te, tn, tb: (tb[s], 0), pipeline_mode=once),
                pl.BlockSpec((1, D, FFN_TF), lambda s, j, te, tn, tb: (te[s], 0, jj(s, j, tn))),
                pl.BlockSpec((1, D, FFN_TF), lambda s, j, te, tn, tb: (te[s], 0, jj(s, j, tn))),
                pl.BlockSpec((1, FFN_TF, D), lambda s, j, te, tn, tb: (te[s], jj(s, j, tn), 0)),
            ],
            out_specs=pl.BlockSpec((FFN_R, D), lambda s, j, te, tn, tb: (tb[s], 0), pipeline_mode=once),
            scratch_shapes=scratch,
        ),
        compiler_params=_params(2),
        name="ffn",
    )(tile_expert, tile_rows, tile_block, hs, w1, w3, w2)


DMA_UNROLL = 8


def _dispatch_kernel(pos_ref, pad_ref, h_ref, o_hbm, zero_scr, sem, zsem):
    i = pl.program_id(0)

    @pl.when(i == 0)
    def _():
        zero_scr[...] = jnp.zeros_like(zero_scr)
        for e in range(NE):
            first, count = pad_ref[e], pad_ref[NE + e]

            def zero_copy(r, first=first):
                return pltpu.make_async_copy(zero_scr, o_hbm.at[pl.ds(first + r, 1), :], zsem)

            def start(r, carry):
                zero_copy(r).start()
                return carry

            def wait(r, carry):
                zero_copy(r).wait()
                return carry

            lax.fori_loop(0, count, start, 0)
            lax.fori_loop(0, count, wait, 0)

    def issue(r, carry):
        for slot in range(2):
            p = pos_ref[2 * (i * TILE + r) + slot]
            pltpu.make_async_copy(h_ref.at[pl.ds(r, 1), :], o_hbm.at[pl.ds(p, 1), :], sem).start()
        return carry

    lax.fori_loop(0, TILE, issue, 0, unroll=DMA_UNROLL)
    for slot in range(2):
        pltpu.make_async_copy(h_ref, o_hbm.at[pl.ds(0, TILE), :], sem).wait()


def _dispatch(pos, pad, h2):
    return pl.pallas_call(
        _dispatch_kernel,
        out_shape=jax.ShapeDtypeStruct((MOE_TILES * FFN_R, D), F32),
        grid_spec=pltpu.PrefetchScalarGridSpec(
            num_scalar_prefetch=2,
            grid=(NTILES,),
            in_specs=[pl.BlockSpec((TILE, D), lambda i, pos, pad: (i, 0))],
            out_specs=pl.BlockSpec(memory_space=pl.ANY),
            scratch_shapes=[pltpu.VMEM((1, D), F32), pltpu.SemaphoreType.DMA, pltpu.SemaphoreType.DMA],
        ),
        compiler_params=_params(1),
        name="dispatch",
    )(pos, pad, h2)


def _ln2_kernel(x_ref, f_ref, mod_ref, g_ref, b_ref, o_ref):
    m = mod_ref[0]
    o_ref[...] = _layer_norm(ALPHA * x_ref[...] + m[G2:G2 + 1, :] * f_ref[...], g_ref[...], b_ref[...])


def _ln2(x1, f, mod_l, ln_g, ln_b):
    return pl.pallas_call(
        _ln2_kernel,
        out_shape=jax.ShapeDtypeStruct((T, D), F32),
        grid=(NTILES,),
        in_specs=[
            pl.BlockSpec((TILE, D), lambda i: (i, 0)),
            pl.BlockSpec((TILE, D), lambda i: (i, 0)),
            pl.BlockSpec((1, 6, D), lambda i: (_mod_row(i * TILE), 0, 0)),
            pl.BlockSpec((1, D), lambda i: (0, 0)),
            pl.BlockSpec((1, D), lambda i: (0, 0)),
        ],
        out_specs=pl.BlockSpec((TILE, D), lambda i: (i, 0)),
        compiler_params=_params(1),
        name="ln2",
    )(x1, f, mod_l, ln_g.reshape(1, D), ln_b.reshape(1, D))


def _combine_kernel(pos_ref, x_ref, route_ref, mod_ref, g_ref, b_ref, y_hbm, op_ref, os_ref, buf, sem):
    i = pl.program_id(0)

    def gather(tile):
        def issue(r, carry):
            for slot in range(2):
                p = pos_ref[2 * (tile * TILE + r) + slot]
                pltpu.make_async_copy(y_hbm.at[pl.ds(p, 1), :], buf.at[tile % 2, slot, pl.ds(r, 1), :],
                                      sem.at[tile % 2]).start()
            return carry

        lax.fori_loop(0, TILE, issue, 0, unroll=DMA_UNROLL)

    @pl.when(i == 0)
    def _():
        gather(i)

    @pl.when(i + 1 < NTILES)
    def _():
        gather(i + 1)

    for slot in range(2):
        pltpu.make_async_copy(y_hbm.at[pl.ds(0, TILE), :], buf.at[i % 2, slot], sem.at[i % 2]).wait()
    route = route_ref[...]
    f = route[:, 2:3] * buf[i % 2, 0] + route[:, 3:4] * buf[i % 2, 1]
    m = mod_ref[0]
    out = _layer_norm(ALPHA * x_ref[...] + m[G2:G2 + 1, :] * f, g_ref[...], b_ref[...])

    @pl.when(i < NTILES_P)
    def _():
        op_ref[...] = out

    @pl.when(i >= NTILES_P)
    def _():
        os_ref[...] = out


def _combine(pos, x1, route, mod_l, ln_g, ln_b, y):
    return pl.pallas_call(
        _combine_kernel,
        out_shape=(jax.ShapeDtypeStruct((TP, D), F32), jax.ShapeDtypeStruct((T - TP, D), F32)),
        grid_spec=pltpu.PrefetchScalarGridSpec(
            num_scalar_prefetch=1,
            grid=(NTILES,),
            in_specs=[
                pl.BlockSpec((TILE, D), lambda i, pos: (i, 0)),
                pl.BlockSpec((TILE, 128), lambda i, pos: (i, 0)),
                pl.BlockSpec((1, 6, D), lambda i, pos: (_mod_row(i * TILE), 0, 0)),
                pl.BlockSpec((1, D), lambda i, pos: (0, 0)),
                pl.BlockSpec((1, D), lambda i, pos: (0, 0)),
                pl.BlockSpec(memory_space=pl.ANY),
            ],
            out_specs=(pl.BlockSpec((TILE, D), lambda i, pos: (jnp.minimum(i, NTILES_P - 1), 0)),
                       pl.BlockSpec((TILE, D), lambda i, pos: (jnp.maximum(i - NTILES_P, 0), 0))),
            scratch_shapes=[pltpu.VMEM((2, 2, TILE, D), F32), pltpu.SemaphoreType.DMA((2,))],
        ),
        compiler_params=_params(1),
        name="combine",
    )(pos, x1, route, mod_l, ln_g.reshape(1, D), ln_b.reshape(1, D), y)


MOE_TILES = 2 * T // FFN_R + NE


def _routing_tables(route, counts_f):
    eidx = route[:, 0:2].astype(jnp.int32)
    rank = route[:, 4:6].astype(jnp.int32)
    counts = counts_f[0, :NE].astype(jnp.int32)
    ntiles = (counts + FFN_R - 1) // FFN_R
    per_tile = (counts + jnp.maximum(ntiles, 1) - 1) // jnp.maximum(ntiles, 1)
    per_tile = (per_tile + FFN_SUB - 1) // FFN_SUB * FFN_SUB
    tend = jnp.cumsum(ntiles)
    tstart = tend - ntiles
    experts = jnp.arange(NE, dtype=jnp.int32)
    pick = lambda table: jnp.sum(jnp.where(eidx[..., None] == experts, table, 0), axis=-1)
    rows_e = jnp.maximum(pick(per_tile), 1)
    local = rank // rows_e
    pos = ((pick(tstart) + local) * FFN_R + (rank - local * rows_e)).reshape(-1)
    tiles = jnp.arange(MOE_TILES, dtype=jnp.int32)
    total = tend[-1]
    t_eff = jnp.minimum(tiles, total - 1)
    tile_expert = jnp.minimum(jnp.sum((t_eff[:, None] >= tend[None, :]).astype(jnp.int32), axis=1), NE - 1)
    tile_rows = jnp.clip(counts[tile_expert] - (t_eff - tstart[tile_expert]) * per_tile[tile_expert],
                         0, per_tile[tile_expert])
    tile_rows = jnp.where(tiles < total, tile_rows, 0)
    last_rows = counts - (ntiles - 1) * per_tile
    pad = jnp.concatenate([(tend - 1) * FFN_R + last_rows, (-last_rows) % FFN_SUB]).astype(jnp.int32)
    return pos, pad, tile_expert.astype(jnp.int32), tile_rows.astype(jnp.int32), t_eff


def _grid_pos_emb():
    rows = L_S // GRID_W
    r = np.repeat(np.arange(rows, dtype=np.float64), GRID_W)
    col = np.tile(np.arange(GRID_W, dtype=np.float64), rows)
    quarter = D // 4
    freqs = 1.0 / (10000.0 ** (np.arange(quarter, dtype=np.float64) / quarter))

    def enc(p):
        ang = p[:, None] * freqs[None, :]
        return np.concatenate([np.sin(ang), np.cos(ang)], -1)

    return jnp.asarray(np.concatenate([enc(r), enc(col)], -1), dtype=F32)


def _cumsum_matrices():
    i = jnp.arange(TILE)
    same = (i[:, None] // CHUNK) == (i[None, :] // CHUNK)
    fwd = jnp.logical_and(same, i[None, :] <= i[:, None])
    bwd = jnp.logical_and(same, i[None, :] >= i[:, None])
    return fwd.astype(BF16), bwd.astype(BF16)


def kernel(x_prompt, x_sample, state_gla, c, c_ctx, w_mod, b_mod, w_in, w_a2, b_a2, gla_norm_g, conv_w, conv_b, conv_ln_g, conv_ln_b, w_pw2, b_pw2, w_out, ln1_g, ln1_b, ln2_g, ln2_b, ffn_w1, ffn_w3, ffn_w2, moe_w_router, moe_b_router, moe_w1, moe_w3, moe_w2):
    x = _assemble(x_prompt.reshape(TP, D), x_sample.reshape(T - TP, D), _grid_pos_emb())
    cond8 = jnp.zeros((8, D), F32).at[0].set(c_ctx).at[1:1 + NB_S].set(c)
    mod = _modulation(cond8, w_mod, b_mod)
    tri = _cumsum_matrices()
    w_main = w_in[:, :, :MAIN_COLS].astype(BF16)
    w_prep = _wprep(w_in[:, :, MAIN_COLS:])
    dense_tiles = T // FFN_R
    dense_meta = (jnp.zeros((dense_tiles,), jnp.int32), jnp.full((dense_tiles,), FFN_R, jnp.int32),
                  jnp.arange(dense_tiles, dtype=jnp.int32))

    states = []
    for l in range(DEPTH):
        mod_l = mod[l]
        z, a_lr, u = _inproj(x, mod_l, w_main, w_prep, l)
        u2 = _conv_module(u, conv_w[l], conv_b[l], conv_ln_g[l], conv_ln_b[l],
                          w_pw2[l].astype(BF16), b_pw2[l])
        o_dir, s_dir = [], []
        for d in range(2):
            wa2 = jnp.zeros((2 * RANK, QC), F32).at[d * RANK:(d + 1) * RANK].set(w_a2[l, d]).astype(BF16)
            o, s_new = _gla(z, a_lr, wa2, b_a2[l, d].reshape(1, QC), tri[d], state_gla[:, l], d)
            o_dir.append(o)
            s_dir.append(s_new)
        states.append(jnp.stack(s_dir, axis=1))
        is_moe = l % 2 == 1
        i = l // 2
        router = None
        if is_moe:
            wr_hi = moe_w_router[i].astype(BF16)
            wr_lo = (moe_w_router[i] - wr_hi.astype(F32)).astype(BF16)
            router = (jnp.pad(jnp.concatenate([wr_hi, wr_lo], axis=1), ((0, 0), (0, 128 - 2 * NE))),
                      jnp.pad(moe_b_router[i], (0, 128 - NE)).reshape(1, 128))
        outs = _mixout(o_dir[0], o_dir[1], z, u2, x, mod_l, gla_norm_g[l], w_out[l].astype(BF16),
                       ln1_g[l], ln1_b[l], router)
        if is_moe:
            x1, h2, route, counts = outs
            pos, pad, t_exp, t_rows, t_blk = _routing_tables(route, counts)
            hs = _dispatch(pos, pad, h2)
            y = _ffn(hs, t_exp, t_rows, t_blk, moe_w1[i], moe_w3[i], moe_w2[i])
            out_p, out_s = _combine(pos, x1, route, mod_l, ln2_g[l], ln2_b[l], y)
            x = jnp.concatenate([out_p, out_s], axis=0) if l + 1 < DEPTH else None
        else:
            x1, h2 = outs
            y = _ffn(h2, *dense_meta, ffn_w1[i][None], ffn_w3[i][None], ffn_w2[i][None])
            x = _ln2(x1, y, mod_l, ln2_g[l], ln2_b[l])

    if x is not None:
        out_p, out_s = x[:TP], x[TP:]
    return out_p.reshape(NB_P, L_P, D), out_s.reshape(NB_S, L_S, D), jnp.stack(states, axis=1)
```

```python
import functools

import jax
import jax.numpy as jnp
import numpy as np
from jax import lax
from jax.experimental import pallas as pl
from jax.experimental.pallas import tpu as pltpu

F32 = jnp.float32
BF16 = jnp.bfloat16

D = 2048
NB_P, L_P = 16, 256
NB_S, L_S = 4, 1024
TP = NB_P * L_P
T = TP + NB_S * L_S
DEPTH = 2
GRID_W = 64
GW = D // 2
CW = D - GW
H = 4
DV = GW // H
DK = DV // 2
QC = H * DK
RANK = 16
TAU = 16.0
CHUNK = 64
CONV_K = 31
FF = 7 * D // 2
NE = 8
ALPHA = (2 * DEPTH) ** 0.25
LN_EPS = 1e-5
RMS_EPS = 1e-6
MAIN_COLS = 2 * QC + 2 * GW

TILE = 256
NTILES = T // TILE
NTILES_P = TP // TILE
TILES_PER_S = L_S // TILE
HALO = 16

VMEM_LIMIT = 56 * 1024 * 1024

SH1, SC1, G1, SH2, SC2, G2 = range(6)


def _params(n_axes, vmem=VMEM_LIMIT):
    return pltpu.CompilerParams(dimension_semantics=("arbitrary",) * n_axes,
                                vmem_limit_bytes=vmem)


def _mod_row(tok0):
    return jnp.where(tok0 < TP, 0, 1 + (tok0 - TP) // L_S)


def _sigmoid(x):
    return 1.0 / (1.0 + jnp.exp(-x))


def _silu(x):
    return x * _sigmoid(x)


def _layer_norm(r, g, b):
    mu = jnp.mean(r, -1, keepdims=True)
    rc = r - mu
    var = jnp.mean(rc * rc, -1, keepdims=True)
    return rc * lax.rsqrt(var + LN_EPS) * g + b


def _assemble_kernel(xp_ref, xs_ref, pos_ref, o_ref):
    i = pl.program_id(0)

    @pl.when(i < NTILES_P)
    def _():
        o_ref[...] = xp_ref[...]

    @pl.when(i >= NTILES_P)
    def _():
        o_ref[...] = xs_ref[...] + pos_ref[...]


def _assemble(xp, xs, pos):
    return pl.pallas_call(
        _assemble_kernel,
        out_shape=jax.ShapeDtypeStruct((T, D), F32),
        grid=(NTILES,),
        in_specs=[
            pl.BlockSpec((TILE, D), lambda i: (jnp.minimum(i, NTILES_P - 1), 0)),
            pl.BlockSpec((TILE, D), lambda i: (jnp.maximum(i - NTILES_P, 0), 0)),
            pl.BlockSpec((TILE, D), lambda i: (jnp.maximum(i - NTILES_P, 0) % TILES_PER_S, 0)),
        ],
        out_specs=pl.BlockSpec((TILE, D), lambda i: (i, 0)),
        compiler_params=_params(1),
        name="assemble",
    )(xp, xs, pos)


MOD_TN = 1024


def _mod_kernel(c_ref, w_ref, b_ref, o_ref):
    s = _silu(c_ref[...])
    s_hi = s.astype(BF16)
    s_lo = (s - s_hi.astype(F32)).astype(BF16)
    w = w_ref[0]
    w_hi = w.astype(BF16)
    w_lo = (w - w_hi.astype(F32)).astype(BF16)
    o_ref[0] = (jnp.dot(s_hi, w_hi, preferred_element_type=F32)
                + jnp.dot(s_hi, w_lo, preferred_element_type=F32)
                + jnp.dot(s_lo, w_hi, preferred_element_type=F32)) + b_ref[0]


def _modulation(cond8, w_mod, b_mod):
    out = pl.pallas_call(
        _mod_kernel,
        out_shape=jax.ShapeDtypeStruct((DEPTH, 8, 6 * D), F32),
        grid=(DEPTH, 6 * D // MOD_TN),
        in_specs=[
            pl.BlockSpec((8, D), lambda l, j: (0, 0)),
            pl.BlockSpec((1, D, MOD_TN), lambda l, j: (l, 0, j)),
            pl.BlockSpec((1, 1, MOD_TN), lambda l, j: (l, 0, j)),
        ],
        out_specs=pl.BlockSpec((1, 8, MOD_TN), lambda l, j: (l, 0, j)),
        compiler_params=_params(2),
        name="modulation",
    )(cond8, w_mod, b_mod.reshape(DEPTH, 1, 6 * D))
    return out.reshape(DEPTH, 8, 6, D)


PROJ_TM = 1024
PROJ_TN = 1024
PROJ_NMAIN = MAIN_COLS // PROJ_TN
GLU_TN = 512
PROJ_NGLU = CW // GLU_TN

def _modulated(x_ref, mod_ref, shift, scale):
    m = mod_ref[0]
    return x_ref[...] * (1.0 + m[scale:scale + 1, :]) + m[shift:shift + 1, :]


def _inproj_kernel(x_ref, mod_ref, w_ref, wa_ref, wua_ref, wug_ref, z_ref, a_ref, u_ref, h_scr):
    j = pl.program_id(1)

    @pl.when(j == 0)
    def _():
        hb = _modulated(x_ref, mod_ref, SH1, SC1).astype(BF16)
        h_scr[...] = hb
        a_ref[...] = jnp.dot(hb, wa_ref[0], preferred_element_type=F32)

    @pl.when(j < PROJ_NMAIN)
    def _():
        z_ref[...] = jnp.dot(h_scr[...], w_ref[0], preferred_element_type=F32)

    @pl.when(j >= PROJ_NMAIN)
    def _():
        h = h_scr[...]
        a = jnp.dot(h, wua_ref[0], preferred_element_type=F32)
        g = jnp.dot(h, wug_ref[0], preferred_element_type=F32)
        u_ref[...] = a * _sigmoid(g)


def _inproj(x, mod_l, w_main, w_prep, layer):
    main_j = lambda j: jnp.minimum(j, PROJ_NMAIN - 1)
    glu_j = lambda j: jnp.maximum(j - PROJ_NMAIN, 0)
    val0 = 0
    gate0 = CW // GLU_TN
    low_rank = 2 * CW // 128
    return pl.pallas_call(
        _inproj_kernel,
        out_shape=(jax.ShapeDtypeStruct((T, MAIN_COLS), F32),
                   jax.ShapeDtypeStruct((T, 128), F32),
                   jax.ShapeDtypeStruct((T, CW), F32)),
        grid=(T // PROJ_TM, PROJ_NMAIN + PROJ_NGLU),
        in_specs=[
            pl.BlockSpec((PROJ_TM, D), lambda i, j: (i, 0)),
            pl.BlockSpec((1, 6, D), lambda i, j: (_mod_row(i * PROJ_TM), 0, 0)),
            pl.BlockSpec((1, D, PROJ_TN), lambda i, j: (layer, 0, main_j(j))),
            pl.BlockSpec((1, D, 128), lambda i, j: (layer, 0, low_rank)),
            pl.BlockSpec((1, D, GLU_TN), lambda i, j: (layer, 0, val0 + glu_j(j))),
            pl.BlockSpec((1, D, GLU_TN), lambda i, j: (layer, 0, gate0 + glu_j(j))),
        ],
        out_specs=(pl.BlockSpec((PROJ_TM, PROJ_TN), lambda i, j: (i, main_j(j))),
                   pl.BlockSpec((PROJ_TM, 128), lambda i, j: (i, 0)),
                   pl.BlockSpec((PROJ_TM, GLU_TN), lambda i, j: (i, glu_j(j)))),
        scratch_shapes=[pltpu.VMEM((PROJ_TM, D), BF16)],
        compiler_params=_params(2),
        name="inproj",
    )(x, mod_l, w_main, w_prep, w_prep, w_prep)


CONV_RC = 32
CONV_CC = 256


def _conv_kernel(uc_ref, up_ref, un_ref, cw_ref, cb_ref, lg_ref, lb_ref, wp_ref, bp_ref,
                 o_ref, pad_scr, conv_scr):
    i = pl.program_id(0)
    s = jnp.maximum(i - NTILES_P, 0) % TILES_PER_S
    has_prev = jnp.logical_and(i >= NTILES_P, s != 0)
    has_next = jnp.logical_and(i >= NTILES_P, s != TILES_PER_S - 1)
    pad_scr[0:HALO, :] = jnp.where(has_prev, up_ref[...], 0.0)
    pad_scr[HALO:HALO + TILE, :] = uc_ref[...]
    pad_scr[HALO + TILE:HALO + TILE + HALO, :] = jnp.where(has_next, un_ref[...], 0.0)

    off = HALO - CONV_K // 2
    for c in range(CW // CONV_CC):
        cs = slice(c * CONV_CC, (c + 1) * CONV_CC)

        def body(r, carry, cs=cs):
            r0 = pl.multiple_of(r * CONV_RC, CONV_RC)
            win = pad_scr[pl.ds(r0, 2 * CONV_RC), cs]
            acc = None
            for b in range(8):
                part = None
                for a in range((CONV_K + off) // 8 + 1):
                    k = 8 * a + b - off
                    if 0 <= k < CONV_K:
                        term = cw_ref[k:k + 1, cs] * win[8 * a:8 * a + CONV_RC + 8, :]
                        part = term if part is None else part + term
                part = part[b:b + CONV_RC, :]
                acc = part if acc is None else acc + part
            conv_scr[pl.ds(r0, CONV_RC), cs] = acc
            return carry

        lax.fori_loop(0, TILE // CONV_RC, body, 0)

    v = conv_scr[...] + cb_ref[...]
    y = _silu(_layer_norm(v, lg_ref[...], lb_ref[...]))
    o_ref[...] = (jnp.dot(y.astype(BF16), wp_ref[...], preferred_element_type=F32) + bp_ref[...]).astype(BF16)


def _conv_module(u, conv_w, conv_b, ln_g, ln_b, w_pw2, b_pw2):
    hb = TILE // HALO
    row = lambda a: a.reshape(1, CW)
    return pl.pallas_call(
        _conv_kernel,
        out_shape=jax.ShapeDtypeStruct((T, CW), BF16),
        grid=(NTILES,),
        in_specs=[
            pl.BlockSpec((TILE, CW), lambda i: (i, 0)),
            pl.BlockSpec((HALO, CW), lambda i: (jnp.maximum(i * hb - 1, 0), 0)),
            pl.BlockSpec((HALO, CW), lambda i: (jnp.minimum((i + 1) * hb, T // HALO - 1), 0)),
            pl.BlockSpec((CONV_K, CW), lambda i: (0, 0)),
            pl.BlockSpec((1, CW), lambda i: (0, 0)),
            pl.BlockSpec((1, CW), lambda i: (0, 0)),
            pl.BlockSpec((1, CW), lambda i: (0, 0)),
            pl.BlockSpec((CW, CW), lambda i: (0, 0)),
            pl.BlockSpec((1, CW), lambda i: (0, 0)),
        ],
        out_specs=pl.BlockSpec((TILE, CW), lambda i: (i, 0)),
        scratch_shapes=[pltpu.VMEM((TILE + 2 * HALO, CW), F32), pltpu.VMEM((TILE, CW), F32)],
        compiler_params=_params(1),
        name="conv_module",
    )(u, u, u, conv_w, row(conv_b), row(ln_g), row(ln_b), w_pw2, row(b_pw2))


def _split3(x):
    hi = x.astype(BF16)
    r1 = x - hi.astype(F32)
    mid = r1.astype(BF16)
    lo = (r1 - mid.astype(F32)).astype(BF16)
    return hi, mid, lo


def _log_sigmoid(x):
    return jnp.minimum(x, 0.0) - jnp.log(1.0 + jnp.exp(-jnp.abs(x)))


def _dot_nt(a, b):
    return lax.dot_general(a, b, (((1,), (1,)), ((), ())), preferred_element_type=F32)


def _dot_tn(a, b):
    return lax.dot_general(a, b, (((0,), (0,)), ((), ())), preferred_element_type=F32)


def _gla_kernel(q_ref, k_ref, v_ref, a_ref, wa2_ref, ba2_ref, tri_ref, s0_ref,
                o_ref, snew_ref, st_scr, *, reverse):
    n = pl.program_id(0)
    tile = (NTILES - 1 - n) if reverse else n
    is_prompt = tile < NTILES_P
    spos = jnp.maximum(tile - NTILES_P, 0) % TILES_PER_S
    seq_first = spos == (TILES_PER_S - 1 if reverse else 0)

    @pl.when(is_prompt)
    def _():
        st_scr[...] = jnp.zeros_like(st_scr)

    @pl.when(jnp.logical_and(jnp.logical_not(is_prompt), seq_first))
    def _():
        for h in range(H):
            st_scr[h] = s0_ref[0, 0, h].T

    tri = tri_ref[...]
    mask = tri > 0
    a_lr = a_ref[:, 0:2 * RANK].astype(BF16)
    n_chunks = TILE // CHUNK
    order = range(n_chunks - 1, -1, -1) if reverse else range(n_chunks)
    logits = jnp.dot(a_lr, wa2_ref[...], preferred_element_type=F32) + ba2_ref[...]
    hi, mid, lo = _split3(_log_sigmoid(logits) / TAU)
    b_all = (jnp.dot(tri, hi, preferred_element_type=F32)
             + jnp.dot(tri, mid, preferred_element_type=F32)
             + jnp.dot(tri, lo, preferred_element_type=F32))
    for h in range(H):
        ks = slice(h * DK, (h + 1) * DK)
        b = b_all[:, ks]
        q = q_ref[:, ks] * (DK ** -0.5)
        k = k_ref[:, ks]
        vb = v_ref[:, h * DV:(h + 1) * DV].astype(BF16)
        qd = (q * jnp.exp(b)).astype(BF16)
        kd = (k * jnp.exp(-b)).astype(BF16)
        att = jnp.where(mask, _dot_nt(qd, kd), 0.0).astype(BF16)
        o_intra = jnp.dot(att, vb, preferred_element_type=F32)
        st = st_scr[h]
        for c in order:
            rows = slice(c * CHUNK, (c + 1) * CHUNK)
            last = c * CHUNK if reverse else (c + 1) * CHUNK - 1
            b_last = b[last:last + 1, :]
            kl = (k[rows] * jnp.exp(b_last - b[rows])).astype(BF16)
            o_ref[rows, h * DV:(h + 1) * DV] = o_intra[rows] + _dot_nt(qd[rows], st.astype(BF16))
            st = jnp.exp(b_last) * st + _dot_tn(vb[rows], kl)
        st_scr[h] = st

    @pl.when(is_prompt)
    def _():
        for h in range(H):
            snew_ref[0, h] = st_scr[h].T


def _gla(z, a_lr, wa2_dir, ba2_dir, tri_dir, s0, direction):
    reverse = direction == 1
    tile = (lambda n: NTILES - 1 - n) if reverse else (lambda n: n)
    req = lambda n: jnp.clip((tile(n) - NTILES_P) // TILES_PER_S, 0, NB_S - 1)
    return pl.pallas_call(
        functools.partial(_gla_kernel, reverse=reverse),
        out_shape=(jax.ShapeDtypeStruct((T, GW), F32),
                   jax.ShapeDtypeStruct((NB_P, H, DK, DV), F32)),
        grid=(NTILES,),
        in_specs=[
            pl.BlockSpec((TILE, QC), lambda n: (tile(n), 0)),
            pl.BlockSpec((TILE, QC), lambda n: (tile(n), 1)),
            pl.BlockSpec((TILE, GW), lambda n: (tile(n), 1)),
            pl.BlockSpec((TILE, 128), lambda n: (tile(n), 0)),
            pl.BlockSpec((2 * RANK, QC), lambda n: (0, 0)),
            pl.BlockSpec((1, QC), lambda n: (0, 0)),
            pl.BlockSpec((TILE, TILE), lambda n: (0, 0)),
            pl.BlockSpec((1, 1, H, DK, DV), lambda n: (req(n), direction, 0, 0, 0)),
        ],
        out_specs=(pl.BlockSpec((TILE, GW), lambda n: (tile(n), 0)),
                   pl.BlockSpec((1, H, DK, DV), lambda n: (jnp.minimum(tile(n), NTILES_P - 1), 0, 0, 0))),
        scratch_shapes=[pltpu.VMEM((H, DV, DK), F32)],
        compiler_params=_params(1),
        name="gla_bwd" if reverse else "gla_fwd",
    )(z, z, z, a_lr, wa2_dir, ba2_dir, tri_dir, s0)


MIX_GROUPS = 1


def _mixout_kernel(of_ref, ob_ref, g_ref, u_ref, x_ref, mod_ref, gng_ref, wo_ref, l1g_ref, l1b_ref,
                   *rest, with_router):
    if with_router:
        wr_ref, br_ref, ltri_ref, x1_ref, h2_ref, route_ref, cnt_ref, run_scr = rest
    else:
        x1_ref, h2_ref = rest
    m = mod_ref[0]
    h2_parts = []
    for p in range(MIX_GROUPS):
        rs = slice(p * (TILE // MIX_GROUPS), (p + 1) * (TILE // MIX_GROUPS))
        o = of_ref[rs, :] + ob_ref[rs, :]
        parts = []
        for h in range(H):
            oh = o[:, h * DV:(h + 1) * DV]
            ms = jnp.mean(oh * oh, -1, keepdims=True)
            parts.append(oh * lax.rsqrt(ms + RMS_EPS) * gng_ref[...])
        on = jnp.concatenate(parts, axis=-1) * _silu(g_ref[rs, :])
        y = (jnp.dot(on.astype(BF16), wo_ref[0:GW, :], preferred_element_type=F32)
             + jnp.dot(u_ref[rs, :], wo_ref[GW:D, :], preferred_element_type=F32))
        x1 = _layer_norm(ALPHA * x_ref[rs, :] + m[G1:G1 + 1, :] * y, l1g_ref[...], l1b_ref[...])
        x1_ref[rs, :] = x1
        h2_part = x1 * (1.0 + m[SC2:SC2 + 1, :]) + m[SH2:SH2 + 1, :]
        h2_ref[rs, :] = h2_part.astype(h2_ref.dtype)
        h2_parts.append(h2_part)
    if with_router:
        h2 = jnp.concatenate(h2_parts, axis=0)
        h_hi = h2.astype(BF16)
        h_lo = (h2 - h_hi.astype(F32)).astype(BF16)
        p_hi = jnp.dot(h_hi, wr_ref[...], preferred_element_type=F32)
        p_lo = jnp.dot(h_lo, wr_ref[...], preferred_element_type=F32)
        logits = p_hi + pltpu.roll(p_hi, 128 - NE, 1) + p_lo + br_ref[...]
        lane = lax.broadcasted_iota(jnp.int32, logits.shape, 1)
        lg = jnp.where(lane < NE, logits, -jnp.inf)
        v1 = jnp.max(lg, -1, keepdims=True)
        i1 = jnp.min(jnp.where(lg == v1, lane, 128), -1, keepdims=True)
        lg2 = jnp.where(lane == i1, -jnp.inf, lg)
        v2 = jnp.max(lg2, -1, keepdims=True)
        i2 = jnp.min(jnp.where(lg2 == v2, lane, 128), -1, keepdims=True)
        e2 = jnp.exp(v2 - v1)
        w1 = 1.0 / (1.0 + e2)
        w2 = e2 / (1.0 + e2)
        @pl.when(pl.program_id(0) == 0)
        def _():
            run_scr[...] = jnp.zeros_like(run_scr)

        hit1 = lane == i1
        hit2 = lane == i2
        one1 = jnp.where(hit1, 1.0, 0.0)
        one2 = jnp.where(hit2, 1.0, 0.0)
        before1 = jnp.dot(ltri_ref[...], one1.astype(BF16), preferred_element_type=F32)
        before2 = jnp.dot(ltri_ref[...], one2.astype(BF16), preferred_element_type=F32)
        tot1 = jnp.sum(one1, axis=0, keepdims=True)
        tot2 = jnp.sum(one2, axis=0, keepdims=True)
        run = run_scr[...]
        rank1 = jnp.sum(jnp.where(hit1, before1 + run, 0.0), -1, keepdims=True)
        rank2 = jnp.sum(jnp.where(hit2, before2 + (run + tot1), 0.0), -1, keepdims=True)
        run = run + tot1 + tot2
        run_scr[...] = run
        cnt_ref[...] = jnp.broadcast_to(run, cnt_ref.shape)
        route_ref[...] = jnp.where(lane == 0, i1.astype(F32),
                         jnp.where(lane == 1, i2.astype(F32),
                         jnp.where(lane == 2, w1,
                         jnp.where(lane == 3, w2,
                         jnp.where(lane == 4, rank1, jnp.where(lane == 5, rank2, 0.0))))))


def _mixout(o_f, o_b, z, u2, x, mod_l, gng, w_out, ln_g, ln_b, router=None):
    row = lambda a: a.reshape(1, -1)
    full = lambda shape: pl.BlockSpec(shape, lambda i: (0,) * len(shape))
    in_specs = [
        pl.BlockSpec((TILE, GW), lambda i: (i, 0)),
        pl.BlockSpec((TILE, GW), lambda i: (i, 0)),
        pl.BlockSpec((TILE, GW), lambda i: (i, 2)),
        pl.BlockSpec((TILE, CW), lambda i: (i, 0)),
        pl.BlockSpec((TILE, D), lambda i: (i, 0)),
        pl.BlockSpec((1, 6, D), lambda i: (_mod_row(i * TILE), 0, 0)),
        full((1, DV)), full((D, D)), full((1, D)), full((1, D)),
    ]
    args = [o_f, o_b, z, u2, x, mod_l, row(gng), w_out, row(ln_g), row(ln_b)]
    out_shape = [jax.ShapeDtypeStruct((T, D), F32)]
    out_specs = [pl.BlockSpec((TILE, D), lambda i: (i, 0)), pl.BlockSpec((TILE, D), lambda i: (i, 0))]
    scratch = []
    if router is None:
        out_shape.append(jax.ShapeDtypeStruct((T, D), BF16))
    else:
        w_r, b_r = router
        idx = jnp.arange(TILE)
        ltri = (idx[None, :] < idx[:, None]).astype(BF16)
        in_specs += [full((D, 128)), full((1, 128)), full((TILE, TILE))]
        args += [w_r, b_r, ltri]
        out_shape += [jax.ShapeDtypeStruct((T, D), F32),
                      jax.ShapeDtypeStruct((T, 128), F32), jax.ShapeDtypeStruct((8, 128), F32)]
        out_specs += [pl.BlockSpec((TILE, 128), lambda i: (i, 0)), pl.BlockSpec((8, 128), lambda i: (0, 0))]
        scratch = [pltpu.VMEM((1, 128), F32)]
    return pl.pallas_call(
        functools.partial(_mixout_kernel, with_router=router is not None),
        out_shape=tuple(out_shape),
        grid=(NTILES,),
        in_specs=in_specs,
        out_specs=tuple(out_specs),
        scratch_shapes=scratch,
        compiler_params=_params(1),
        name="mixout",
    )(*args)


FFN_R = 1024
FFN_SUB = 256
FFN_TF = 512


def _ffn_kernel(te_ref, tn_ref, tb_ref, h_ref, w1_ref, w3_ref, w2_ref, o_ref, g_scr, a_scr, *h_scr, wide_rows):
    s = pl.program_id(0)
    j = pl.program_id(1)
    n = tn_ref[s]
    n_sub = (n + FFN_SUB - 1) // FFN_SUB

    @pl.when(jnp.logical_and(j == 0, n > 0))
    def _():
        o_ref[...] = jnp.zeros_like(o_ref)

    rows_ref = h_scr[0] if wide_rows else h_ref
    for k in range(1, FFN_R // FFN_SUB + 1):
        m = k * FFN_SUB

        if wide_rows:
            @pl.when(jnp.logical_and(n_sub == k, j == 0))
            def _(m=m):
                rows_ref[0:m, :] = h_ref[0:m, :].astype(BF16)

        @pl.when(n_sub == k)
        def _(m=m):
            hc = rows_ref[0:m, :]
            g_scr[0:m, :] = jnp.dot(hc, w1_ref[0].astype(BF16), preferred_element_type=F32)
            u = jnp.dot(hc, w3_ref[0].astype(BF16), preferred_element_type=F32)
            a_scr[0:m, :] = (_silu(g_scr[0:m, :]) * u).astype(BF16)
            o_ref[0:m, :] += jnp.dot(a_scr[0:m, :], w2_ref[0].astype(BF16), preferred_element_type=F32)


def _ffn(hs, tile_expert, tile_rows, tile_block, w1, w3, w2):
    n_tiles = tile_expert.shape[0]
    nj = FF // FFN_TF

    def jj(s, j, tn):
        return jnp.where(tn[s] > 0, j, nj - 1)

    once = pl.Buffered(1)
    wide_rows = hs.dtype == F32
    scratch = [pltpu.VMEM((FFN_R, FFN_TF), F32), pltpu.VMEM((FFN_R, FFN_TF), BF16)]
    if wide_rows:
        scratch.append(pltpu.VMEM((FFN_R, D), BF16))
    return pl.pallas_call(
        functools.partial(_ffn_kernel, wide_rows=wide_rows),
        out_shape=jax.ShapeDtypeStruct((hs.shape[0], D), F32),
        grid_spec=pltpu.PrefetchScalarGridSpec(
            num_scalar_prefetch=3,
            grid=(n_tiles, nj),
            in_specs=[
                pl.BlockSpec((FFN_R, hs.shape[1]), lambda s, j, te, tn, tb: (tb[s], 0), pipeline_mode=once),
                pl.BlockSpec((1, D, FFN_TF), lambda s, j, te, tn, tb: (te[s], 0, jj(s, j, tn))),
                pl.BlockSpec((1, D, FFN_TF), lambda s, j, te, tn, tb: (te[s], 0, jj(s, j, tn))),
                pl.BlockSpec((1, FFN_TF, D), lambda s, j, te, tn, tb: (te[s], jj(s, j, tn), 0)),
            ],
            out_specs=pl.BlockSpec((FFN_R, D), lambda s, j, te, tn, tb: (tb[s], 0), pipeline_mode=once),
            scratch_shapes=scratch,
        ),
        compiler_params=_params(2),
        name="ffn",
    )(tile_expert, tile_rows, tile_block, hs, w1, w3, w2)


DMA_UNROLL = 8


def _dispatch_kernel(pos_ref, pad_ref, h_ref, o_hbm, zero_scr, sem, zsem):
    i = pl.program_id(0)

    @pl.when(i == 0)
    def _():
        zero_scr[...] = jnp.zeros_like(zero_scr)
        for e in range(NE):
            first, count = pad_ref[e], pad_ref[NE + e]

            def zero_copy(r, first=first):
                return pltpu.make_async_copy(zero_scr, o_hbm.at[pl.ds(first + r, 1), :], zsem)

            def start(r, carry):
                zero_copy(r).start()
                return carry

            def wait(r, carry):
                zero_copy(r).wait()
                return carry

            lax.fori_loop(0, count, start, 0)
            lax.fori_loop(0, count, wait, 0)

    def issue(r, carry):
        for slot in range(2):
            p = pos_ref[2 * (i * TILE + r) + slot]
            pltpu.make_async_copy(h_ref.at[pl.ds(r, 1), :], o_hbm.at[pl.ds(p, 1), :], sem).start()
        return carry

    lax.fori_loop(0, TILE, issue, 0, unroll=DMA_UNROLL)
    for slot in range(2):
        pltpu.make_async_copy(h_ref, o_hbm.at[pl.ds(0, TILE), :], sem).wait()


def _dispatch(pos, pad, h2):
    return pl.pallas_call(
        _dispatch_kernel,
        out_shape=jax.ShapeDtypeStruct((MOE_TILES * FFN_R, D), F32),
        grid_spec=pltpu.PrefetchScalarGridSpec(
            num_scalar_prefetch=2,
            grid=(NTILES,),
            in_specs=[pl.BlockSpec((TILE, D), lambda i, pos, pad: (i, 0))],
            out_specs=pl.BlockSpec(memory_space=pl.ANY),
            scratch_shapes=[pltpu.VMEM((1, D), F32), pltpu.SemaphoreType.DMA, pltpu.SemaphoreType.DMA],
        ),
        compiler_params=_params(1),
        name="dispatch",
    )(pos, pad, h2)


def _ln2_kernel(x_ref, f_ref, mod_ref, g_ref, b_ref, o_ref):
    m = mod_ref[0]
    o_ref[...] = _layer_norm(ALPHA * x_ref[...] + m[G2:G2 + 1, :] * f_ref[...], g_ref[...], b_ref[...])


def _ln2(x1, f, mod_l, ln_g, ln_b):
    return pl.pallas_call(
        _ln2_kernel,
        out_shape=jax.ShapeDtypeStruct((T, D), F32),
        grid=(NTILES,),
        in_specs=[
            pl.BlockSpec((TILE, D), lambda i: (i, 0)),
            pl.BlockSpec((TILE, D), lambda i: (i, 0)),
            pl.BlockSpec((1, 6, D), lambda i: (_mod_row(i * TILE), 0, 0)),
            pl.BlockSpec((1, D), lambda i: (0, 0)),
            pl.BlockSpec((1, D), lambda i: (0, 0)),
        ],
        out_specs=pl.BlockSpec((TILE, D), lambda i: (i, 0)),
        compiler_params=_params(1),
        name="ln2",
    )(x1, f, mod_l, ln_g.reshape(1, D), ln_b.reshape(1, D))


def _combine_kernel(pos_ref, x_ref, route_ref, mod_ref, g_ref, b_ref, y_hbm, op_ref, os_ref, buf, sem):
    i = pl.program_id(0)

    def gather(tile):
        def issue(r, carry):
            for slot in range(2):
                p = pos_ref[2 * (tile * TILE + r) + slot]
                pltpu.make_async_copy(y_hbm.at[pl.ds(p, 1), :], buf.at[tile % 2, slot, pl.ds(r, 1), :],
                                      sem.at[tile % 2]).start()
            return carry

        lax.fori_loop(0, TILE, issue, 0, unroll=DMA_UNROLL)

    @pl.when(i == 0)
    def _():
        gather(i)

    @pl.when(i + 1 < NTILES)
    def _():
        gather(i + 1)

    for slot in range(2):
        pltpu.make_async_copy(y_hbm.at[pl.ds(0, TILE), :], buf.at[i % 2, slot], sem.at[i % 2]).wait()
    route = route_ref[...]
    f = route[:, 2:3] * buf[i % 2, 0] + route[:, 3:4] * buf[i % 2, 1]
    m = mod_ref[0]
    out = _layer_norm(ALPHA * x_ref[...] + m[G2:G2 + 1, :] * f, g_ref[...], b_ref[...])

    @pl.when(i < NTILES_P)
    def _():
        op_ref[...] = out

    @pl.when(i >= NTILES_P)
    def _():
        os_ref[...] = out


def _combine(pos, x1, route, mod_l, ln_g, ln_b, y):
    return pl.pallas_call(
        _combine_kernel,
        out_shape=(jax.ShapeDtypeStruct((TP, D), F32), jax.ShapeDtypeStruct((T - TP, D), F32)),
        grid_spec=pltpu.PrefetchScalarGridSpec(
            num_scalar_prefetch=1,
            grid=(NTILES,),
            in_specs=[
                pl.BlockSpec((TILE, D), lambda i, pos: (i, 0)),
                pl.BlockSpec((TILE, 128), lambda i, pos: (i, 0)),
                pl.BlockSpec((1, 6, D), lambda i, pos: (_mod_row(i * TILE), 0, 0)),
                pl.BlockSpec((1, D), lambda i, pos: (0, 0)),
                pl.BlockSpec((1, D), lambda i, pos: (0, 0)),
                pl.BlockSpec(memory_space=pl.ANY),
            ],
            out_specs=(pl.BlockSpec((TILE, D), lambda i, pos: (jnp.minimum(i, NTILES_P - 1), 0)),
                       pl.BlockSpec((TILE, D), lambda i, pos: (jnp.maximum(i - NTILES_P, 0), 0))),
            scratch_shapes=[pltpu.VMEM((2, 2, TILE, D), F32), pltpu.SemaphoreType.DMA((2,))],
        ),
        compiler_params=_params(1),
        name="combine",
    )(pos, x1, route, mod_l, ln_g.reshape(1, D), ln_b.reshape(1, D), y)


MOE_TILES = 2 * T // FFN_R + NE


def _routing_tables(route, counts_f):
    eidx = route[:, 0:2].astype(jnp.int32)
    rank = route[:, 4:6].astype(jnp.int32)
    counts = counts_f[0, :NE].astype(jnp.int32)
    ntiles = (counts + FFN_R - 1) // FFN_R
    per_tile = (counts + jnp.maximum(ntiles, 1) - 1) // jnp.maximum(ntiles, 1)
    per_tile = (per_tile + FFN_SUB - 1) // FFN_SUB * FFN_SUB
    tend = jnp.cumsum(ntiles)
    tstart = tend - ntiles
    experts = jnp.arange(NE, dtype=jnp.int32)
    pick = lambda table: jnp.sum(jnp.where(eidx[..., None] == experts, table, 0), axis=-1)
    rows_e = pick(per_tile)
    local = sum((rank >= k * rows_e).astype(jnp.int32) for k in range(1, T // FFN_R))
    pos = ((pick(tstart) + local) * FFN_R + (rank - local * rows_e)).reshape(-1)
    tiles = jnp.arange(MOE_TILES, dtype=jnp.int32)
    total = tend[-1]
    t_eff = jnp.minimum(tiles, total - 1)
    tile_expert = jnp.minimum(jnp.sum((t_eff[:, None] >= tend[None, :]).astype(jnp.int32), axis=1), NE - 1)
    tile_rows = jnp.clip(counts[tile_expert] - (t_eff - tstart[tile_expert]) * per_tile[tile_expert],
                         0, per_tile[tile_expert])
    tile_rows = jnp.where(tiles < total, tile_rows, 0)
    last_rows = counts - (ntiles - 1) * per_tile
    pad = jnp.concatenate([(tend - 1) * FFN_R + last_rows, (-last_rows) % FFN_SUB]).astype(jnp.int32)
    return pos, pad, tile_expert.astype(jnp.int32), tile_rows.astype(jnp.int32), t_eff


def _grid_pos_emb():
    rows = L_S // GRID_W
    r = np.repeat(np.arange(rows, dtype=np.float64), GRID_W)
    col = np.tile(np.arange(GRID_W, dtype=np.float64), rows)
    quarter = D // 4
    freqs = 1.0 / (10000.0 ** (np.arange(quarter, dtype=np.float64) / quarter))

    def enc(p):
        ang = p[:, None] * freqs[None, :]
        return np.concatenate([np.sin(ang), np.cos(ang)], -1)

    return jnp.asarray(np.concatenate([enc(r), enc(col)], -1), dtype=F32)


def _cumsum_matrices():
    i = jnp.arange(TILE)
    same = (i[:, None] // CHUNK) == (i[None, :] // CHUNK)
    fwd = jnp.logical_and(same, i[None, :] <= i[:, None])
    bwd = jnp.logical_and(same, i[None, :] >= i[:, None])
    return fwd.astype(BF16), bwd.astype(BF16)


def kernel(x_prompt, x_sample, state_gla, c, c_ctx, w_mod, b_mod, w_in, w_a2, b_a2, gla_norm_g, conv_w, conv_b, conv_ln_g, conv_ln_b, w_pw2, b_pw2, w_out, ln1_g, ln1_b, ln2_g, ln2_b, ffn_w1, ffn_w3, ffn_w2, moe_w_router, moe_b_router, moe_w1, moe_w3, moe_w2):
    x = _assemble(x_prompt.reshape(TP, D), x_sample.reshape(T - TP, D), _grid_pos_emb())
    cond8 = jnp.zeros((8, D), F32).at[0].set(c_ctx).at[1:1 + NB_S].set(c)
    mod = _modulation(cond8, w_mod, b_mod)
    tri = _cumsum_matrices()
    w_bf = w_in.astype(BF16)
    w_main = w_bf[:, :, :MAIN_COLS]
    w_prep = jnp.concatenate(
        [w_bf[:, :, MAIN_COLS + 2 * RANK:],
         jnp.pad(w_bf[:, :, MAIN_COLS:MAIN_COLS + 2 * RANK], ((0, 0), (0, 0), (0, 128 - 2 * RANK)))], axis=-1)
    dense_tiles = T // FFN_R
    dense_meta = (jnp.zeros((dense_tiles,), jnp.int32), jnp.full((dense_tiles,), FFN_R, jnp.int32),
                  jnp.arange(dense_tiles, dtype=jnp.int32))

    states = []
    for l in range(DEPTH):
        mod_l = mod[l]
        z, a_lr, u = _inproj(x, mod_l, w_main, w_prep, l)
        u2 = _conv_module(u, conv_w[l], conv_b[l], conv_ln_g[l], conv_ln_b[l],
                          w_pw2[l].astype(BF16), b_pw2[l])
        o_dir, s_dir = [], []
        for d in range(2):
            wa2 = jnp.zeros((2 * RANK, QC), F32).at[d * RANK:(d + 1) * RANK].set(w_a2[l, d]).astype(BF16)
            o, s_new = _gla(z, a_lr, wa2, b_a2[l, d].reshape(1, QC), tri[d], state_gla[:, l], d)
            o_dir.append(o)
            s_dir.append(s_new)
        states.append(jnp.stack(s_dir, axis=1))
        is_moe = l % 2 == 1
        i = l // 2
        router = None
        if is_moe:
            wr_hi = moe_w_router[i].astype(BF16)
            wr_lo = (moe_w_router[i] - wr_hi.astype(F32)).astype(BF16)
            router = (jnp.pad(jnp.concatenate([wr_hi, wr_lo], axis=1), ((0, 0), (0, 128 - 2 * NE))),
                      jnp.pad(moe_b_router[i], (0, 128 - NE)).reshape(1, 128))
        outs = _mixout(o_dir[0], o_dir[1], z, u2, x, mod_l, gla_norm_g[l], w_out[l].astype(BF16),
                       ln1_g[l], ln1_b[l], router)
        if is_moe:
            x1, h2, route, counts = outs
            pos, pad, t_exp, t_rows, t_blk = _routing_tables(route, counts)
            hs = _dispatch(pos, pad, h2)
            y = _ffn(hs, t_exp, t_rows, t_blk, moe_w1[i], moe_w3[i], moe_w2[i])
            out_p, out_s = _combine(pos, x1, route, mod_l, ln2_g[l], ln2_b[l], y)
            x = jnp.concatenate([out_p, out_s], axis=0) if l + 1 < DEPTH else None
        else:
            x1, h2 = outs
            y = _ffn(h2, *dense_meta, ffn_w1[i][None], ffn_w3[i][None], ffn_w2[i][None])
            x = _ln2(x1, y, mod_l, ln2_g[l], ln2_b[l])

    if x is not None:
        out_p, out_s = x[:TP], x[TP:]
    return out_p.reshape(NB_P, L_P, D), out_s.reshape(NB_S, L_S, D), jnp.stack(states, axis=1)
```

```python
import functools

import jax
import jax.numpy as jnp
import numpy as np
from jax import lax
from jax.experimental import pallas as pl
from jax.experimental.pallas import tpu as pltpu

F32 = jnp.float32
BF16 = jnp.bfloat16

D = 2048
NB_P, L_P = 16, 256
NB_S, L_S = 4, 1024
TP = NB_P * L_P
T = TP + NB_S * L_S
DEPTH = 2
GRID_W = 64
GW = D // 2
CW = D - GW
H = 4
DV = GW // H
DK = DV // 2
QC = H * DK
RANK = 16
TAU = 16.0
CHUNK = 64
CONV_K = 31
FF = 7 * D // 2
NE = 8
ALPHA = (2 * DEPTH) ** 0.25
LN_EPS = 1e-5
RMS_EPS = 1e-6
MAIN_COLS = 2 * QC + 2 * GW

TILE = 256
NTILES = T // TILE
NTILES_P = TP // TILE
TILES_PER_S = L_S // TILE
HALO = 16

VMEM_LIMIT = 56 * 1024 * 1024

SH1, SC1, G1, SH2, SC2, G2 = range(6)


def _params(n_axes, vmem=VMEM_LIMIT):
    return pltpu.CompilerParams(dimension_semantics=("arbitrary",) * n_axes,
                                vmem_limit_bytes=vmem)


def _mod_row(tok0):
    return jnp.where(tok0 < TP, 0, 1 + (tok0 - TP) // L_S)


def _sigmoid(x):
    return 1.0 / (1.0 + jnp.exp(-x))


def _silu(x):
    return x * _sigmoid(x)


def _layer_norm(r, g, b):
    mu = jnp.mean(r, -1, keepdims=True)
    rc = r - mu
    var = jnp.mean(rc * rc, -1, keepdims=True)
    return rc * lax.rsqrt(var + LN_EPS) * g + b


def _assemble_kernel(xp_ref, xs_ref, pos_ref, o_ref):
    i = pl.program_id(0)

    @pl.when(i < NTILES_P)
    def _():
        o_ref[...] = xp_ref[...]

    @pl.when(i >= NTILES_P)
    def _():
        o_ref[...] = xs_ref[...] + pos_ref[...]


def _assemble(xp, xs, pos):
    return pl.pallas_call(
        _assemble_kernel,
        out_shape=jax.ShapeDtypeStruct((T, D), F32),
        grid=(NTILES,),
        in_specs=[
            pl.BlockSpec((TILE, D), lambda i: (jnp.minimum(i, NTILES_P - 1), 0)),
            pl.BlockSpec((TILE, D), lambda i: (jnp.maximum(i - NTILES_P, 0), 0)),
            pl.BlockSpec((TILE, D), lambda i: (jnp.maximum(i - NTILES_P, 0) % TILES_PER_S, 0)),
        ],
        out_specs=pl.BlockSpec((TILE, D), lambda i: (i, 0)),
        compiler_params=_params(1),
        name="assemble",
    )(xp, xs, pos)


MOD_TN = 1024


def _mod_kernel(c_ref, w_ref, b_ref, o_ref):
    s = _silu(c_ref[...])
    s_hi = s.astype(BF16)
    s_lo = (s - s_hi.astype(F32)).astype(BF16)
    w = w_ref[0]
    w_hi = w.astype(BF16)
    w_lo = (w - w_hi.astype(F32)).astype(BF16)
    o_ref[0] = (jnp.dot(s_hi, w_hi, preferred_element_type=F32)
                + jnp.dot(s_hi, w_lo, preferred_element_type=F32)
                + jnp.dot(s_lo, w_hi, preferred_element_type=F32)) + b_ref[0]


def _modulation(cond8, w_mod, b_mod):
    out = pl.pallas_call(
        _mod_kernel,
        out_shape=jax.ShapeDtypeStruct((DEPTH, 8, 6 * D), F32),
        grid=(DEPTH, 6 * D // MOD_TN),
        in_specs=[
            pl.BlockSpec((8, D), lambda l, j: (0, 0)),
            pl.BlockSpec((1, D, MOD_TN), lambda l, j: (l, 0, j)),
            pl.BlockSpec((1, 1, MOD_TN), lambda l, j: (l, 0, j)),
        ],
        out_specs=pl.BlockSpec((1, 8, MOD_TN), lambda l, j: (l, 0, j)),
        compiler_params=_params(2),
        name="modulation",
    )(cond8, w_mod, b_mod.reshape(DEPTH, 1, 6 * D))
    return out.reshape(DEPTH, 8, 6, D)


PROJ_TM = 1024
PROJ_TN = 1024
PROJ_NMAIN = MAIN_COLS // PROJ_TN
GLU_TN = 512
PROJ_NGLU = CW // GLU_TN

def _modulated(x_ref, mod_ref, shift, scale):
    m = mod_ref[0]
    return x_ref[...] * (1.0 + m[scale:scale + 1, :]) + m[shift:shift + 1, :]


def _inproj_kernel(x_ref, mod_ref, w_ref, wa_ref, wua_ref, wug_ref, z_ref, a_ref, u_ref, h_scr):
    j = pl.program_id(1)

    @pl.when(j == 0)
    def _():
        hb = _modulated(x_ref, mod_ref, SH1, SC1).astype(BF16)
        h_scr[...] = hb
        a_ref[...] = jnp.dot(hb, wa_ref[0], preferred_element_type=F32)

    @pl.when(j < PROJ_NMAIN)
    def _():
        z_ref[...] = jnp.dot(h_scr[...], w_ref[0], preferred_element_type=F32)

    @pl.when(j >= PROJ_NMAIN)
    def _():
        h = h_scr[...]
        a = jnp.dot(h, wua_ref[0], preferred_element_type=F32)
        g = jnp.dot(h, wug_ref[0], preferred_element_type=F32)
        u_ref[...] = a * _sigmoid(g)


def _inproj(x, mod_l, w_main, w_prep, layer):
    main_j = lambda j: jnp.minimum(j, PROJ_NMAIN - 1)
    glu_j = lambda j: jnp.maximum(j - PROJ_NMAIN, 0)
    val0 = 0
    gate0 = CW // GLU_TN
    low_rank = 2 * CW // 128
    return pl.pallas_call(
        _inproj_kernel,
        out_shape=(jax.ShapeDtypeStruct((T, MAIN_COLS), F32),
                   jax.ShapeDtypeStruct((T, 128), F32),
                   jax.ShapeDtypeStruct((T, CW), F32)),
        grid=(T // PROJ_TM, PROJ_NMAIN + PROJ_NGLU),
        in_specs=[
            pl.BlockSpec((PROJ_TM, D), lambda i, j: (i, 0)),
            pl.BlockSpec((1, 6, D), lambda i, j: (_mod_row(i * PROJ_TM), 0, 0)),
            pl.BlockSpec((1, D, PROJ_TN), lambda i, j: (layer, 0, main_j(j))),
            pl.BlockSpec((1, D, 128), lambda i, j: (layer, 0, low_rank)),
            pl.BlockSpec((1, D, GLU_TN), lambda i, j: (layer, 0, val0 + glu_j(j))),
            pl.BlockSpec((1, D, GLU_TN), lambda i, j: (layer, 0, gate0 + glu_j(j))),
        ],
        out_specs=(pl.BlockSpec((PROJ_TM, PROJ_TN), lambda i, j: (i, main_j(j))),
                   pl.BlockSpec((PROJ_TM, 128), lambda i, j: (i, 0)),
                   pl.BlockSpec((PROJ_TM, GLU_TN), lambda i, j: (i, glu_j(j)))),
        scratch_shapes=[pltpu.VMEM((PROJ_TM, D), BF16)],
        compiler_params=_params(2),
        name="inproj",
    )(x, mod_l, w_main, w_prep, w_prep, w_prep)


CONV_RC = 64
CONV_CC = 128
CONV_SPAN = (CONV_K + HALO - CONV_K // 2 + 7) // 8 * 8


def _conv_kernel(uc_ref, up_ref, un_ref, cw_ref, cb_ref, lg_ref, lb_ref, wp_ref, bp_ref,
                 o_ref, pad_scr, conv_scr):
    i = pl.program_id(0)
    s = jnp.maximum(i - NTILES_P, 0) % TILES_PER_S
    has_prev = jnp.logical_and(i >= NTILES_P, s != 0)
    has_next = jnp.logical_and(i >= NTILES_P, s != TILES_PER_S - 1)
    pad_scr[0:HALO, :] = jnp.where(has_prev, up_ref[...], 0.0)
    pad_scr[HALO:HALO + TILE, :] = uc_ref[...]
    pad_scr[HALO + TILE:HALO + TILE + HALO, :] = jnp.where(has_next, un_ref[...], 0.0)

    off = HALO - CONV_K // 2
    for c in range(CW // CONV_CC):
        cs = slice(c * CONV_CC, (c + 1) * CONV_CC)

        def body(r, carry, cs=cs):
            r0 = pl.multiple_of(r * CONV_RC, CONV_RC)
            win = pad_scr[pl.ds(r0, CONV_RC + CONV_SPAN), cs]
            acc = None
            for b in range(8):
                part = None
                for a in range((CONV_K + off) // 8 + 1):
                    k = 8 * a + b - off
                    if 0 <= k < CONV_K:
                        term = cw_ref[k:k + 1, cs] * win[8 * a:8 * a + CONV_RC + 8, :]
                        part = term if part is None else part + term
                part = part[b:b + CONV_RC, :]
                acc = part if acc is None else acc + part
            conv_scr[pl.ds(r0, CONV_RC), cs] = acc
            return carry

        lax.fori_loop(0, TILE // CONV_RC, body, 0)

    v = conv_scr[...] + cb_ref[...]
    y = _silu(_layer_norm(v, lg_ref[...], lb_ref[...]))
    o_ref[...] = (jnp.dot(y.astype(BF16), wp_ref[...], preferred_element_type=F32) + bp_ref[...]).astype(BF16)


def _conv_module(u, conv_w, conv_b, ln_g, ln_b, w_pw2, b_pw2):
    hb = TILE // HALO
    row = lambda a: a.reshape(1, CW)
    return pl.pallas_call(
        _conv_kernel,
        out_shape=jax.ShapeDtypeStruct((T, CW), BF16),
        grid=(NTILES,),
        in_specs=[
            pl.BlockSpec((TILE, CW), lambda i: (i, 0)),
            pl.BlockSpec((HALO, CW), lambda i: (jnp.maximum(i * hb - 1, 0), 0)),
            pl.BlockSpec((HALO, CW), lambda i: (jnp.minimum((i + 1) * hb, T // HALO - 1), 0)),
            pl.BlockSpec((CONV_K, CW), lambda i: (0, 0)),
            pl.BlockSpec((1, CW), lambda i: (0, 0)),
            pl.BlockSpec((1, CW), lambda i: (0, 0)),
            pl.BlockSpec((1, CW), lambda i: (0, 0)),
            pl.BlockSpec((CW, CW), lambda i: (0, 0)),
            pl.BlockSpec((1, CW), lambda i: (0, 0)),
        ],
        out_specs=pl.BlockSpec((TILE, CW), lambda i: (i, 0)),
        scratch_shapes=[pltpu.VMEM((TILE + 2 * HALO, CW), F32), pltpu.VMEM((TILE, CW), F32)],
        compiler_params=_params(1),
        name="conv_module",
    )(u, u, u, conv_w, row(conv_b), row(ln_g), row(ln_b), w_pw2, row(b_pw2))


def _split3(x):
    hi = x.astype(BF16)
    r1 = x - hi.astype(F32)
    mid = r1.astype(BF16)
    lo = (r1 - mid.astype(F32)).astype(BF16)
    return hi, mid, lo


def _log_sigmoid(x):
    return jnp.minimum(x, 0.0) - jnp.log(1.0 + jnp.exp(-jnp.abs(x)))


def _dot_nt(a, b):
    return lax.dot_general(a, b, (((1,), (1,)), ((), ())), preferred_element_type=F32)


def _dot_tn(a, b):
    return lax.dot_general(a, b, (((0,), (0,)), ((), ())), preferred_element_type=F32)


def _gla_kernel(q_ref, k_ref, v_ref, a_ref, wa2_ref, ba2_ref, tri_ref, s0_ref,
                o_ref, snew_ref, st_scr, *, reverse):
    n = pl.program_id(0)
    tile = (NTILES - 1 - n) if reverse else n
    is_prompt = tile < NTILES_P
    spos = jnp.maximum(tile - NTILES_P, 0) % TILES_PER_S
    seq_first = spos == (TILES_PER_S - 1 if reverse else 0)

    @pl.when(is_prompt)
    def _():
        st_scr[...] = jnp.zeros_like(st_scr)

    @pl.when(jnp.logical_and(jnp.logical_not(is_prompt), seq_first))
    def _():
        for h in range(H):
            st_scr[h] = s0_ref[0, 0, h].T

    tri = tri_ref[...]
    mask = tri > 0
    a_lr = a_ref[:, 0:2 * RANK].astype(BF16)
    n_chunks = TILE // CHUNK
    order = range(n_chunks - 1, -1, -1) if reverse else range(n_chunks)
    logits = jnp.dot(a_lr, wa2_ref[...], preferred_element_type=F32) + ba2_ref[...]
    hi, mid, lo = _split3(_log_sigmoid(logits) / TAU)
    b_all = (jnp.dot(tri, hi, preferred_element_type=F32)
             + jnp.dot(tri, mid, preferred_element_type=F32)
             + jnp.dot(tri, lo, preferred_element_type=F32))
    for h in range(H):
        ks = slice(h * DK, (h + 1) * DK)
        b = b_all[:, ks]
        q = q_ref[:, ks] * (DK ** -0.5)
        k = k_ref[:, ks]
        vb = v_ref[:, h * DV:(h + 1) * DV].astype(BF16)
        qd = (q * jnp.exp(b)).astype(BF16)
        kd = (k * jnp.exp(-b)).astype(BF16)
        att = jnp.where(mask, _dot_nt(qd, kd), 0.0).astype(BF16)
        o_intra = jnp.dot(att, vb, preferred_element_type=F32)
        st = st_scr[h]
        for c in order:
            rows = slice(c * CHUNK, (c + 1) * CHUNK)
            last = c * CHUNK if reverse else (c + 1) * CHUNK - 1
            b_last = b[last:last + 1, :]
            kl = (k[rows] * jnp.exp(b_last - b[rows])).astype(BF16)
            o_ref[rows, h * DV:(h + 1) * DV] = o_intra[rows] + _dot_nt(qd[rows], st.astype(BF16))
            st = jnp.exp(b_last) * st + _dot_tn(vb[rows], kl)
        st_scr[h] = st

    @pl.when(is_prompt)
    def _():
        for h in range(H):
            snew_ref[0, h] = st_scr[h].T


def _gla(z, a_lr, wa2_dir, ba2_dir, tri_dir, s0, direction):
    reverse = direction == 1
    tile = (lambda n: NTILES - 1 - n) if reverse else (lambda n: n)
    req = lambda n: jnp.clip((tile(n) - NTILES_P) // TILES_PER_S, 0, NB_S - 1)
    return pl.pallas_call(
        functools.partial(_gla_kernel, reverse=reverse),
        out_shape=(jax.ShapeDtypeStruct((T, GW), F32),
                   jax.ShapeDtypeStruct((NB_P, H, DK, DV), F32)),
        grid=(NTILES,),
        in_specs=[
            pl.BlockSpec((TILE, QC), lambda n: (tile(n), 0)),
            pl.BlockSpec((TILE, QC), lambda n: (tile(n), 1)),
            pl.BlockSpec((TILE, GW), lambda n: (tile(n), 1)),
            pl.BlockSpec((TILE, 128), lambda n: (tile(n), 0)),
            pl.BlockSpec((2 * RANK, QC), lambda n: (0, 0)),
            pl.BlockSpec((1, QC), lambda n: (0, 0)),
            pl.BlockSpec((TILE, TILE), lambda n: (0, 0)),
            pl.BlockSpec((1, 1, H, DK, DV), lambda n: (req(n), direction, 0, 0, 0)),
        ],
        out_specs=(pl.BlockSpec((TILE, GW), lambda n: (tile(n), 0)),
                   pl.BlockSpec((1, H, DK, DV), lambda n: (jnp.minimum(tile(n), NTILES_P - 1), 0, 0, 0))),
        scratch_shapes=[pltpu.VMEM((H, DV, DK), F32)],
        compiler_params=_params(1),
        name="gla_bwd" if reverse else "gla_fwd",
    )(z, z, z, a_lr, wa2_dir, ba2_dir, tri_dir, s0)


MIX_GROUPS = 1


def _mixout_kernel(of_ref, ob_ref, g_ref, u_ref, x_ref, mod_ref, gng_ref, wo_ref, l1g_ref, l1b_ref,
                   *rest, with_router):
    if with_router:
        wr_ref, br_ref, ltri_ref, x1_ref, h2_ref, route_ref, cnt_ref, run_scr = rest
    else:
        x1_ref, h2_ref = rest
    m = mod_ref[0]
    h2_parts = []
    for p in range(MIX_GROUPS):
        rs = slice(p * (TILE // MIX_GROUPS), (p + 1) * (TILE // MIX_GROUPS))
        o = of_ref[rs, :] + ob_ref[rs, :]
        parts = []
        for h in range(H):
            oh = o[:, h * DV:(h + 1) * DV]
            ms = jnp.mean(oh * oh, -1, keepdims=True)
            parts.append(oh * lax.rsqrt(ms + RMS_EPS) * gng_ref[...])
        on = jnp.concatenate(parts, axis=-1) * _silu(g_ref[rs, :])
        y = (jnp.dot(on.astype(BF16), wo_ref[0:GW, :], preferred_element_type=F32)
             + jnp.dot(u_ref[rs, :], wo_ref[GW:D, :], preferred_element_type=F32))
        x1 = _layer_norm(ALPHA * x_ref[rs, :] + m[G1:G1 + 1, :] * y, l1g_ref[...], l1b_ref[...])
        x1_ref[rs, :] = x1
        h2_part = x1 * (1.0 + m[SC2:SC2 + 1, :]) + m[SH2:SH2 + 1, :]
        h2_ref[rs, :] = h2_part.astype(h2_ref.dtype)
        h2_parts.append(h2_part)
    if with_router:
        h2 = jnp.concatenate(h2_parts, axis=0)
        h_hi = h2.astype(BF16)
        h_lo = (h2 - h_hi.astype(F32)).astype(BF16)
        p_hi = jnp.dot(h_hi, wr_ref[...], preferred_element_type=F32)
        p_lo = jnp.dot(h_lo, wr_ref[...], preferred_element_type=F32)
        logits = p_hi + pltpu.roll(p_hi, 128 - NE, 1) + p_lo + br_ref[...]
        lane = lax.broadcasted_iota(jnp.int32, logits.shape, 1)
        lg = jnp.where(lane < NE, logits, -jnp.inf)
        v1 = jnp.max(lg, -1, keepdims=True)
        i1 = jnp.min(jnp.where(lg == v1, lane, 128), -1, keepdims=True)
        lg2 = jnp.where(lane == i1, -jnp.inf, lg)
        v2 = jnp.max(lg2, -1, keepdims=True)
        i2 = jnp.min(jnp.where(lg2 == v2, lane, 128), -1, keepdims=True)
        e2 = jnp.exp(v2 - v1)
        w1 = 1.0 / (1.0 + e2)
        w2 = e2 / (1.0 + e2)
        @pl.when(pl.program_id(0) == 0)
        def _():
            run_scr[...] = jnp.zeros_like(run_scr)

        hit1 = lane == i1
        hit2 = lane == i2
        one1 = jnp.where(hit1, 1.0, 0.0)
        one2 = jnp.where(hit2, 1.0, 0.0)
        before1 = jnp.dot(ltri_ref[...], one1.astype(BF16), preferred_element_type=F32)
        before2 = jnp.dot(ltri_ref[...], one2.astype(BF16), preferred_element_type=F32)
        tot1 = jnp.sum(one1, axis=0, keepdims=True)
        tot2 = jnp.sum(one2, axis=0, keepdims=True)
        run = run_scr[...]
        rank1 = jnp.sum(jnp.where(hit1, before1 + run, 0.0), -1, keepdims=True)
        rank2 = jnp.sum(jnp.where(hit2, before2 + (run + tot1), 0.0), -1, keepdims=True)
        run = run + tot1 + tot2
        run_scr[...] = run
        cnt_ref[...] = jnp.broadcast_to(run, cnt_ref.shape)
        route_ref[...] = jnp.where(lane == 0, i1.astype(F32),
                         jnp.where(lane == 1, i2.astype(F32),
                         jnp.where(lane == 2, w1,
                         jnp.where(lane == 3, w2,
                         jnp.where(lane == 4, rank1, jnp.where(lane == 5, rank2, 0.0))))))


def _mixout(o_f, o_b, z, u2, x, mod_l, gng, w_out, ln_g, ln_b, router=None):
    row = lambda a: a.reshape(1, -1)
    full = lambda shape: pl.BlockSpec(shape, lambda i: (0,) * len(shape))
    in_specs = [
        pl.BlockSpec((TILE, GW), lambda i: (i, 0)),
        pl.BlockSpec((TILE, GW), lambda i: (i, 0)),
        pl.BlockSpec((TILE, GW), lambda i: (i, 2)),
        pl.BlockSpec((TILE, CW), lambda i: (i, 0)),
        pl.BlockSpec((TILE, D), lambda i: (i, 0)),
        pl.BlockSpec((1, 6, D), lambda i: (_mod_row(i * TILE), 0, 0)),
        full((1, DV)), full((D, D)), full((1, D)), full((1, D)),
    ]
    args = [o_f, o_b, z, u2, x, mod_l, row(gng), w_out, row(ln_g), row(ln_b)]
    out_shape = [jax.ShapeDtypeStruct((T, D), F32)]
    out_specs = [pl.BlockSpec((TILE, D), lambda i: (i, 0)), pl.BlockSpec((TILE, D), lambda i: (i, 0))]
    scratch = []
    if router is None:
        out_shape.append(jax.ShapeDtypeStruct((T, D), BF16))
    else:
        w_r, b_r = router
        idx = jnp.arange(TILE)
        ltri = (idx[None, :] < idx[:, None]).astype(BF16)
        in_specs += [full((D, 128)), full((1, 128)), full((TILE, TILE))]
        args += [w_r, b_r, ltri]
        out_shape += [jax.ShapeDtypeStruct((T, D), F32),
                      jax.ShapeDtypeStruct((T, 128), F32), jax.ShapeDtypeStruct((8, 128), F32)]
        out_specs += [pl.BlockSpec((TILE, 128), lambda i: (i, 0)), pl.BlockSpec((8, 128), lambda i: (0, 0))]
        scratch = [pltpu.VMEM((1, 128), F32)]
    return pl.pallas_call(
        functools.partial(_mixout_kernel, with_router=router is not None),
        out_shape=tuple(out_shape),
        grid=(NTILES,),
        in_specs=in_specs,
        out_specs=tuple(out_specs),
        scratch_shapes=scratch,
        compiler_params=_params(1),
        name="mixout",
    )(*args)


FFN_R = 1024
FFN_SUB = 256
FFN_TF = 512


def _ffn_kernel(te_ref, tn_ref, tb_ref, h_ref, w1_ref, w3_ref, w2_ref, o_ref, g_scr, a_scr, *h_scr, wide_rows):
    s = pl.program_id(0)
    j = pl.program_id(1)
    n = tn_ref[s]
    n_sub = (n + FFN_SUB - 1) // FFN_SUB

    @pl.when(jnp.logical_and(j == 0, n > 0))
    def _():
        o_ref[...] = jnp.zeros_like(o_ref)

    rows_ref = h_scr[0] if wide_rows else h_ref
    for k in range(1, FFN_R // FFN_SUB + 1):
        m = k * FFN_SUB

        if wide_rows:
            @pl.when(jnp.logical_and(n_sub == k, j == 0))
            def _(m=m):
                rows_ref[0:m, :] = h_ref[0:m, :].astype(BF16)

        @pl.when(n_sub == k)
        def _(m=m):
            hc = rows_ref[0:m, :]
            g_scr[0:m, :] = jnp.dot(hc, w1_ref[0].astype(BF16), preferred_element_type=F32)
            u = jnp.dot(hc, w3_ref[0].astype(BF16), preferred_element_type=F32)
            a_scr[0:m, :] = (_silu(g_scr[0:m, :]) * u).astype(BF16)
            o_ref[0:m, :] += jnp.dot(a_scr[0:m, :], w2_ref[0].astype(BF16), preferred_element_type=F32)


def _ffn(hs, tile_expert, tile_rows, tile_block, w1, w3, w2):
    n_tiles = tile_expert.shape[0]
    nj = FF // FFN_TF

    def jj(s, j, tn):
        return jnp.where(tn[s] > 0, j, nj - 1)

    once = pl.Buffered(1)
    wide_rows = hs.dtype == F32
    scratch = [pltpu.VMEM((FFN_R, FFN_TF), F32), pltpu.VMEM((FFN_R, FFN_TF), BF16)]
    if wide_rows:
        scratch.append(pltpu.VMEM((FFN_R, D), BF16))
    return pl.pallas_call(
        functools.partial(_ffn_kernel, wide_rows=wide_rows),
        out_shape=jax.ShapeDtypeStruct((hs.shape[0], D), F32),
        grid_spec=pltpu.PrefetchScalarGridSpec(
            num_scalar_prefetch=3,
            grid=(n_tiles, nj),
            in_specs=[
                pl.BlockSpec((FFN_R, hs.shape[1]), lambda s, j, te, tn, tb: (tb[s], 0), pipeline_mode=once),
                pl.BlockSpec((1, D, FFN_TF), lambda s, j, te, tn, tb: (te[s], 0, jj(s, j, tn))),
                pl.BlockSpec((1, D, FFN_TF), lambda s, j, te, tn, tb: (te[s], 0, jj(s, j, tn))),
                pl.BlockSpec((1, FFN_TF, D), lambda s, j, te, tn, tb: (te[s], jj(s, j, tn), 0)),
            ],
            out_specs=pl.BlockSpec((FFN_R, D), lambda s, j, te, tn, tb: (tb[s], 0), pipeline_mode=once),
            scratch_shapes=scratch,
        ),
        compiler_params=_params(2),
        name="ffn",
    )(tile_expert, tile_rows, tile_block, hs, w1, w3, w2)


DMA_UNROLL = 8


def _dispatch_kernel(pos_ref, pad_ref, h_ref, o_hbm, zero_scr, sem, zsem):
    i = pl.program_id(0)

    @pl.when(i == 0)
    def _():
        zero_scr[...] = jnp.zeros_like(zero_scr)
        for e in range(NE):
            first, count = pad_ref[e], pad_ref[NE + e]

            def zero_copy(r, first=first):
                return pltpu.make_async_copy(zero_scr, o_hbm.at[pl.ds(first + r, 1), :], zsem)

            def start(r, carry):
                zero_copy(r).start()
                return carry

            def wait(r, carry):
                zero_copy(r).wait()
                return carry

            lax.fori_loop(0, count, start, 0)
            lax.fori_loop(0, count, wait, 0)

    def issue(r, carry):
        for slot in range(2):
            p = pos_ref[2 * (i * TILE + r) + slot]
            pltpu.make_async_copy(h_ref.at[pl.ds(r, 1), :], o_hbm.at[pl.ds(p, 1), :], sem).start()
        return carry

    lax.fori_loop(0, TILE, issue, 0, unroll=DMA_UNROLL)
    for slot in range(2):
        pltpu.make_async_copy(h_ref, o_hbm.at[pl.ds(0, TILE), :], sem).wait()


def _dispatch(pos, pad, h2):
    return pl.pallas_call(
        _dispatch_kernel,
        out_shape=jax.ShapeDtypeStruct((MOE_TILES * FFN_R, D), F32),
        grid_spec=pltpu.PrefetchScalarGridSpec(
            num_scalar_prefetch=2,
            grid=(NTILES,),
            in_specs=[pl.BlockSpec((TILE, D), lambda i, pos, pad: (i, 0))],
            out_specs=pl.BlockSpec(memory_space=pl.ANY),
            scratch_shapes=[pltpu.VMEM((1, D), F32), pltpu.SemaphoreType.DMA, pltpu.SemaphoreType.DMA],
        ),
        compiler_params=_params(1),
        name="dispatch",
    )(pos, pad, h2)


def _ln2_kernel(x_ref, f_ref, mod_ref, g_ref, b_ref, o_ref):
    m = mod_ref[0]
    o_ref[...] = _layer_norm(ALPHA * x_ref[...] + m[G2:G2 + 1, :] * f_ref[...], g_ref[...], b_ref[...])


def _ln2(x1, f, mod_l, ln_g, ln_b):
    return pl.pallas_call(
        _ln2_kernel,
        out_shape=jax.ShapeDtypeStruct((T, D), F32),
        grid=(NTILES,),
        in_specs=[
            pl.BlockSpec((TILE, D), lambda i: (i, 0)),
            pl.BlockSpec((TILE, D), lambda i: (i, 0)),
            pl.BlockSpec((1, 6, D), lambda i: (_mod_row(i * TILE), 0, 0)),
            pl.BlockSpec((1, D), lambda i: (0, 0)),
            pl.BlockSpec((1, D), lambda i: (0, 0)),
        ],
        out_specs=pl.BlockSpec((TILE, D), lambda i: (i, 0)),
        compiler_params=_params(1),
        name="ln2",
    )(x1, f, mod_l, ln_g.reshape(1, D), ln_b.reshape(1, D))


def _combine_kernel(pos_ref, x_ref, route_ref, mod_ref, g_ref, b_ref, y_hbm, op_ref, os_ref, buf_a, buf_b, sem):
    i = pl.program_id(0)
    bufs = (buf_a, buf_b)

    def row_copy(tile, r, slot, dst, dsem):
        p = pos_ref[2 * (tile * TILE + r) + slot]
        return pltpu.make_async_copy(y_hbm.at[pl.ds(p, 1), :], dst.at[slot, pl.ds(r, 1), :], dsem)

    def wait_rows(dst, dsem):
        for slot in range(2):
            pltpu.make_async_copy(y_hbm.at[pl.ds(0, TILE), :], dst.at[slot], dsem).wait()

    @pl.when(i == 0)
    def _():
        def issue(r, carry):
            for slot in range(2):
                row_copy(0, r, slot, buf_a, sem.at[0]).start()
            return carry

        lax.fori_loop(0, TILE, issue, 0, unroll=DMA_UNROLL)

    for par in range(2):
        @pl.when(i % 2 == par)
        def _(par=par):
            cur, nxt = bufs[par], bufs[1 - par]
            wait_rows(cur, sem.at[par])
            nxt_tile = jnp.minimum(i + 1, NTILES - 1)
            for r in range(TILE):
                for slot in range(2):
                    row_copy(nxt_tile, r, slot, nxt, sem.at[1 - par]).start()
            route = route_ref[...]
            f = route[:, 2:3] * cur[0] + route[:, 3:4] * cur[1]
            m = mod_ref[0]
            out = _layer_norm(ALPHA * x_ref[...] + m[G2:G2 + 1, :] * f, g_ref[...], b_ref[...])

            @pl.when(i < NTILES_P)
            def _():
                op_ref[...] = out

            @pl.when(i >= NTILES_P)
            def _():
                os_ref[...] = out

            @pl.when(i == NTILES - 1)
            def _():
                wait_rows(nxt, sem.at[1 - par])


def _combine(pos, x1, route, mod_l, ln_g, ln_b, y):
    return pl.pallas_call(
        _combine_kernel,
        out_shape=(jax.ShapeDtypeStruct((TP, D), F32), jax.ShapeDtypeStruct((T - TP, D), F32)),
        grid_spec=pltpu.PrefetchScalarGridSpec(
            num_scalar_prefetch=1,
            grid=(NTILES,),
            in_specs=[
                pl.BlockSpec((TILE, D), lambda i, pos: (i, 0)),
                pl.BlockSpec((TILE, 128), lambda i, pos: (i, 0)),
                pl.BlockSpec((1, 6, D), lambda i, pos: (_mod_row(i * TILE), 0, 0)),
                pl.BlockSpec((1, D), lambda i, pos: (0, 0)),
                pl.BlockSpec((1, D), lambda i, pos: (0, 0)),
                pl.BlockSpec(memory_space=pl.ANY),
            ],
            out_specs=(pl.BlockSpec((TILE, D), lambda i, pos: (jnp.minimum(i, NTILES_P - 1), 0)),
                       pl.BlockSpec((TILE, D), lambda i, pos: (jnp.maximum(i - NTILES_P, 0), 0))),
            scratch_shapes=[pltpu.VMEM((2, TILE, D), F32), pltpu.VMEM((2, TILE, D), F32),
                            pltpu.SemaphoreType.DMA((2,))],
        ),
        compiler_params=_params(1),
        name="combine",
    )(pos, x1, route, mod_l, ln_g.reshape(1, D), ln_b.reshape(1, D), y)


MOE_TILES = 2 * T // FFN_R + NE


def _routing_tables(route, counts_f):
    eidx = route[:, 0:2].astype(jnp.int32)
    rank = route[:, 4:6].astype(jnp.int32)
    counts = counts_f[0, :NE].astype(jnp.int32)
    ntiles = (counts + FFN_R - 1) // FFN_R
    per_tile = (counts + jnp.maximum(ntiles, 1) - 1) // jnp.maximum(ntiles, 1)
    per_tile = (per_tile + FFN_SUB - 1) // FFN_SUB * FFN_SUB
    tend = jnp.cumsum(ntiles)
    tstart = tend - ntiles
    experts = jnp.arange(NE, dtype=jnp.int32)
    pick = lambda table: jnp.sum(jnp.where(eidx[..., None] == experts, table, 0), axis=-1)
    rows_e = pick(per_tile)
    local = sum((rank >= k * rows_e).astype(jnp.int32) for k in range(1, T // FFN_R))
    pos = ((pick(tstart) + local) * FFN_R + (rank - local * rows_e)).reshape(-1)
    tiles = jnp.arange(MOE_TILES, dtype=jnp.int32)
    total = tend[-1]
    t_eff = jnp.minimum(tiles, total - 1)
    tile_expert = jnp.minimum(jnp.sum((t_eff[:, None] >= tend[None, :]).astype(jnp.int32), axis=1), NE - 1)
    tile_rows = jnp.clip(counts[tile_expert] - (t_eff - tstart[tile_expert]) * per_tile[tile_expert],
                         0, per_tile[tile_expert])
    tile_rows = jnp.where(tiles < total, tile_rows, 0)
    last_rows = counts - (ntiles - 1) * per_tile
    pad = jnp.concatenate([(tend - 1) * FFN_R + last_rows, (-last_rows) % FFN_SUB]).astype(jnp.int32)
    return pos, pad, tile_expert.astype(jnp.int32), tile_rows.astype(jnp.int32), t_eff


def _grid_pos_emb():
    rows = L_S // GRID_W
    r = np.repeat(np.arange(rows, dtype=np.float64), GRID_W)
    col = np.tile(np.arange(GRID_W, dtype=np.float64), rows)
    quarter = D // 4
    freqs = 1.0 / (10000.0 ** (np.arange(quarter, dtype=np.float64) / quarter))

    def enc(p):
        ang = p[:, None] * freqs[None, :]
        return np.concatenate([np.sin(ang), np.cos(ang)], -1)

    return jnp.asarray(np.concatenate([enc(r), enc(col)], -1), dtype=F32)


def _cumsum_matrices():
    i = jnp.arange(TILE)
    same = (i[:, None] // CHUNK) == (i[None, :] // CHUNK)
    fwd = jnp.logical_and(same, i[None, :] <= i[:, None])
    bwd = jnp.logical_and(same, i[None, :] >= i[:, None])
    return fwd.astype(BF16), bwd.astype(BF16)


def kernel(x_prompt, x_sample, state_gla, c, c_ctx, w_mod, b_mod, w_in, w_a2, b_a2, gla_norm_g, conv_w, conv_b, conv_ln_g, conv_ln_b, w_pw2, b_pw2, w_out, ln1_g, ln1_b, ln2_g, ln2_b, ffn_w1, ffn_w3, ffn_w2, moe_w_router, moe_b_router, moe_w1, moe_w3, moe_w2):
    x = _assemble(x_prompt.reshape(TP, D), x_sample.reshape(T - TP, D), _grid_pos_emb())
    cond8 = jnp.zeros((8, D), F32).at[0].set(c_ctx).at[1:1 + NB_S].set(c)
    mod = _modulation(cond8, w_mod, b_mod)
    tri = _cumsum_matrices()
    w_bf = w_in.astype(BF16)
    w_main = w_bf[:, :, :MAIN_COLS]
    w_prep = jnp.concatenate(
        [w_bf[:, :, MAIN_COLS + 2 * RANK:],
         jnp.pad(w_bf[:, :, MAIN_COLS:MAIN_COLS + 2 * RANK], ((0, 0), (0, 0), (0, 128 - 2 * RANK)))], axis=-1)
    dense_tiles = T // FFN_R
    dense_meta = (jnp.zeros((dense_tiles,), jnp.int32), jnp.full((dense_tiles,), FFN_R, jnp.int32),
                  jnp.arange(dense_tiles, dtype=jnp.int32))

    states = []
    for l in range(DEPTH):
        mod_l = mod[l]
        z, a_lr, u = _inproj(x, mod_l, w_main, w_prep, l)
        u2 = _conv_module(u, conv_w[l], conv_b[l], conv_ln_g[l], conv_ln_b[l],
                          w_pw2[l].astype(BF16), b_pw2[l])
        o_dir, s_dir = [], []
        for d in range(2):
            wa2 = jnp.zeros((2 * RANK, QC), F32).at[d * RANK:(d + 1) * RANK].set(w_a2[l, d]).astype(BF16)
            o, s_new = _gla(z, a_lr, wa2, b_a2[l, d].reshape(1, QC), tri[d], state_gla[:, l], d)
            o_dir.append(o)
            s_dir.append(s_new)
        states.append(jnp.stack(s_dir, axis=1))
        is_moe = l % 2 == 1
        i = l // 2
        router = None
        if is_moe:
            wr_hi = moe_w_router[i].astype(BF16)
            wr_lo = (moe_w_router[i] - wr_hi.astype(F32)).astype(BF16)
            router = (jnp.pad(jnp.concatenate([wr_hi, wr_lo], axis=1), ((0, 0), (0, 128 - 2 * NE))),
                      jnp.pad(moe_b_router[i], (0, 128 - NE)).reshape(1, 128))
        outs = _mixout(o_dir[0], o_dir[1], z, u2, x, mod_l, gla_norm_g[l], w_out[l].astype(BF16),
                       ln1_g[l], ln1_b[l], router)
        if is_moe:
            x1, h2, route, counts = outs
            pos, pad, t_exp, t_rows, t_blk = _routing_tables(route, counts)
            hs = _dispatch(pos, pad, h2)
            y = _ffn(hs, t_exp, t_rows, t_blk, moe_w1[i], moe_w3[i], moe_w2[i])
            out_p, out_s = _combine(pos, x1, route, mod_l, ln2_g[l], ln2_b[l], y)
            x = jnp.concatenate([out_p, out_s], axis=0) if l + 1 < DEPTH else None
        else:
            x1, h2 = outs
            y = _ffn(h2, *dense_meta, ffn_w1[i][None], ffn_w3[i][None], ffn_w2[i][None])
            x = _ln2(x1, y, mod_l, ln2_g[l], ln2_b[l])

    if x is not None:
        out_p, out_s = x[:TP], x[TP:]
    return out_p.reshape(NB_P, L_P, D), out_s.reshape(NB_S, L_S, D), jnp.stack(states, axis=1)
```

```python
import functools

import jax
import jax.numpy as jnp
import numpy as np
from jax import lax
from jax.experimental import pallas as pl
from jax.experimental.pallas import tpu as pltpu

F32 = jnp.float32
BF16 = jnp.bfloat16

D = 2048
NB_P, L_P = 16, 256
NB_S, L_S = 4, 1024
TP = NB_P * L_P
T = TP + NB_S * L_S
DEPTH = 2
GRID_W = 64
GW = D // 2
CW = D - GW
H = 4
DV = GW // H
DK = DV // 2
QC = H * DK
RANK = 16
TAU = 16.0
CHUNK = 64
CONV_K = 31
FF = 7 * D // 2
NE = 8
ALPHA = (2 * DEPTH) ** 0.25
LN_EPS = 1e-5
RMS_EPS = 1e-6
MAIN_COLS = 2 * QC + 2 * GW

TILE = 256
NTILES = T // TILE
NTILES_P = TP // TILE
TILES_PER_S = L_S // TILE
HALO = 16

VMEM_LIMIT = 56 * 1024 * 1024

SH1, SC1, G1, SH2, SC2, G2 = range(6)


def _params(n_axes, vmem=VMEM_LIMIT):
    return pltpu.CompilerParams(dimension_semantics=("arbitrary",) * n_axes,
                                vmem_limit_bytes=vmem)


def _mod_row(tok0):
    return jnp.where(tok0 < TP, 0, 1 + (tok0 - TP) // L_S)


def _sigmoid(x):
    return 1.0 / (1.0 + jnp.exp(-x))


def _silu(x):
    return x * _sigmoid(x)


def _layer_norm(r, g, b):
    mu = jnp.mean(r, -1, keepdims=True)
    rc = r - mu
    var = jnp.mean(rc * rc, -1, keepdims=True)
    return rc * lax.rsqrt(var + LN_EPS) * g + b


def _assemble_kernel(xp_ref, xs_ref, pos_ref, o_ref):
    i = pl.program_id(0)

    @pl.when(i < NTILES_P)
    def _():
        o_ref[...] = xp_ref[...]

    @pl.when(i >= NTILES_P)
    def _():
        o_ref[...] = xs_ref[...] + pos_ref[...]


def _assemble(xp, xs, pos):
    return pl.pallas_call(
        _assemble_kernel,
        out_shape=jax.ShapeDtypeStruct((T, D), F32),
        grid=(NTILES,),
        in_specs=[
            pl.BlockSpec((TILE, D), lambda i: (jnp.minimum(i, NTILES_P - 1), 0)),
            pl.BlockSpec((TILE, D), lambda i: (jnp.maximum(i - NTILES_P, 0), 0)),
            pl.BlockSpec((TILE, D), lambda i: (jnp.maximum(i - NTILES_P, 0) % TILES_PER_S, 0)),
        ],
        out_specs=pl.BlockSpec((TILE, D), lambda i: (i, 0)),
        compiler_params=_params(1),
        name="assemble",
    )(xp, xs, pos)


MOD_TN = 1024


def _mod_kernel(c_ref, w_ref, b_ref, o_ref):
    s = _silu(c_ref[...])
    s_hi = s.astype(BF16)
    s_lo = (s - s_hi.astype(F32)).astype(BF16)
    w = w_ref[0]
    w_hi = w.astype(BF16)
    w_lo = (w - w_hi.astype(F32)).astype(BF16)
    o_ref[0] = (jnp.dot(s_hi, w_hi, preferred_element_type=F32)
                + jnp.dot(s_hi, w_lo, preferred_element_type=F32)
                + jnp.dot(s_lo, w_hi, preferred_element_type=F32)) + b_ref[0]


def _modulation(cond8, w_mod, b_mod):
    out = pl.pallas_call(
        _mod_kernel,
        out_shape=jax.ShapeDtypeStruct((DEPTH, 8, 6 * D), F32),
        grid=(DEPTH, 6 * D // MOD_TN),
        in_specs=[
            pl.BlockSpec((8, D), lambda l, j: (0, 0)),
            pl.BlockSpec((1, D, MOD_TN), lambda l, j: (l, 0, j)),
            pl.BlockSpec((1, 1, MOD_TN), lambda l, j: (l, 0, j)),
        ],
        out_specs=pl.BlockSpec((1, 8, MOD_TN), lambda l, j: (l, 0, j)),
        compiler_params=_params(2),
        name="modulation",
    )(cond8, w_mod, b_mod.reshape(DEPTH, 1, 6 * D))
    return out.reshape(DEPTH, 8, 6, D)


PROJ_TM = 1024
PROJ_TN = 1024
PROJ_NMAIN = MAIN_COLS // PROJ_TN
GLU_TN = 512
PROJ_NGLU = CW // GLU_TN

def _modulated(x_ref, mod_ref, shift, scale):
    m = mod_ref[0]
    return x_ref[...] * (1.0 + m[scale:scale + 1, :]) + m[shift:shift + 1, :]


def _inproj_kernel(x_ref, mod_ref, w_ref, wa_ref, wua_ref, wug_ref, z_ref, a_ref, u_ref, h_scr):
    j = pl.program_id(1)

    @pl.when(j == 0)
    def _():
        hb = _modulated(x_ref, mod_ref, SH1, SC1).astype(BF16)
        h_scr[...] = hb
        a_ref[...] = jnp.dot(hb, wa_ref[0], preferred_element_type=F32)

    @pl.when(j < PROJ_NMAIN)
    def _():
        z_ref[...] = jnp.dot(h_scr[...], w_ref[0], preferred_element_type=F32)

    @pl.when(j >= PROJ_NMAIN)
    def _():
        h = h_scr[...]
        a = jnp.dot(h, wua_ref[0], preferred_element_type=F32)
        g = jnp.dot(h, wug_ref[0], preferred_element_type=F32)
        u_ref[...] = a * _sigmoid(g)


def _inproj(x, mod_l, w_main, w_prep, layer):
    main_j = lambda j: jnp.minimum(j, PROJ_NMAIN - 1)
    glu_j = lambda j: jnp.maximum(j - PROJ_NMAIN, 0)
    val0 = 0
    gate0 = CW // GLU_TN
    low_rank = 2 * CW // 128
    return pl.pallas_call(
        _inproj_kernel,
        out_shape=(jax.ShapeDtypeStruct((T, MAIN_COLS), F32),
                   jax.ShapeDtypeStruct((T, 128), F32),
                   jax.ShapeDtypeStruct((T, CW), F32)),
        grid=(T // PROJ_TM, PROJ_NMAIN + PROJ_NGLU),
        in_specs=[
            pl.BlockSpec((PROJ_TM, D), lambda i, j: (i, 0)),
            pl.BlockSpec((1, 6, D), lambda i, j: (_mod_row(i * PROJ_TM), 0, 0)),
            pl.BlockSpec((1, D, PROJ_TN), lambda i, j: (layer, 0, main_j(j))),
            pl.BlockSpec((1, D, 128), lambda i, j: (layer, 0, low_rank)),
            pl.BlockSpec((1, D, GLU_TN), lambda i, j: (layer, 0, val0 + glu_j(j))),
            pl.BlockSpec((1, D, GLU_TN), lambda i, j: (layer, 0, gate0 + glu_j(j))),
        ],
        out_specs=(pl.BlockSpec((PROJ_TM, PROJ_TN), lambda i, j: (i, main_j(j))),
                   pl.BlockSpec((PROJ_TM, 128), lambda i, j: (i, 0)),
                   pl.BlockSpec((PROJ_TM, GLU_TN), lambda i, j: (i, glu_j(j)))),
        scratch_shapes=[pltpu.VMEM((PROJ_TM, D), BF16)],
        compiler_params=_params(2),
        name="inproj",
    )(x, mod_l, w_main, w_prep, w_prep, w_prep)


CONV_RC = 64
CONV_CC = 128
CONV_SPAN = (CONV_K + HALO - CONV_K // 2 + 7) // 8 * 8


def _conv_kernel(uc_ref, up_ref, un_ref, cw_ref, cb_ref, lg_ref, lb_ref, wp_ref, bp_ref,
                 o_ref, pad_scr, conv_scr):
    i = pl.program_id(0)
    s = jnp.maximum(i - NTILES_P, 0) % TILES_PER_S
    has_prev = jnp.logical_and(i >= NTILES_P, s != 0)
    has_next = jnp.logical_and(i >= NTILES_P, s != TILES_PER_S - 1)
    pad_scr[0:HALO, :] = jnp.where(has_prev, up_ref[...], 0.0)
    pad_scr[HALO:HALO + TILE, :] = uc_ref[...]
    pad_scr[HALO + TILE:HALO + TILE + HALO, :] = jnp.where(has_next, un_ref[...], 0.0)

    off = HALO - CONV_K // 2
    for c in range(CW // CONV_CC):
        cs = slice(c * CONV_CC, (c + 1) * CONV_CC)

        def body(r, carry, cs=cs):
            r0 = pl.multiple_of(r * CONV_RC, CONV_RC)
            win = pad_scr[pl.ds(r0, CONV_RC + CONV_SPAN), cs]
            acc = None
            for b in range(8):
                part = None
                for a in range((CONV_K + off) // 8 + 1):
                    k = 8 * a + b - off
                    if 0 <= k < CONV_K:
                        term = cw_ref[k:k + 1, cs] * win[8 * a:8 * a + CONV_RC + 8, :]
                        part = term if part is None else part + term
                part = part[b:b + CONV_RC, :]
                acc = part if acc is None else acc + part
            conv_scr[pl.ds(r0, CONV_RC), cs] = acc
            return carry

        lax.fori_loop(0, TILE // CONV_RC, body, 0)

    v = conv_scr[...] + cb_ref[...]
    y = _silu(_layer_norm(v, lg_ref[...], lb_ref[...]))
    o_ref[...] = (jnp.dot(y.astype(BF16), wp_ref[...], preferred_element_type=F32) + bp_ref[...]).astype(BF16)


def _conv_module(u, conv_w, conv_b, ln_g, ln_b, w_pw2, b_pw2):
    hb = TILE // HALO
    row = lambda a: a.reshape(1, CW)
    return pl.pallas_call(
        _conv_kernel,
        out_shape=jax.ShapeDtypeStruct((T, CW), BF16),
        grid=(NTILES,),
        in_specs=[
            pl.BlockSpec((TILE, CW), lambda i: (i, 0)),
            pl.BlockSpec((HALO, CW), lambda i: (jnp.maximum(i * hb - 1, 0), 0)),
            pl.BlockSpec((HALO, CW), lambda i: (jnp.minimum((i + 1) * hb, T // HALO - 1), 0)),
            pl.BlockSpec((CONV_K, CW), lambda i: (0, 0)),
            pl.BlockSpec((1, CW), lambda i: (0, 0)),
            pl.BlockSpec((1, CW), lambda i: (0, 0)),
            pl.BlockSpec((1, CW), lambda i: (0, 0)),
            pl.BlockSpec((CW, CW), lambda i: (0, 0)),
            pl.BlockSpec((1, CW), lambda i: (0, 0)),
        ],
        out_specs=pl.BlockSpec((TILE, CW), lambda i: (i, 0)),
        scratch_shapes=[pltpu.VMEM((TILE + 2 * HALO, CW), F32), pltpu.VMEM((TILE, CW), F32)],
        compiler_params=_params(1),
        name="conv_module",
    )(u, u, u, conv_w, row(conv_b), row(ln_g), row(ln_b), w_pw2, row(b_pw2))


def _split3(x):
    hi = x.astype(BF16)
    r1 = x - hi.astype(F32)
    mid = r1.astype(BF16)
    lo = (r1 - mid.astype(F32)).astype(BF16)
    return hi, mid, lo


def _log_sigmoid(x):
    return jnp.minimum(x, 0.0) - jnp.log(1.0 + jnp.exp(-jnp.abs(x)))


def _dot_nt(a, b):
    return lax.dot_general(a, b, (((1,), (1,)), ((), ())), preferred_element_type=F32)


def _dot_tn(a, b):
    return lax.dot_general(a, b, (((0,), (0,)), ((), ())), preferred_element_type=F32)


def _gla_kernel(q_ref, k_ref, v_ref, a_ref, wa2_ref, ba2_ref, tri_ref, s0_ref,
                o_ref, snew_ref, st_scr, *, reverse):
    n = pl.program_id(0)
    tile = (NTILES - 1 - n) if reverse else n
    is_prompt = tile < NTILES_P
    spos = jnp.maximum(tile - NTILES_P, 0) % TILES_PER_S
    seq_first = spos == (TILES_PER_S - 1 if reverse else 0)

    @pl.when(is_prompt)
    def _():
        st_scr[...] = jnp.zeros_like(st_scr)

    @pl.when(jnp.logical_and(jnp.logical_not(is_prompt), seq_first))
    def _():
        for h in range(H):
            st_scr[h] = s0_ref[0, 0, h].T

    tri = tri_ref[...]
    mask = tri > 0
    a_lr = a_ref[:, 0:2 * RANK].astype(BF16)
    n_chunks = TILE // CHUNK
    order = range(n_chunks - 1, -1, -1) if reverse else range(n_chunks)
    logits = jnp.dot(a_lr, wa2_ref[...], preferred_element_type=F32) + ba2_ref[...]
    hi, mid, lo = _split3(_log_sigmoid(logits) / TAU)
    b_all = (jnp.dot(tri, hi, preferred_element_type=F32)
             + jnp.dot(tri, mid, preferred_element_type=F32)
             + jnp.dot(tri, lo, preferred_element_type=F32))
    for h in range(H):
        ks = slice(h * DK, (h + 1) * DK)
        b = b_all[:, ks]
        q = q_ref[:, ks] * (DK ** -0.5)
        k = k_ref[:, ks]
        vb = v_ref[:, h * DV:(h + 1) * DV].astype(BF16)
        qd = (q * jnp.exp(b)).astype(BF16)
        kd = (k * jnp.exp(-b)).astype(BF16)
        att = jnp.where(mask, _dot_nt(qd, kd), 0.0).astype(BF16)
        o_intra = jnp.dot(att, vb, preferred_element_type=F32)
        st = st_scr[h]
        for c in order:
            rows = slice(c * CHUNK, (c + 1) * CHUNK)
            last = c * CHUNK if reverse else (c + 1) * CHUNK - 1
            b_last = b[last:last + 1, :]
            kl = (k[rows] * jnp.exp(b_last - b[rows])).astype(BF16)
            o_ref[rows, h * DV:(h + 1) * DV] = o_intra[rows] + _dot_nt(qd[rows], st.astype(BF16))
            st = jnp.exp(b_last) * st + _dot_tn(vb[rows], kl)
        st_scr[h] = st

    @pl.when(is_prompt)
    def _():
        for h in range(H):
            snew_ref[0, h] = st_scr[h].T


def _gla(z, a_lr, wa2_dir, ba2_dir, tri_dir, s0, direction):
    reverse = direction == 1
    tile = (lambda n: NTILES - 1 - n) if reverse else (lambda n: n)
    req = lambda n: jnp.clip((tile(n) - NTILES_P) // TILES_PER_S, 0, NB_S - 1)
    return pl.pallas_call(
        functools.partial(_gla_kernel, reverse=reverse),
        out_shape=(jax.ShapeDtypeStruct((T, GW), F32),
                   jax.ShapeDtypeStruct((NB_P, H, DK, DV), F32)),
        grid=(NTILES,),
        in_specs=[
            pl.BlockSpec((TILE, QC), lambda n: (tile(n), 0)),
            pl.BlockSpec((TILE, QC), lambda n: (tile(n), 1)),
            pl.BlockSpec((TILE, GW), lambda n: (tile(n), 1)),
            pl.BlockSpec((TILE, 128), lambda n: (tile(n), 0)),
            pl.BlockSpec((2 * RANK, QC), lambda n: (0, 0)),
            pl.BlockSpec((1, QC), lambda n: (0, 0)),
            pl.BlockSpec((TILE, TILE), lambda n: (0, 0)),
            pl.BlockSpec((1, 1, H, DK, DV), lambda n: (req(n), direction, 0, 0, 0)),
        ],
        out_specs=(pl.BlockSpec((TILE, GW), lambda n: (tile(n), 0)),
                   pl.BlockSpec((1, H, DK, DV), lambda n: (jnp.minimum(tile(n), NTILES_P - 1), 0, 0, 0))),
        scratch_shapes=[pltpu.VMEM((H, DV, DK), F32)],
        compiler_params=_params(1),
        name="gla_bwd" if reverse else "gla_fwd",
    )(z, z, z, a_lr, wa2_dir, ba2_dir, tri_dir, s0)


MIX_GROUPS = 1


def _mixout_kernel(of_ref, ob_ref, g_ref, u_ref, x_ref, mod_ref, gng_ref, wo_ref, l1g_ref, l1b_ref,
                   *rest, with_router):
    if with_router:
        wr_ref, br_ref, ltri_ref, x1_ref, h2_ref, route_ref, cnt_ref, run_scr = rest
    else:
        x1_ref, h2_ref = rest
    m = mod_ref[0]
    h2_parts = []
    for p in range(MIX_GROUPS):
        rs = slice(p * (TILE // MIX_GROUPS), (p + 1) * (TILE // MIX_GROUPS))
        o = of_ref[rs, :] + ob_ref[rs, :]
        parts = []
        for h in range(H):
            oh = o[:, h * DV:(h + 1) * DV]
            ms = jnp.mean(oh * oh, -1, keepdims=True)
            parts.append(oh * lax.rsqrt(ms + RMS_EPS) * gng_ref[...])
        on = jnp.concatenate(parts, axis=-1) * _silu(g_ref[rs, :])
        y = (jnp.dot(on.astype(BF16), wo_ref[0:GW, :], preferred_element_type=F32)
             + jnp.dot(u_ref[rs, :], wo_ref[GW:D, :], preferred_element_type=F32))
        x1 = _layer_norm(ALPHA * x_ref[rs, :] + m[G1:G1 + 1, :] * y, l1g_ref[...], l1b_ref[...])
        x1_ref[rs, :] = x1
        h2_part = x1 * (1.0 + m[SC2:SC2 + 1, :]) + m[SH2:SH2 + 1, :]
        h2_ref[rs, :] = h2_part.astype(h2_ref.dtype)
        h2_parts.append(h2_part)
    if with_router:
        h2 = jnp.concatenate(h2_parts, axis=0)
        h_hi = h2.astype(BF16)
        h_lo = (h2 - h_hi.astype(F32)).astype(BF16)
        p_hi = jnp.dot(h_hi, wr_ref[...], preferred_element_type=F32)
        p_lo = jnp.dot(h_lo, wr_ref[...], preferred_element_type=F32)
        logits = p_hi + pltpu.roll(p_hi, 128 - NE, 1) + p_lo + br_ref[...]
        lane = lax.broadcasted_iota(jnp.int32, logits.shape, 1)
        lg = jnp.where(lane < NE, logits, -jnp.inf)
        v1 = jnp.max(lg, -1, keepdims=True)
        i1 = jnp.min(jnp.where(lg == v1, lane, 128), -1, keepdims=True)
        lg2 = jnp.where(lane == i1, -jnp.inf, lg)
        v2 = jnp.max(lg2, -1, keepdims=True)
        i2 = jnp.min(jnp.where(lg2 == v2, lane, 128), -1, keepdims=True)
        e2 = jnp.exp(v2 - v1)
        w1 = 1.0 / (1.0 + e2)
        w2 = e2 / (1.0 + e2)
        @pl.when(pl.program_id(0) == 0)
        def _():
            run_scr[...] = jnp.zeros_like(run_scr)

        hit1 = lane == i1
        hit2 = lane == i2
        one1 = jnp.where(hit1, 1.0, 0.0)
        one2 = jnp.where(hit2, 1.0, 0.0)
        before1 = jnp.dot(ltri_ref[...], one1.astype(BF16), preferred_element_type=F32)
        before2 = jnp.dot(ltri_ref[...], one2.astype(BF16), preferred_element_type=F32)
        tot1 = jnp.sum(one1, axis=0, keepdims=True)
        tot2 = jnp.sum(one2, axis=0, keepdims=True)
        run = run_scr[...]
        rank1 = jnp.sum(jnp.where(hit1, before1 + run, 0.0), -1, keepdims=True)
        rank2 = jnp.sum(jnp.where(hit2, before2 + (run + tot1), 0.0), -1, keepdims=True)
        run = run + tot1 + tot2
        run_scr[...] = run
        cnt_ref[...] = jnp.broadcast_to(run, cnt_ref.shape)
        route_ref[...] = jnp.where(lane == 0, i1.astype(F32),
                         jnp.where(lane == 1, i2.astype(F32),
                         jnp.where(lane == 2, w1,
                         jnp.where(lane == 3, w2,
                         jnp.where(lane == 4, rank1, jnp.where(lane == 5, rank2, 0.0))))))


def _mixout(o_f, o_b, z, u2, x, mod_l, gng, w_out, ln_g, ln_b, router=None):
    row = lambda a: a.reshape(1, -1)
    full = lambda shape: pl.BlockSpec(shape, lambda i: (0,) * len(shape))
    in_specs = [
        pl.BlockSpec((TILE, GW), lambda i: (i, 0)),
        pl.BlockSpec((TILE, GW), lambda i: (i, 0)),
        pl.BlockSpec((TILE, GW), lambda i: (i, 2)),
        pl.BlockSpec((TILE, CW), lambda i: (i, 0)),
        pl.BlockSpec((TILE, D), lambda i: (i, 0)),
        pl.BlockSpec((1, 6, D), lambda i: (_mod_row(i * TILE), 0, 0)),
        full((1, DV)), full((D, D)), full((1, D)), full((1, D)),
    ]
    args = [o_f, o_b, z, u2, x, mod_l, row(gng), w_out, row(ln_g), row(ln_b)]
    out_shape = [jax.ShapeDtypeStruct((T, D), F32)]
    out_specs = [pl.BlockSpec((TILE, D), lambda i: (i, 0)), pl.BlockSpec((TILE, D), lambda i: (i, 0))]
    scratch = []
    if router is None:
        out_shape.append(jax.ShapeDtypeStruct((T, D), BF16))
    else:
        w_r, b_r = router
        idx = jnp.arange(TILE)
        ltri = (idx[None, :] < idx[:, None]).astype(BF16)
        in_specs += [full((D, 128)), full((1, 128)), full((TILE, TILE))]
        args += [w_r, b_r, ltri]
        out_shape += [jax.ShapeDtypeStruct((T, D), F32),
                      jax.ShapeDtypeStruct((T, 128), F32), jax.ShapeDtypeStruct((8, 128), F32)]
        out_specs += [pl.BlockSpec((TILE, 128), lambda i: (i, 0)), pl.BlockSpec((8, 128), lambda i: (0, 0))]
        scratch = [pltpu.VMEM((1, 128), F32)]
    return pl.pallas_call(
        functools.partial(_mixout_kernel, with_router=router is not None),
        out_shape=tuple(out_shape),
        grid=(NTILES,),
        in_specs=in_specs,
        out_specs=tuple(out_specs),
        scratch_shapes=scratch,
        compiler_params=_params(1),
        name="mixout",
    )(*args)


FFN_R = 1024
FFN_SUB = 256
FFN_TF = 512


def _ffn_kernel(te_ref, tn_ref, tb_ref, h_ref, w1_ref, w3_ref, w2_ref, o_ref, g_scr, a_scr, *h_scr, wide_rows):
    s = pl.program_id(0)
    j = pl.program_id(1)
    n = tn_ref[s]
    n_sub = (n + FFN_SUB - 1) // FFN_SUB

    @pl.when(jnp.logical_and(j == 0, n > 0))
    def _():
        o_ref[...] = jnp.zeros_like(o_ref)

    rows_ref = h_scr[0] if wide_rows else h_ref
    for k in range(1, FFN_R // FFN_SUB + 1):
        m = k * FFN_SUB

        if wide_rows:
            @pl.when(jnp.logical_and(n_sub == k, j == 0))
            def _(m=m):
                rows_ref[0:m, :] = h_ref[0:m, :].astype(BF16)

        @pl.when(n_sub == k)
        def _(m=m):
            hc = rows_ref[0:m, :]
            g_scr[0:m, :] = jnp.dot(hc, w1_ref[0].astype(BF16), preferred_element_type=F32)
            u = jnp.dot(hc, w3_ref[0].astype(BF16), preferred_element_type=F32)
            a_scr[0:m, :] = (_silu(g_scr[0:m, :]) * u).astype(BF16)
            o_ref[0:m, :] += jnp.dot(a_scr[0:m, :], w2_ref[0].astype(BF16), preferred_element_type=F32)


def _ffn(hs, tile_expert, tile_rows, tile_block, w1, w3, w2):
    n_tiles = tile_expert.shape[0]
    nj = FF // FFN_TF

    def jj(s, j, tn):
        return jnp.where(tn[s] > 0, j, nj - 1)

    once = pl.Buffered(1)
    wide_rows = hs.dtype == F32
    scratch = [pltpu.VMEM((FFN_R, FFN_TF), F32), pltpu.VMEM((FFN_R, FFN_TF), BF16)]
    if wide_rows:
        scratch.append(pltpu.VMEM((FFN_R, D), BF16))
    return pl.pallas_call(
        functools.partial(_ffn_kernel, wide_rows=wide_rows),
        out_shape=jax.ShapeDtypeStruct((hs.shape[0], D), F32),
        grid_spec=pltpu.PrefetchScalarGridSpec(
            num_scalar_prefetch=3,
            grid=(n_tiles, nj),
            in_specs=[
                pl.BlockSpec((FFN_R, hs.shape[1]), lambda s, j, te, tn, tb: (tb[s], 0), pipeline_mode=once),
                pl.BlockSpec((1, D, FFN_TF), lambda s, j, te, tn, tb: (te[s], 0, jj(s, j, tn))),
                pl.BlockSpec((1, D, FFN_TF), lambda s, j, te, tn, tb: (te[s], 0, jj(s, j, tn))),
                pl.BlockSpec((1, FFN_TF, D), lambda s, j, te, tn, tb: (te[s], jj(s, j, tn), 0)),
            ],
            out_specs=pl.BlockSpec((FFN_R, D), lambda s, j, te, tn, tb: (tb[s], 0), pipeline_mode=once),
            scratch_shapes=scratch,
        ),
        compiler_params=_params(2),
        name="ffn",
    )(tile_expert, tile_rows, tile_block, hs, w1, w3, w2)


DMA_UNROLL = 8


def _dispatch_kernel(pos_ref, pad_ref, h_ref, o_hbm, zero_scr, sem, zsem):
    i = pl.program_id(0)

    @pl.when(i == 0)
    def _():
        zero_scr[...] = jnp.zeros_like(zero_scr)
        for e in range(NE):
            first, count = pad_ref[e], pad_ref[NE + e]

            def zero_copy(r, first=first):
                return pltpu.make_async_copy(zero_scr, o_hbm.at[pl.ds(first + r, 1), :], zsem)

            def start(r, carry):
                zero_copy(r).start()
                return carry

            def wait(r, carry):
                zero_copy(r).wait()
                return carry

            lax.fori_loop(0, count, start, 0)
            lax.fori_loop(0, count, wait, 0)

    for r in range(TILE):
        for slot in range(2):
            p = pos_ref[2 * (i * TILE + r) + slot]
            pltpu.make_async_copy(h_ref.at[pl.ds(r, 1), :], o_hbm.at[pl.ds(p, 1), :], sem).start()
    for slot in range(2):
        pltpu.make_async_copy(h_ref, o_hbm.at[pl.ds(0, TILE), :], sem).wait()


def _dispatch(pos, pad, h2):
    return pl.pallas_call(
        _dispatch_kernel,
        out_shape=jax.ShapeDtypeStruct((MOE_TILES * FFN_R, D), F32),
        grid_spec=pltpu.PrefetchScalarGridSpec(
            num_scalar_prefetch=2,
            grid=(NTILES,),
            in_specs=[pl.BlockSpec((TILE, D), lambda i, pos, pad: (i, 0))],
            out_specs=pl.BlockSpec(memory_space=pl.ANY),
            scratch_shapes=[pltpu.VMEM((1, D), F32), pltpu.SemaphoreType.DMA, pltpu.SemaphoreType.DMA],
        ),
        compiler_params=_params(1),
        name="dispatch",
    )(pos, pad, h2)


def _ln2_kernel(x_ref, f_ref, mod_ref, g_ref, b_ref, o_ref):
    m = mod_ref[0]
    o_ref[...] = _layer_norm(ALPHA * x_ref[...] + m[G2:G2 + 1, :] * f_ref[...], g_ref[...], b_ref[...])


def _ln2(x1, f, mod_l, ln_g, ln_b):
    return pl.pallas_call(
        _ln2_kernel,
        out_shape=jax.ShapeDtypeStruct((T, D), F32),
        grid=(NTILES,),
        in_specs=[
            pl.BlockSpec((TILE, D), lambda i: (i, 0)),
            pl.BlockSpec((TILE, D), lambda i: (i, 0)),
            pl.BlockSpec((1, 6, D), lambda i: (_mod_row(i * TILE), 0, 0)),
            pl.BlockSpec((1, D), lambda i: (0, 0)),
            pl.BlockSpec((1, D), lambda i: (0, 0)),
        ],
        out_specs=pl.BlockSpec((TILE, D), lambda i: (i, 0)),
        compiler_params=_params(1),
        name="ln2",
    )(x1, f, mod_l, ln_g.reshape(1, D), ln_b.reshape(1, D))


def _combine_kernel(pos_ref, x_ref, route_ref, mod_ref, g_ref, b_ref, y_hbm, op_ref, os_ref, buf_a, buf_b, sem):
    i = pl.program_id(0)
    bufs = (buf_a, buf_b)

    def row_copy(tile, r, slot, dst, dsem):
        p = pos_ref[2 * (tile * TILE + r) + slot]
        return pltpu.make_async_copy(y_hbm.at[pl.ds(p, 1), :], dst.at[slot, pl.ds(r, 1), :], dsem)

    def wait_rows(dst, dsem):
        for slot in range(2):
            pltpu.make_async_copy(y_hbm.at[pl.ds(0, TILE), :], dst.at[slot], dsem).wait()

    @pl.when(i == 0)
    def _():
        def issue(r, carry):
            for slot in range(2):
                row_copy(0, r, slot, buf_a, sem.at[0]).start()
            return carry

        lax.fori_loop(0, TILE, issue, 0, unroll=DMA_UNROLL)

    for par in range(2):
        @pl.when(i % 2 == par)
        def _(par=par):
            cur, nxt = bufs[par], bufs[1 - par]
            wait_rows(cur, sem.at[par])
            nxt_tile = jnp.minimum(i + 1, NTILES - 1)
            for r in range(TILE):
                for slot in range(2):
                    row_copy(nxt_tile, r, slot, nxt, sem.at[1 - par]).start()
            route = route_ref[...]
            f = route[:, 2:3] * cur[0] + route[:, 3:4] * cur[1]
            m = mod_ref[0]
            out = _layer_norm(ALPHA * x_ref[...] + m[G2:G2 + 1, :] * f, g_ref[...], b_ref[...])

            @pl.when(i < NTILES_P)
            def _():
                op_ref[...] = out

            @pl.when(i >= NTILES_P)
            def _():
                os_ref[...] = out

            @pl.when(i == NTILES - 1)
            def _():
                wait_rows(nxt, sem.at[1 - par])


def _combine(pos, x1, route, mod_l, ln_g, ln_b, y):
    return pl.pallas_call(
        _combine_kernel,
        out_shape=(jax.ShapeDtypeStruct((TP, D), F32), jax.ShapeDtypeStruct((T - TP, D), F32)),
        grid_spec=pltpu.PrefetchScalarGridSpec(
            num_scalar_prefetch=1,
            grid=(NTILES,),
            in_specs=[
                pl.BlockSpec((TILE, D), lambda i, pos: (i, 0)),
                pl.BlockSpec((TILE, 128), lambda i, pos: (i, 0)),
                pl.BlockSpec((1, 6, D), lambda i, pos: (_mod_row(i * TILE), 0, 0)),
                pl.BlockSpec((1, D), lambda i, pos: (0, 0)),
                pl.BlockSpec((1, D), lambda i, pos: (0, 0)),
                pl.BlockSpec(memory_space=pl.ANY),
            ],
            out_specs=(pl.BlockSpec((TILE, D), lambda i, pos: (jnp.minimum(i, NTILES_P - 1), 0)),
                       pl.BlockSpec((TILE, D), lambda i, pos: (jnp.maximum(i - NTILES_P, 0), 0))),
            scratch_shapes=[pltpu.VMEM((2, TILE, D), F32), pltpu.VMEM((2, TILE, D), F32),
                            pltpu.SemaphoreType.DMA((2,))],
        ),
        compiler_params=_params(1),
        name="combine",
    )(pos, x1, route, mod_l, ln_g.reshape(1, D), ln_b.reshape(1, D), y)


MOE_TILES = 2 * T // FFN_R + NE


def _routing_tables(route, counts_f):
    eidx = route[:, 0:2].astype(jnp.int32)
    rank = route[:, 4:6].astype(jnp.int32)
    counts = counts_f[0, :NE].astype(jnp.int32)
    ntiles = (counts + FFN_R - 1) // FFN_R
    per_tile = (counts + jnp.maximum(ntiles, 1) - 1) // jnp.maximum(ntiles, 1)
    per_tile = (per_tile + FFN_SUB - 1) // FFN_SUB * FFN_SUB
    tend = jnp.cumsum(ntiles)
    tstart = tend - ntiles
    experts = jnp.arange(NE, dtype=jnp.int32)
    pick = lambda table: jnp.sum(jnp.where(eidx[..., None] == experts, table, 0), axis=-1)
    rows_e = pick(per_tile)
    local = sum((rank >= k * rows_e).astype(jnp.int32) for k in range(1, T // FFN_R))
    pos = ((pick(tstart) + local) * FFN_R + (rank - local * rows_e)).reshape(-1)
    tiles = jnp.arange(MOE_TILES, dtype=jnp.int32)
    total = tend[-1]
    t_eff = jnp.minimum(tiles, total - 1)
    tile_expert = jnp.minimum(jnp.sum((t_eff[:, None] >= tend[None, :]).astype(jnp.int32), axis=1), NE - 1)
    tile_rows = jnp.clip(counts[tile_expert] - (t_eff - tstart[tile_expert]) * per_tile[tile_expert],
                         0, per_tile[tile_expert])
    tile_rows = jnp.where(tiles < total, tile_rows, 0)
    last_rows = counts - (ntiles - 1) * per_tile
    pad = jnp.concatenate([(tend - 1) * FFN_R + last_rows, (-last_rows) % FFN_SUB]).astype(jnp.int32)
    return pos, pad, tile_expert.astype(jnp.int32), tile_rows.astype(jnp.int32), t_eff


def _grid_pos_emb():
    rows = L_S // GRID_W
    r = np.repeat(np.arange(rows, dtype=np.float64), GRID_W)
    col = np.tile(np.arange(GRID_W, dtype=np.float64), rows)
    quarter = D // 4
    freqs = 1.0 / (10000.0 ** (np.arange(quarter, dtype=np.float64) / quarter))

    def enc(p):
        ang = p[:, None] * freqs[None, :]
        return np.concatenate([np.sin(ang), np.cos(ang)], -1)

    return jnp.asarray(np.concatenate([enc(r), enc(col)], -1), dtype=F32)


def _cumsum_matrices():
    i = jnp.arange(TILE)
    same = (i[:, None] // CHUNK) == (i[None, :] // CHUNK)
    fwd = jnp.logical_and(same, i[None, :] <= i[:, None])
    bwd = jnp.logical_and(same, i[None, :] >= i[:, None])
    return fwd.astype(BF16), bwd.astype(BF16)


def kernel(x_prompt, x_sample, state_gla, c, c_ctx, w_mod, b_mod, w_in, w_a2, b_a2, gla_norm_g, conv_w, conv_b, conv_ln_g, conv_ln_b, w_pw2, b_pw2, w_out, ln1_g, ln1_b, ln2_g, ln2_b, ffn_w1, ffn_w3, ffn_w2, moe_w_router, moe_b_router, moe_w1, moe_w3, moe_w2):
    x = _assemble(x_prompt.reshape(TP, D), x_sample.reshape(T - TP, D), _grid_pos_emb())
    cond8 = jnp.zeros((8, D), F32).at[0].set(c_ctx).at[1:1 + NB_S].set(c)
    mod = _modulation(cond8, w_mod, b_mod)
    tri = _cumsum_matrices()
    w_bf = w_in.astype(BF16)
    w_main = w_bf[:, :, :MAIN_COLS]
    w_prep = jnp.concatenate(
        [w_bf[:, :, MAIN_COLS + 2 * RANK:],
         jnp.pad(w_bf[:, :, MAIN_COLS:MAIN_COLS + 2 * RANK], ((0, 0), (0, 0), (0, 128 - 2 * RANK)))], axis=-1)
    dense_tiles = T // FFN_R
    dense_meta = (jnp.zeros((dense_tiles,), jnp.int32), jnp.full((dense_tiles,), FFN_R, jnp.int32),
                  jnp.arange(dense_tiles, dtype=jnp.int32))

    states = []
    for l in range(DEPTH):
        mod_l = mod[l]
        z, a_lr, u = _inproj(x, mod_l, w_main, w_prep, l)
        u2 = _conv_module(u, conv_w[l], conv_b[l], conv_ln_g[l], conv_ln_b[l],
                          w_pw2[l].astype(BF16), b_pw2[l])
        o_dir, s_dir = [], []
        for d in range(2):
            wa2 = jnp.zeros((2 * RANK, QC), F32).at[d * RANK:(d + 1) * RANK].set(w_a2[l, d]).astype(BF16)
            o, s_new = _gla(z, a_lr, wa2, b_a2[l, d].reshape(1, QC), tri[d], state_gla[:, l], d)
            o_dir.append(o)
            s_dir.append(s_new)
        states.append(jnp.stack(s_dir, axis=1))
        is_moe = l % 2 == 1
        i = l // 2
        router = None
        if is_moe:
            wr_hi = moe_w_router[i].astype(BF16)
            wr_lo = (moe_w_router[i] - wr_hi.astype(F32)).astype(BF16)
            router = (jnp.pad(jnp.concatenate([wr_hi, wr_lo], axis=1), ((0, 0), (0, 128 - 2 * NE))),
                      jnp.pad(moe_b_router[i], (0, 128 - NE)).reshape(1, 128))
        outs = _mixout(o_dir[0], o_dir[1], z, u2, x, mod_l, gla_norm_g[l], w_out[l].astype(BF16),
                       ln1_g[l], ln1_b[l], router)
        if is_moe:
            x1, h2, route, counts = outs
            pos, pad, t_exp, t_rows, t_blk = _routing_tables(route, counts)
            hs = _dispatch(pos, pad, h2)
            y = _ffn(hs, t_exp, t_rows, t_blk, moe_w1[i], moe_w3[i], moe_w2[i])
            out_p, out_s = _combine(pos, x1, route, mod_l, ln2_g[l], ln2_b[l], y)
            x = jnp.concatenate([out_p, out_s], axis=0) if l + 1 < DEPTH else None
        else:
            x1, h2 = outs
            y = _ffn(h2, *dense_meta, ffn_w1[i][None], ffn_w3[i][None], ffn_w2[i][None])
            x = _ln2(x1, y, mod_l, ln2_g[l], ln2_b[l])

    if x is not None:
        out_p, out_s = x[:TP], x[TP:]
    return out_p.reshape(NB_P, L_P, D), out_s.reshape(NB_S, L_S, D), jnp.stack(states, axis=1)
```

```python
import functools

import jax
import jax.numpy as jnp
import numpy as np
from jax import lax
from jax.experimental import pallas as pl
from jax.experimental.pallas import tpu as pltpu

F32 = jnp.float32
BF16 = jnp.bfloat16

D = 2048
NB_P, L_P = 16, 256
NB_S, L_S = 4, 1024
TP = NB_P * L_P
T = TP + NB_S * L_S
DEPTH = 2
GRID_W = 64
GW = D // 2
CW = D - GW
H = 4
DV = GW // H
DK = DV // 2
QC = H * DK
RANK = 16
TAU = 16.0
CHUNK = 64
CONV_K = 31
FF = 7 * D // 2
NE = 8
ALPHA = (2 * DEPTH) ** 0.25
LN_EPS = 1e-5
RMS_EPS = 1e-6
MAIN_COLS = 2 * QC + 2 * GW

TILE = 256
NTILES = T // TILE
NTILES_P = TP // TILE
TILES_PER_S = L_S // TILE
HALO = 16

VMEM_LIMIT = 56 * 1024 * 1024

SH1, SC1, G1, SH2, SC2, G2 = range(6)


def _params(n_axes, vmem=VMEM_LIMIT):
    return pltpu.CompilerParams(dimension_semantics=("arbitrary",) * n_axes,
                                vmem_limit_bytes=vmem)


def _mod_row(tok0):
    return jnp.where(tok0 < TP, 0, 1 + (tok0 - TP) // L_S)


def _sigmoid(x):
    return 1.0 / (1.0 + jnp.exp(-x))


def _silu(x):
    return x * _sigmoid(x)


def _layer_norm(r, g, b):
    mu = jnp.mean(r, -1, keepdims=True)
    rc = r - mu
    var = jnp.mean(rc * rc, -1, keepdims=True)
    return rc * lax.rsqrt(var + LN_EPS) * g + b


def _assemble_kernel(xp_ref, xs_ref, pos_ref, o_ref):
    i = pl.program_id(0)

    @pl.when(i < NTILES_P)
    def _():
        o_ref[...] = xp_ref[...]

    @pl.when(i >= NTILES_P)
    def _():
        o_ref[...] = xs_ref[...] + pos_ref[...]


def _assemble(xp, xs, pos):
    return pl.pallas_call(
        _assemble_kernel,
        out_shape=jax.ShapeDtypeStruct((T, D), F32),
        grid=(NTILES,),
        in_specs=[
            pl.BlockSpec((TILE, D), lambda i: (jnp.minimum(i, NTILES_P - 1), 0)),
            pl.BlockSpec((TILE, D), lambda i: (jnp.maximum(i - NTILES_P, 0), 0)),
            pl.BlockSpec((TILE, D), lambda i: (jnp.maximum(i - NTILES_P, 0) % TILES_PER_S, 0)),
        ],
        out_specs=pl.BlockSpec((TILE, D), lambda i: (i, 0)),
        compiler_params=_params(1),
        name="assemble",
    )(xp, xs, pos)


MOD_TN = 1024


def _mod_kernel(c_ref, w_ref, b_ref, o_ref):
    s = _silu(c_ref[...])
    s_hi = s.astype(BF16)
    s_lo = (s - s_hi.astype(F32)).astype(BF16)
    w = w_ref[0]
    w_hi = w.astype(BF16)
    w_lo = (w - w_hi.astype(F32)).astype(BF16)
    o_ref[0] = (jnp.dot(s_hi, w_hi, preferred_element_type=F32)
                + jnp.dot(s_hi, w_lo, preferred_element_type=F32)
                + jnp.dot(s_lo, w_hi, preferred_element_type=F32)) + b_ref[0]


def _modulation(cond8, w_mod, b_mod):
    out = pl.pallas_call(
        _mod_kernel,
        out_shape=jax.ShapeDtypeStruct((DEPTH, 8, 6 * D), F32),
        grid=(DEPTH, 6 * D // MOD_TN),
        in_specs=[
            pl.BlockSpec((8, D), lambda l, j: (0, 0)),
            pl.BlockSpec((1, D, MOD_TN), lambda l, j: (l, 0, j)),
            pl.BlockSpec((1, 1, MOD_TN), lambda l, j: (l, 0, j)),
        ],
        out_specs=pl.BlockSpec((1, 8, MOD_TN), lambda l, j: (l, 0, j)),
        compiler_params=_params(2),
        name="modulation",
    )(cond8, w_mod, b_mod.reshape(DEPTH, 1, 6 * D))
    return out.reshape(DEPTH, 8, 6, D)


PROJ_TM = 1024
PROJ_TN = 1024
PROJ_NMAIN = MAIN_COLS // PROJ_TN
GLU_TN = 512
PROJ_NGLU = CW // GLU_TN

def _modulated(x_ref, mod_ref, shift, scale):
    m = mod_ref[0]
    return x_ref[...] * (1.0 + m[scale:scale + 1, :]) + m[shift:shift + 1, :]


def _inproj_kernel(x_ref, mod_ref, w_ref, wa_ref, wua_ref, wug_ref, z_ref, a_ref, u_ref, h_scr):
    j = pl.program_id(1)

    @pl.when(j == 0)
    def _():
        hb = _modulated(x_ref, mod_ref, SH1, SC1).astype(BF16)
        h_scr[...] = hb
        a_ref[...] = jnp.dot(hb, wa_ref[0], preferred_element_type=F32)

    @pl.when(j < PROJ_NMAIN)
    def _():
        z_ref[...] = jnp.dot(h_scr[...], w_ref[0], preferred_element_type=F32)

    @pl.when(j >= PROJ_NMAIN)
    def _():
        h = h_scr[...]
        a = jnp.dot(h, wua_ref[0], preferred_element_type=F32)
        g = jnp.dot(h, wug_ref[0], preferred_element_type=F32)
        u_ref[...] = a * _sigmoid(g)


def _inproj(x, mod_l, w_main, w_prep, layer):
    main_j = lambda j: jnp.minimum(j, PROJ_NMAIN - 1)
    glu_j = lambda j: jnp.maximum(j - PROJ_NMAIN, 0)
    val0 = 0
    gate0 = CW // GLU_TN
    low_rank = 2 * CW // 128
    return pl.pallas_call(
        _inproj_kernel,
        out_shape=(jax.ShapeDtypeStruct((T, MAIN_COLS), F32),
                   jax.ShapeDtypeStruct((T, 128), F32),
                   jax.ShapeDtypeStruct((T, CW), F32)),
        grid=(T // PROJ_TM, PROJ_NMAIN + PROJ_NGLU),
        in_specs=[
            pl.BlockSpec((PROJ_TM, D), lambda i, j: (i, 0)),
            pl.BlockSpec((1, 6, D), lambda i, j: (_mod_row(i * PROJ_TM), 0, 0)),
            pl.BlockSpec((1, D, PROJ_TN), lambda i, j: (layer, 0, main_j(j))),
            pl.BlockSpec((1, D, 128), lambda i, j: (layer, 0, low_rank)),
            pl.BlockSpec((1, D, GLU_TN), lambda i, j: (layer, 0, val0 + glu_j(j))),
            pl.BlockSpec((1, D, GLU_TN), lambda i, j: (layer, 0, gate0 + glu_j(j))),
        ],
        out_specs=(pl.BlockSpec((PROJ_TM, PROJ_TN), lambda i, j: (i, main_j(j))),
                   pl.BlockSpec((PROJ_TM, 128), lambda i, j: (i, 0)),
                   pl.BlockSpec((PROJ_TM, GLU_TN), lambda i, j: (i, glu_j(j)))),
        scratch_shapes=[pltpu.VMEM((PROJ_TM, D), BF16)],
        compiler_params=_params(2),
        name="inproj",
    )(x, mod_l, w_main, w_prep, w_prep, w_prep)


CONV_RC = 64
CONV_CC = 128
CONV_SPAN = (CONV_K + HALO - CONV_K // 2 + 7) // 8 * 8


def _conv_kernel(uc_ref, up_ref, un_ref, cw_ref, cb_ref, lg_ref, lb_ref, wp_ref, bp_ref,
                 o_ref, pad_scr, conv_scr):
    i = pl.program_id(0)
    s = jnp.maximum(i - NTILES_P, 0) % TILES_PER_S
    has_prev = jnp.logical_and(i >= NTILES_P, s != 0)
    has_next = jnp.logical_and(i >= NTILES_P, s != TILES_PER_S - 1)
    pad_scr[0:HALO, :] = jnp.where(has_prev, up_ref[...], 0.0)
    pad_scr[HALO:HALO + TILE, :] = uc_ref[...]
    pad_scr[HALO + TILE:HALO + TILE + HALO, :] = jnp.where(has_next, un_ref[...], 0.0)

    off = HALO - CONV_K // 2
    for c in range(CW // CONV_CC):
        cs = slice(c * CONV_CC, (c + 1) * CONV_CC)

        def body(r, carry, cs=cs):
            r0 = pl.multiple_of(r * CONV_RC, CONV_RC)
            win = pad_scr[pl.ds(r0, CONV_RC + CONV_SPAN), cs]
            acc = None
            for b in range(8):
                part = None
                for a in range((CONV_K + off) // 8 + 1):
                    k = 8 * a + b - off
                    if 0 <= k < CONV_K:
                        term = cw_ref[k:k + 1, cs] * win[8 * a:8 * a + CONV_RC + 8, :]
                        part = term if part is None else part + term
                part = part[b:b + CONV_RC, :]
                acc = part if acc is None else acc + part
            conv_scr[pl.ds(r0, CONV_RC), cs] = acc
            return carry

        lax.fori_loop(0, TILE // CONV_RC, body, 0)

    v = conv_scr[...] + cb_ref[...]
    y = _silu(_layer_norm(v, lg_ref[...], lb_ref[...]))
    o_ref[...] = (jnp.dot(y.astype(BF16), wp_ref[...], preferred_element_type=F32) + bp_ref[...]).astype(BF16)


def _conv_module(u, conv_w, conv_b, ln_g, ln_b, w_pw2, b_pw2):
    hb = TILE // HALO
    row = lambda a: a.reshape(1, CW)
    return pl.pallas_call(
        _conv_kernel,
        out_shape=jax.ShapeDtypeStruct((T, CW), BF16),
        grid=(NTILES,),
        in_specs=[
            pl.BlockSpec((TILE, CW), lambda i: (i, 0)),
            pl.BlockSpec((HALO, CW), lambda i: (jnp.maximum(i * hb - 1, 0), 0)),
            pl.BlockSpec((HALO, CW), lambda i: (jnp.minimum((i + 1) * hb, T // HALO - 1), 0)),
            pl.BlockSpec((CONV_K, CW), lambda i: (0, 0)),
            pl.BlockSpec((1, CW), lambda i: (0, 0)),
            pl.BlockSpec((1, CW), lambda i: (0, 0)),
            pl.BlockSpec((1, CW), lambda i: (0, 0)),
            pl.BlockSpec((CW, CW), lambda i: (0, 0)),
            pl.BlockSpec((1, CW), lambda i: (0, 0)),
        ],
        out_specs=pl.BlockSpec((TILE, CW), lambda i: (i, 0)),
        scratch_shapes=[pltpu.VMEM((TILE + 2 * HALO, CW), F32), pltpu.VMEM((TILE, CW), F32)],
        compiler_params=_params(1),
        name="conv_module",
    )(u, u, u, conv_w, row(conv_b), row(ln_g), row(ln_b), w_pw2, row(b_pw2))


def _split3(x):
    hi = x.astype(BF16)
    r1 = x - hi.astype(F32)
    mid = r1.astype(BF16)
    lo = (r1 - mid.astype(F32)).astype(BF16)
    return hi, mid, lo


def _log_sigmoid(x):
    return jnp.minimum(x, 0.0) - jnp.log(1.0 + jnp.exp(-jnp.abs(x)))


def _dot_nt(a, b):
    return lax.dot_general(a, b, (((1,), (1,)), ((), ())), preferred_element_type=F32)


def _dot_tn(a, b):
    return lax.dot_general(a, b, (((0,), (0,)), ((), ())), preferred_element_type=F32)


def _gla_kernel(q_ref, k_ref, v_ref, a_ref, wa2_ref, ba2_ref, tri_ref, s0_ref,
                o_ref, snew_ref, st_scr, *, reverse):
    n = pl.program_id(0)
    tile = (NTILES - 1 - n) if reverse else n
    is_prompt = tile < NTILES_P
    spos = jnp.maximum(tile - NTILES_P, 0) % TILES_PER_S
    seq_first = spos == (TILES_PER_S - 1 if reverse else 0)

    @pl.when(is_prompt)
    def _():
        st_scr[...] = jnp.zeros_like(st_scr)

    @pl.when(jnp.logical_and(jnp.logical_not(is_prompt), seq_first))
    def _():
        for h in range(H):
            st_scr[h] = s0_ref[0, 0, h].T

    tri = tri_ref[...]
    mask = tri > 0
    a_lr = a_ref[:, 0:2 * RANK].astype(BF16)
    n_chunks = TILE // CHUNK
    order = range(n_chunks - 1, -1, -1) if reverse else range(n_chunks)
    logits = jnp.dot(a_lr, wa2_ref[...], preferred_element_type=F32) + ba2_ref[...]
    hi, mid, lo = _split3(_log_sigmoid(logits) / TAU)
    b_all = (jnp.dot(tri, hi, preferred_element_type=F32)
             + jnp.dot(tri, mid, preferred_element_type=F32)
             + jnp.dot(tri, lo, preferred_element_type=F32))
    for h in range(H):
        ks = slice(h * DK, (h + 1) * DK)
        b = b_all[:, ks]
        q = q_ref[:, ks] * (DK ** -0.5)
        k = k_ref[:, ks]
        vb = v_ref[:, h * DV:(h + 1) * DV].astype(BF16)
        qd = (q * jnp.exp(b)).astype(BF16)
        kd = (k * jnp.exp(-b)).astype(BF16)
        att = jnp.where(mask, _dot_nt(qd, kd), 0.0).astype(BF16)
        o_intra = jnp.dot(att, vb, preferred_element_type=F32)
        st = st_scr[h]
        for c in order:
            rows = slice(c * CHUNK, (c + 1) * CHUNK)
            last = c * CHUNK if reverse else (c + 1) * CHUNK - 1
            b_last = b[last:last + 1, :]
            kl = (k[rows] * jnp.exp(b_last - b[rows])).astype(BF16)
            o_ref[rows, h * DV:(h + 1) * DV] = o_intra[rows] + _dot_nt(qd[rows], st.astype(BF16))
            st = jnp.exp(b_last) * st + _dot_tn(vb[rows], kl)
        st_scr[h] = st

    @pl.when(is_prompt)
    def _():
        for h in range(H):
            snew_ref[0, h] = st_scr[h].T


def _gla(z, a_lr, wa2_dir, ba2_dir, tri_dir, s0, direction):
    reverse = direction == 1
    tile = (lambda n: NTILES - 1 - n) if reverse else (lambda n: n)
    req = lambda n: jnp.clip((tile(n) - NTILES_P) // TILES_PER_S, 0, NB_S - 1)
    return pl.pallas_call(
        functools.partial(_gla_kernel, reverse=reverse),
        out_shape=(jax.ShapeDtypeStruct((T, GW), F32),
                   jax.ShapeDtypeStruct((NB_P, H, DK, DV), F32)),
        grid=(NTILES,),
        in_specs=[
            pl.BlockSpec((TILE, QC), lambda n: (tile(n), 0)),
            pl.BlockSpec((TILE, QC), lambda n: (tile(n), 1)),
            pl.BlockSpec((TILE, GW), lambda n: (tile(n), 1)),
            pl.BlockSpec((TILE, 128), lambda n: (tile(n), 0)),
            pl.BlockSpec((2 * RANK, QC), lambda n: (0, 0)),
            pl.BlockSpec((1, QC), lambda n: (0, 0)),
            pl.BlockSpec((TILE, TILE), lambda n: (0, 0)),
            pl.BlockSpec((1, 1, H, DK, DV), lambda n: (req(n), direction, 0, 0, 0)),
        ],
        out_specs=(pl.BlockSpec((TILE, GW), lambda n: (tile(n), 0)),
                   pl.BlockSpec((1, H, DK, DV), lambda n: (jnp.minimum(tile(n), NTILES_P - 1), 0, 0, 0))),
        scratch_shapes=[pltpu.VMEM((H, DV, DK), F32)],
        compiler_params=_params(1),
        name="gla_bwd" if reverse else "gla_fwd",
    )(z, z, z, a_lr, wa2_dir, ba2_dir, tri_dir, s0)


MIX_GROUPS = 1


def _mixout_kernel(of_ref, ob_ref, g_ref, u_ref, x_ref, mod_ref, gng_ref, wo_ref, l1g_ref, l1b_ref,
                   *rest, with_router):
    if with_router:
        wr_ref, br_ref, ltri_ref, x1_ref, h2_ref, route_ref, cnt_ref, run_scr = rest
    else:
        x1_ref, h2_ref = rest
    m = mod_ref[0]
    h2_parts = []
    for p in range(MIX_GROUPS):
        rs = slice(p * (TILE // MIX_GROUPS), (p + 1) * (TILE // MIX_GROUPS))
        o = of_ref[rs, :] + ob_ref[rs, :]
        parts = []
        for h in range(H):
            oh = o[:, h * DV:(h + 1) * DV]
            ms = jnp.mean(oh * oh, -1, keepdims=True)
            parts.append(oh * lax.rsqrt(ms + RMS_EPS) * gng_ref[...])
        on = jnp.concatenate(parts, axis=-1) * _silu(g_ref[rs, :])
        y = (jnp.dot(on.astype(BF16), wo_ref[0:GW, :], preferred_element_type=F32)
             + jnp.dot(u_ref[rs, :], wo_ref[GW:D, :], preferred_element_type=F32))
        x1 = _layer_norm(ALPHA * x_ref[rs, :] + m[G1:G1 + 1, :] * y, l1g_ref[...], l1b_ref[...])
        x1_ref[rs, :] = x1
        h2_part = x1 * (1.0 + m[SC2:SC2 + 1, :]) + m[SH2:SH2 + 1, :]
        h2_ref[rs, :] = h2_part.astype(h2_ref.dtype)
        h2_parts.append(h2_part)
    if with_router:
        h2 = jnp.concatenate(h2_parts, axis=0)
        h_hi = h2.astype(BF16)
        h_lo = (h2 - h_hi.astype(F32)).astype(BF16)
        p_hi = jnp.dot(h_hi, wr_ref[...], preferred_element_type=F32)
        p_lo = jnp.dot(h_lo, wr_ref[...], preferred_element_type=F32)
        logits = p_hi + pltpu.roll(p_hi, 128 - NE, 1) + p_lo + br_ref[...]
        lane = lax.broadcasted_iota(jnp.int32, logits.shape, 1)
        lg = jnp.where(lane < NE, logits, -jnp.inf)
        v1 = jnp.max(lg, -1, keepdims=True)
        i1 = jnp.min(jnp.where(lg == v1, lane, 128), -1, keepdims=True)
        lg2 = jnp.where(lane == i1, -jnp.inf, lg)
        v2 = jnp.max(lg2, -1, keepdims=True)
        i2 = jnp.min(jnp.where(lg2 == v2, lane, 128), -1, keepdims=True)
        e2 = jnp.exp(v2 - v1)
        w1 = 1.0 / (1.0 + e2)
        w2 = e2 / (1.0 + e2)
        @pl.when(pl.program_id(0) == 0)
        def _():
            run_scr[...] = jnp.zeros_like(run_scr)

        hit1 = lane == i1
        hit2 = lane == i2
        one1 = jnp.where(hit1, 1.0, 0.0)
        one2 = jnp.where(hit2, 1.0, 0.0)
        before1 = jnp.dot(ltri_ref[...], one1.astype(BF16), preferred_element_type=F32)
        before2 = jnp.dot(ltri_ref[...], one2.astype(BF16), preferred_element_type=F32)
        tot1 = jnp.sum(one1, axis=0, keepdims=True)
        tot2 = jnp.sum(one2, axis=0, keepdims=True)
        run = run_scr[...]
        rank1 = jnp.sum(jnp.where(hit1, before1 + run, 0.0), -1, keepdims=True)
        rank2 = jnp.sum(jnp.where(hit2, before2 + (run + tot1), 0.0), -1, keepdims=True)
        run = run + tot1 + tot2
        run_scr[...] = run
        cnt_ref[...] = jnp.broadcast_to(run, cnt_ref.shape)
        route_ref[...] = jnp.where(lane == 0, i1.astype(F32),
                         jnp.where(lane == 1, i2.astype(F32),
                         jnp.where(lane == 2, w1,
                         jnp.where(lane == 3, w2,
                         jnp.where(lane == 4, rank1, jnp.where(lane == 5, rank2, 0.0))))))


def _mixout(o_f, o_b, z, u2, x, mod_l, gng, w_out, ln_g, ln_b, router=None):
    row = lambda a: a.reshape(1, -1)
    full = lambda shape: pl.BlockSpec(shape, lambda i: (0,) * len(shape))
    in_specs = [
        pl.BlockSpec((TILE, GW), lambda i: (i, 0)),
        pl.BlockSpec((TILE, GW), lambda i: (i, 0)),
        pl.BlockSpec((TILE, GW), lambda i: (i, 2)),
        pl.BlockSpec((TILE, CW), lambda i: (i, 0)),
        pl.BlockSpec((TILE, D), lambda i: (i, 0)),
        pl.BlockSpec((1, 6, D), lambda i: (_mod_row(i * TILE), 0, 0)),
        full((1, DV)), full((D, D)), full((1, D)), full((1, D)),
    ]
    args = [o_f, o_b, z, u2, x, mod_l, row(gng), w_out, row(ln_g), row(ln_b)]
    out_shape = [jax.ShapeDtypeStruct((T, D), F32)]
    out_specs = [pl.BlockSpec((TILE, D), lambda i: (i, 0)), pl.BlockSpec((TILE, D), lambda i: (i, 0))]
    scratch = []
    if router is None:
        out_shape.append(jax.ShapeDtypeStruct((T, D), BF16))
    else:
        w_r, b_r = router
        idx = jnp.arange(TILE)
        ltri = (idx[None, :] < idx[:, None]).astype(BF16)
        in_specs += [full((D, 128)), full((1, 128)), full((TILE, TILE))]
        args += [w_r, b_r, ltri]
        out_shape += [jax.ShapeDtypeStruct((T, D), F32),
                      jax.ShapeDtypeStruct((T, 128), F32), jax.ShapeDtypeStruct((8, 128), F32)]
        out_specs += [pl.BlockSpec((TILE, 128), lambda i: (i, 0)), pl.BlockSpec((8, 128), lambda i: (0, 0))]
        scratch = [pltpu.VMEM((1, 128), F32)]
    return pl.pallas_call(
        functools.partial(_mixout_kernel, with_router=router is not None),
        out_shape=tuple(out_shape),
        grid=(NTILES,),
        in_specs=in_specs,
        out_specs=tuple(out_specs),
        scratch_shapes=scratch,
        compiler_params=_params(1),
        name="mixout",
    )(*args)


FFN_R = 1024
FFN_SUB = 256
FFN_TF = 512


def _ffn_kernel(te_ref, tn_ref, tb_ref, h_ref, w1_ref, w3_ref, w2_ref, o_ref, g_scr, a_scr, *h_scr, wide_rows):
    s = pl.program_id(0)
    j = pl.program_id(1)
    n = tn_ref[s]
    n_sub = (n + FFN_SUB - 1) // FFN_SUB

    @pl.when(jnp.logical_and(j == 0, n > 0))
    def _():
        o_ref[...] = jnp.zeros_like(o_ref)

    rows_ref = h_scr[0] if wide_rows else h_ref
    for k in range(1, FFN_R // FFN_SUB + 1):
        m = k * FFN_SUB

        if wide_rows:
            @pl.when(jnp.logical_and(n_sub == k, j == 0))
            def _(m=m):
                rows_ref[0:m, :] = h_ref[0:m, :].astype(BF16)

        @pl.when(n_sub == k)
        def _(m=m):
            hc = rows_ref[0:m, :]
            g_scr[0:m, :] = jnp.dot(hc, w1_ref[0].astype(BF16), preferred_element_type=F32)
            u = jnp.dot(hc, w3_ref[0].astype(BF16), preferred_element_type=F32)
            a_scr[0:m, :] = (_silu(g_scr[0:m, :]) * u).astype(BF16)
            o_ref[0:m, :] += jnp.dot(a_scr[0:m, :], w2_ref[0].astype(BF16), preferred_element_type=F32)


def _ffn(hs, tile_expert, tile_rows, tile_block, w1, w3, w2):
    n_tiles = tile_expert.shape[0]
    nj = FF // FFN_TF

    def jj(s, j, tn):
        return jnp.where(tn[s] > 0, j, nj - 1)

    once = pl.Buffered(1)
    wide_rows = hs.dtype == F32
    scratch = [pltpu.VMEM((FFN_R, FFN_TF), F32), pltpu.VMEM((FFN_R, FFN_TF), BF16)]
    if wide_rows:
        scratch.append(pltpu.VMEM((FFN_R, D), BF16))
    return pl.pallas_call(
        functools.partial(_ffn_kernel, wide_rows=wide_rows),
        out_shape=jax.ShapeDtypeStruct((hs.shape[0], D), F32),
        grid_spec=pltpu.PrefetchScalarGridSpec(
            num_scalar_prefetch=3,
            grid=(n_tiles, nj),
            in_specs=[
                pl.BlockSpec((FFN_R, hs.shape[1]), lambda s, j, te, tn, tb: (tb[s], 0), pipeline_mode=once),
                pl.BlockSpec((1, D, FFN_TF), lambda s, j, te, tn, tb: (te[s], 0, jj(s, j, tn))),
                pl.BlockSpec((1, D, FFN_TF), lambda s, j, te, tn, tb: (te[s], 0, jj(s, j, tn))),
                pl.BlockSpec((1, FFN_TF, D), lambda s, j, te, tn, tb: (te[s], jj(s, j, tn), 0)),
            ],
            out_specs=pl.BlockSpec((FFN_R, D), lambda s, j, te, tn, tb: (tb[s], 0), pipeline_mode=once),
            scratch_shapes=scratch,
        ),
        compiler_params=_params(2),
        name="ffn",
    )(tile_expert, tile_rows, tile_block, hs, w1, w3, w2)


DMA_UNROLL = 8


def _dispatch_kernel(pos_ref, pad_ref, h_ref, o_hbm, zero_scr, sem, zsem):
    i = pl.program_id(0)

    @pl.when(i == 0)
    def _():
        zero_scr[...] = jnp.zeros_like(zero_scr)
        for e in range(NE):
            first, count = pad_ref[e], pad_ref[NE + e]

            def zero_copy(r, first=first):
                return pltpu.make_async_copy(zero_scr, o_hbm.at[pl.ds(first + r, 1), :], zsem)

            def start(r, carry):
                zero_copy(r).start()
                return carry

            def wait(r, carry):
                zero_copy(r).wait()
                return carry

            lax.fori_loop(0, count, start, 0)
            lax.fori_loop(0, count, wait, 0)

    for r in range(TILE):
        for slot in range(2):
            p = pos_ref[2 * (i * TILE + r) + slot]
            pltpu.make_async_copy(h_ref.at[pl.ds(r, 1), :], o_hbm.at[pl.ds(p, 1), :], sem).start(priority=slot)
    for slot in range(2):
        pltpu.make_async_copy(h_ref, o_hbm.at[pl.ds(0, TILE), :], sem).wait()


def _dispatch(pos, pad, h2):
    return pl.pallas_call(
        _dispatch_kernel,
        out_shape=jax.ShapeDtypeStruct((MOE_TILES * FFN_R, D), F32),
        grid_spec=pltpu.PrefetchScalarGridSpec(
            num_scalar_prefetch=2,
            grid=(NTILES,),
            in_specs=[pl.BlockSpec((TILE, D), lambda i, pos, pad: (i, 0))],
            out_specs=pl.BlockSpec(memory_space=pl.ANY),
            scratch_shapes=[pltpu.VMEM((1, D), F32), pltpu.SemaphoreType.DMA, pltpu.SemaphoreType.DMA],
        ),
        compiler_params=_params(1),
        name="dispatch",
    )(pos, pad, h2)


def _ln2_kernel(x_ref, f_ref, mod_ref, g_ref, b_ref, o_ref):
    m = mod_ref[0]
    o_ref[...] = _layer_norm(ALPHA * x_ref[...] + m[G2:G2 + 1, :] * f_ref[...], g_ref[...], b_ref[...])


def _ln2(x1, f, mod_l, ln_g, ln_b):
    return pl.pallas_call(
        _ln2_kernel,
        out_shape=jax.ShapeDtypeStruct((T, D), F32),
        grid=(NTILES,),
        in_specs=[
            pl.BlockSpec((TILE, D), lambda i: (i, 0)),
            pl.BlockSpec((TILE, D), lambda i: (i, 0)),
            pl.BlockSpec((1, 6, D), lambda i: (_mod_row(i * TILE), 0, 0)),
            pl.BlockSpec((1, D), lambda i: (0, 0)),
            pl.BlockSpec((1, D), lambda i: (0, 0)),
        ],
        out_specs=pl.BlockSpec((TILE, D), lambda i: (i, 0)),
        compiler_params=_params(1),
        name="ln2",
    )(x1, f, mod_l, ln_g.reshape(1, D), ln_b.reshape(1, D))


def _combine_kernel(pos_ref, x_ref, route_ref, mod_ref, g_ref, b_ref, y_hbm, op_ref, os_ref, buf_a, buf_b, sem):
    i = pl.program_id(0)
    bufs = (buf_a, buf_b)

    def row_copy(tile, r, slot, dst, dsem):
        p = pos_ref[2 * (tile * TILE + r) + slot]
        return pltpu.make_async_copy(y_hbm.at[pl.ds(p, 1), :], dst.at[slot, pl.ds(r, 1), :], dsem)

    def wait_rows(dst, dsem):
        for slot in range(2):
            pltpu.make_async_copy(y_hbm.at[pl.ds(0, TILE), :], dst.at[slot], dsem).wait()

    @pl.when(i == 0)
    def _():
        def issue(r, carry):
            for slot in range(2):
                row_copy(0, r, slot, buf_a, sem.at[0]).start()
            return carry

        lax.fori_loop(0, TILE, issue, 0, unroll=DMA_UNROLL)

    for par in range(2):
        @pl.when(i % 2 == par)
        def _(par=par):
            cur, nxt = bufs[par], bufs[1 - par]
            wait_rows(cur, sem.at[par])
            nxt_tile = jnp.minimum(i + 1, NTILES - 1)
            for r in range(TILE):
                for slot in range(2):
                    row_copy(nxt_tile, r, slot, nxt, sem.at[1 - par]).start(priority=slot)
            route = route_ref[...]
            f = route[:, 2:3] * cur[0] + route[:, 3:4] * cur[1]
            m = mod_ref[0]
            out = _layer_norm(ALPHA * x_ref[...] + m[G2:G2 + 1, :] * f, g_ref[...], b_ref[...])

            @pl.when(i < NTILES_P)
            def _():
                op_ref[...] = out

            @pl.when(i >= NTILES_P)
            def _():
                os_ref[...] = out

            @pl.when(i == NTILES - 1)
            def _():
                wait_rows(nxt, sem.at[1 - par])


def _combine(pos, x1, route, mod_l, ln_g, ln_b, y):
    return pl.pallas_call(
        _combine_kernel,
        out_shape=(jax.ShapeDtypeStruct((TP, D), F32), jax.ShapeDtypeStruct((T - TP, D), F32)),
        grid_spec=pltpu.PrefetchScalarGridSpec(
            num_scalar_prefetch=1,
            grid=(NTILES,),
            in_specs=[
                pl.BlockSpec((TILE, D), lambda i, pos: (i, 0)),
                pl.BlockSpec((TILE, 128), lambda i, pos: (i, 0)),
                pl.BlockSpec((1, 6, D), lambda i, pos: (_mod_row(i * TILE), 0, 0)),
                pl.BlockSpec((1, D), lambda i, pos: (0, 0)),
                pl.BlockSpec((1, D), lambda i, pos: (0, 0)),
                pl.BlockSpec(memory_space=pl.ANY),
            ],
            out_specs=(pl.BlockSpec((TILE, D), lambda i, pos: (jnp.minimum(i, NTILES_P - 1), 0)),
                       pl.BlockSpec((TILE, D), lambda i, pos: (jnp.maximum(i - NTILES_P, 0), 0))),
            scratch_shapes=[pltpu.VMEM((2, TILE, D), F32), pltpu.VMEM((2, TILE, D), F32),
                            pltpu.SemaphoreType.DMA((2,))],
        ),
        compiler_params=_params(1),
        name="combine",
    )(pos, x1, route, mod_l, ln_g.reshape(1, D), ln_b.reshape(1, D), y)


MOE_TILES = 2 * T // FFN_R + NE


def _routing_tables(route, counts_f):
    eidx = route[:, 0:2].astype(jnp.int32)
    rank = route[:, 4:6].astype(jnp.int32)
    counts = counts_f[0, :NE].astype(jnp.int32)
    ntiles = (counts + FFN_R - 1) // FFN_R
    per_tile = (counts + jnp.maximum(ntiles, 1) - 1) // jnp.maximum(ntiles, 1)
    per_tile = (per_tile + FFN_SUB - 1) // FFN_SUB * FFN_SUB
    tend = jnp.cumsum(ntiles)
    tstart = tend - ntiles
    experts = jnp.arange(NE, dtype=jnp.int32)
    pick = lambda table: jnp.sum(jnp.where(eidx[..., None] == experts, table, 0), axis=-1)
    rows_e = pick(per_tile)
    local = sum((rank >= k * rows_e).astype(jnp.int32) for k in range(1, T // FFN_R))
    pos = ((pick(tstart) + local) * FFN_R + (rank - local * rows_e)).reshape(-1)
    tiles = jnp.arange(MOE_TILES, dtype=jnp.int32)
    total = tend[-1]
    t_eff = jnp.minimum(tiles, total - 1)
    tile_expert = jnp.minimum(jnp.sum((t_eff[:, None] >= tend[None, :]).astype(jnp.int32), axis=1), NE - 1)
    tile_rows = jnp.clip(counts[tile_expert] - (t_eff - tstart[tile_expert]) * per_tile[tile_expert],
                         0, per_tile[tile_expert])
    tile_rows = jnp.where(tiles < total, tile_rows, 0)
    last_rows = counts - (ntiles - 1) * per_tile
    pad = jnp.concatenate([(tend - 1) * FFN_R + last_rows, (-last_rows) % FFN_SUB]).astype(jnp.int32)
    return pos, pad, tile_expert.astype(jnp.int32), tile_rows.astype(jnp.int32), t_eff


def _grid_pos_emb():
    rows = L_S // GRID_W
    r = np.repeat(np.arange(rows, dtype=np.float64), GRID_W)
    col = np.tile(np.arange(GRID_W, dtype=np.float64), rows)
    quarter = D // 4
    freqs = 1.0 / (10000.0 ** (np.arange(quarter, dtype=np.float64) / quarter))

    def enc(p):
        ang = p[:, None] * freqs[None, :]
        return np.concatenate([np.sin(ang), np.cos(ang)], -1)

    return jnp.asarray(np.concatenate([enc(r), enc(col)], -1), dtype=F32)


def _cumsum_matrices():
    i = jnp.arange(TILE)
    same = (i[:, None] // CHUNK) == (i[None, :] // CHUNK)
    fwd = jnp.logical_and(same, i[None, :] <= i[:, None])
    bwd = jnp.logical_and(same, i[None, :] >= i[:, None])
    return fwd.astype(BF16), bwd.astype(BF16)


def kernel(x_prompt, x_sample, state_gla, c, c_ctx, w_mod, b_mod, w_in, w_a2, b_a2, gla_norm_g, conv_w, conv_b, conv_ln_g, conv_ln_b, w_pw2, b_pw2, w_out, ln1_g, ln1_b, ln2_g, ln2_b, ffn_w1, ffn_w3, ffn_w2, moe_w_router, moe_b_router, moe_w1, moe_w3, moe_w2):
    x = _assemble(x_prompt.reshape(TP, D), x_sample.reshape(T - TP, D), _grid_pos_emb())
    cond8 = jnp.zeros((8, D), F32).at[0].set(c_ctx).at[1:1 + NB_S].set(c)
    mod = _modulation(cond8, w_mod, b_mod)
    tri = _cumsum_matrices()
    w_bf = w_in.astype(BF16)
    w_main = w_bf[:, :, :MAIN_COLS]
    w_prep = jnp.concatenate(
        [w_bf[:, :, MAIN_COLS + 2 * RANK:],
         jnp.pad(w_bf[:, :, MAIN_COLS:MAIN_COLS + 2 * RANK], ((0, 0), (0, 0), (0, 128 - 2 * RANK)))], axis=-1)
    dense_tiles = T // FFN_R
    dense_meta = (jnp.zeros((dense_tiles,), jnp.int32), jnp.full((dense_tiles,), FFN_R, jnp.int32),
                  jnp.arange(dense_tiles, dtype=jnp.int32))

    states = []
    for l in range(DEPTH):
        mod_l = mod[l]
        z, a_lr, u = _inproj(x, mod_l, w_main, w_prep, l)
        u2 = _conv_module(u, conv_w[l], conv_b[l], conv_ln_g[l], conv_ln_b[l],
                          w_pw2[l].astype(BF16), b_pw2[l])
        o_dir, s_dir = [], []
        for d in range(2):
            wa2 = jnp.zeros((2 * RANK, QC), F32).at[d * RANK:(d + 1) * RANK].set(w_a2[l, d]).astype(BF16)
            o, s_new = _gla(z, a_lr, wa2, b_a2[l, d].reshape(1, QC), tri[d], state_gla[:, l], d)
            o_dir.append(o)
            s_dir.append(s_new)
        states.append(jnp.stack(s_dir, axis=1))
        is_moe = l % 2 == 1
        i = l // 2
        router = None
        if is_moe:
            wr_hi = moe_w_router[i].astype(BF16)
            wr_lo = (moe_w_router[i] - wr_hi.astype(F32)).astype(BF16)
            router = (jnp.pad(jnp.concatenate([wr_hi, wr_lo], axis=1), ((0, 0), (0, 128 - 2 * NE))),
                      jnp.pad(moe_b_router[i], (0, 128 - NE)).reshape(1, 128))
        outs = _mixout(o_dir[0], o_dir[1], z, u2, x, mod_l, gla_norm_g[l], w_out[l].astype(BF16),
                       ln1_g[l], ln1_b[l], router)
        if is_moe:
            x1, h2, route, counts = outs
            pos, pad, t_exp, t_rows, t_blk = _routing_tables(route, counts)
            hs = _dispatch(pos, pad, h2)
            y = _ffn(hs, t_exp, t_rows, t_blk, moe_w1[i], moe_w3[i], moe_w2[i])
            out_p, out_s = _combine(pos, x1, route, mod_l, ln2_g[l], ln2_b[l], y)
            x = jnp.concatenate([out_p, out_s], axis=0) if l + 1 < DEPTH else None
        else:
            x1, h2 = outs
            y = _ffn(h2, *dense_meta, ffn_w1[i][None], ffn_w3[i][None], ffn_w2[i][None])
            x = _ln2(x1, y, mod_l, ln2_g[l], ln2_b[l])

    if x is not None:
        out_p, out_s = x[:TP], x[TP:]
    return out_p.reshape(NB_P, L_P, D), out_s.reshape(NB_S, L_S, D), jnp.stack(states, axis=1)
```
